```python
import math
import jax, jax.numpy as jnp
from jax import lax
import numpy as np

D_MODEL = 1024
BATCH = 8
SEQ = 2048
DEPTH = 1
DEC_BATCH = 128
DEC_SEQ = 8
PAST_LEN = 16384
PAGE_SIZE = 128

MIX_WIDTH = D_MODEL
GM_HD = 128
GM_HEADS = (MIX_WIDTH // 2) // GM_HD
GM_WIDTH = GM_HEADS * GM_HD
GM_CHUNK = 128
ML_HD = 128
ML_HEADS = (MIX_WIDTH // 2) // ML_HD
ML_WIDTH = ML_HEADS * ML_HD
ML_CHUNK = 128
CONV_W = 4
D_FF = 2816
ALPHA = (2.0 * DEPTH) ** 0.25
BETA = (8.0 * DEPTH) ** -0.25
LN_EPS = 1e-5
N_IN = 2 * GM_WIDTH + 4 * ML_WIDTH + 2 * ML_HEADS
SPLIT_POINTS = [GM_WIDTH, 2 * GM_WIDTH, 2 * GM_WIDTH + ML_WIDTH, 2 * GM_WIDTH + 2 * ML_WIDTH,
                2 * GM_WIDTH + 3 * ML_WIDTH, 2 * GM_WIDTH + 4 * ML_WIDTH, 2 * GM_WIDTH + 4 * ML_WIDTH + ML_HEADS]
FG_OFF = 2 * GM_WIDTH + 4 * ML_WIDTH + ML_HEADS

kernel_name = "hymba_gmlp_mlstm_macaron_deepnorm_step"


def layer_norm(x, g, b):
    xf = x.astype(jnp.float32)
    mu = jnp.mean(xf, -1, keepdims=True)
    var = jnp.mean(jnp.square(xf - mu), -1, keepdims=True)
    return ((xf - mu) * lax.rsqrt(var + LN_EPS) * g + b).astype(x.dtype)


def head_rms(x, g):
    xf = x.astype(jnp.float32)
    return (xf * lax.rsqrt(jnp.mean(jnp.square(xf), -1, keepdims=True) + LN_EPS) * g).astype(x.dtype)


def swiglu(x, wg, wu, wd):
    return (jax.nn.silu(x @ wg) * (x @ wu)) @ wd


def causal_dwconv(x_ext, w, b, T):
    return sum(x_ext[:, j:j + T] * w[j] for j in range(CONV_W)) + b


def chunk_gmlp(u, v, ln_g, ln_b, w_s, b_s):
    B, T, H, Dh = v.shape
    vn = layer_norm(v, ln_g, ln_b)
    L = min(GM_CHUNK, T)
    NC = -(-T // L)
    pad = NC * L - T
    vp = jnp.pad(vn, ((0, 0), (0, pad), (0, 0), (0, 0))).reshape(B, NC, L, H, Dh)
    ws = jnp.tril(w_s[:, :L, :L])
    mixed = jnp.einsum('hts,bcshd->bcthd', ws, vp) + b_s[:, :L].T[:, :, None]
    mixed = mixed.reshape(B, NC * L, H, Dh)[:, :T]
    return u * mixed, vn


def mlstm_chunkwise(q, k, v, ig, lf, C0, n0, m0):
    B, T, H, D = q.shape
    f32 = jnp.float32
    L = math.gcd(ML_CHUNK, T)
    NC = T // L

    def to_chunks(a):
        return jnp.moveaxis(a.astype(f32).reshape((B, NC, L) + a.shape[2:]), 1, 0)

    xs = (to_chunks(q), to_chunks(k), to_chunks(v), to_chunks(ig), to_chunks(lf))
    causal = jnp.tril(jnp.ones((L, L), bool))

    def step(carry, inp):
        C, n, m = carry
        qc, kc, vc, ic, fc = inp
        bcum = jnp.cumsum(fc, axis=1)
        bt = bcum.transpose(0, 2, 1)
        it = ic.transpose(0, 2, 1)
        dlog = jnp.where(causal, bt[..., :, None] - bt[..., None, :] + it[..., None, :], -jnp.inf)
        inter = bt + m[..., None]
        m_t = jnp.maximum(inter, jnp.max(dlog, -1))
        w_intra = jnp.exp(dlog - m_t[..., None])
        w_inter = jnp.exp(inter - m_t).transpose(0, 2, 1)
        s = jnp.einsum('bthd,bshd->bhts', qc, kc) * w_intra
        num = jnp.einsum('bhts,bshd->bthd', s, vc) + w_inter[..., None] * jnp.einsum('bthd,bhde->bthe', qc, C)
        den = jnp.sum(s, -1).transpose(0, 2, 1) + w_inter * jnp.einsum('bthd,bhd->bth', qc, n)
        floor = jnp.exp(-m_t).transpose(0, 2, 1)
        h = num / jnp.maximum(jnp.abs(den), floor)[..., None]
        m_new = m_t[..., -1]
        dec = jnp.exp(inter[..., -1] - m_new)
        w_k = jnp.exp(bcum[:, -1:, :] - bcum + ic - m_new[:, None, :])
        C_new = dec[..., None, None] * C + jnp.einsum('bsh,bshd,bshe->bhde', w_k, kc, vc)
        n_new = dec[..., None] * n + jnp.einsum('bsh,bshd->bhd', w_k, kc)
        return (C_new, n_new, m_new), h

    (C, n, m), hs = lax.scan(step, (C0.astype(f32), n0.astype(f32), m0.astype(f32)), xs)
    h = jnp.moveaxis(hs, 0, 1).reshape(B, T, H, D)
    return h, C, n, m


def decoder_layer(x, conv_buf, C0, n0, m0, prm):
    (f1_wg, f1_wu, f1_wd, ln1_g, ln1_b, w_in, b_in, gm_ln_g, gm_ln_b, gm_ws, gm_bs,
     conv_w, conv_b, gm_out_g, ml_out_g, w_out, ln2_g, ln2_b,
     f2_wg, f2_wu, f2_wd, ln3_g, ln3_b) = prm
    B, T, _ = x.shape
    dt = x.dtype
    x = layer_norm(ALPHA * x + 0.5 * swiglu(x, f1_wg, f1_wu, f1_wd), ln1_g, ln1_b)
    z = x @ w_in + b_in
    u, v, q, k, vm, o, ig, fg = jnp.split(z, SPLIT_POINTS, axis=-1)
    u = jax.nn.gelu(u).reshape(B, T, GM_HEADS, GM_HD)
    v = jax.nn.gelu(v).reshape(B, T, GM_HEADS, GM_HD)
    gm, vn = chunk_gmlp(u, v, gm_ln_g, gm_ln_b, gm_ws, gm_bs)
    gm_rows = vn[:, ((T - 1) // GM_CHUNK) * GM_CHUNK:]
    qk = jnp.concatenate([q, k], -1)
    qk_ext = jnp.concatenate([conv_buf.astype(qk.dtype), qk], 1)
    new_buf = qk_ext[:, T:]
    qk = jax.nn.silu(causal_dwconv(qk_ext, conv_w, conv_b, T))
    q, k = jnp.split(qk, 2, axis=-1)
    to_heads = lambda a: a.reshape(B, T, ML_HEADS, ML_HD)
    h_ml, C, n, m = mlstm_chunkwise(to_heads(q), to_heads(k) * (ML_HD ** -0.5), to_heads(vm),
                                    ig, jax.nn.log_sigmoid(fg.astype(jnp.float32)), C0, n0, m0)
    h_ml = jax.nn.sigmoid(to_heads(o)) * h_ml.astype(dt)
    mix = jnp.concatenate([head_rms(gm, gm_out_g), head_rms(h_ml, ml_out_g)], axis=2).reshape(B, T, MIX_WIDTH)
    x = layer_norm(ALPHA * x + mix @ w_out, ln2_g, ln2_b)
    x = layer_norm(ALPHA * x + 0.5 * swiglu(x, f2_wg, f2_wu, f2_wd), ln3_g, ln3_b)
    return x, (gm_rows, new_buf, C.astype(C0.dtype), n.astype(n0.dtype), m.astype(m0.dtype))


def setup_inputs(seed: int = 0) -> dict:
    key = jax.random.key(seed)
    ks = iter(jax.random.split(key, 40))
    nrm = lambda shape, scale: scale * jax.random.normal(next(ks), shape, jnp.float32)
    P = DEPTH
    x_prompt = nrm((BATCH, SEQ, D_MODEL), 1.0)
    x_sample = nrm((DEC_BATCH, DEC_SEQ, D_MODEL), 1.0)
    state_conv = nrm((P, DEC_BATCH, CONV_W - 1, 2 * ML_WIDTH), 1.0)
    state_C = nrm((P, DEC_BATCH, ML_HEADS, ML_HD, ML_HD), 0.05)
    state_n = nrm((P, DEC_BATCH, ML_HEADS, ML_HD), 0.1)
    state_m = nrm((P, DEC_BATCH, ML_HEADS), 1.0)
    ffn1_wg = nrm((P, D_MODEL, D_FF), D_MODEL ** -0.5)
    ffn1_wu = nrm((P, D_MODEL, D_FF), D_MODEL ** -0.5)
    ffn1_wd = nrm((P, D_FF, D_MODEL), BETA * D_FF ** -0.5)
    ln1_g = 1.0 + nrm((P, D_MODEL), 0.02)
    ln1_b = nrm((P, D_MODEL), 0.02)
    w_in = nrm((P, D_MODEL, N_IN), D_MODEL ** -0.5)
    b_in = nrm((P, N_IN), 0.02)
    b_in = b_in.at[:, FG_OFF:FG_OFF + ML_HEADS].add(jnp.linspace(3.0, 6.0, ML_HEADS, dtype=jnp.float32))
    gm_ln_g = 1.0 + nrm((P, GM_HEADS, GM_HD), 0.02)
    gm_ln_b = nrm((P, GM_HEADS, GM_HD), 0.02)
    gm_ws = nrm((P, GM_HEADS, GM_CHUNK, GM_CHUNK), GM_CHUNK ** -0.5)
    gm_bs = 1.0 + nrm((P, GM_HEADS, GM_CHUNK), 0.02)
    conv_w = nrm((P, CONV_W, 2 * ML_WIDTH), CONV_W ** -0.5)
    conv_b = nrm((P, 2 * ML_WIDTH), 0.02)
    gm_out_g = 1.0 + nrm((P, GM_HEADS, GM_HD), 0.02)
    ml_out_g = 1.0 + nrm((P, ML_HEADS, ML_HD), 0.02)
    w_out = nrm((P, MIX_WIDTH, D_MODEL), BETA * MIX_WIDTH ** -0.5)
    ln2_g = 1.0 + nrm((P, D_MODEL), 0.02)
    ln2_b = nrm((P, D_MODEL), 0.02)
    ffn2_wg = nrm((P, D_MODEL, D_FF), D_MODEL ** -0.5)
    ffn2_wu = nrm((P, D_MODEL, D_FF), D_MODEL ** -0.5)
    ffn2_wd = nrm((P, D_FF, D_MODEL), BETA * D_FF ** -0.5)
    ln3_g = 1.0 + nrm((P, D_MODEL), 0.02)
    ln3_b = nrm((P, D_MODEL), 0.02)
    return {"x_prompt": x_prompt, "x_sample": x_sample, "state_conv": state_conv, "state_C": state_C,
            "state_n": state_n, "state_m": state_m,
            "ffn1_wg": ffn1_wg, "ffn1_wu": ffn1_wu, "ffn1_wd": ffn1_wd, "ln1_g": ln1_g, "ln1_b": ln1_b,
            "w_in": w_in, "b_in": b_in, "gm_ln_g": gm_ln_g, "gm_ln_b": gm_ln_b, "gm_ws": gm_ws,
            "gm_bs": gm_bs, "conv_w": conv_w, "conv_b": conv_b, "gm_out_g": gm_out_g,
            "ml_out_g": ml_out_g, "w_out": w_out, "ln2_g": ln2_g, "ln2_b": ln2_b,
            "ffn2_wg": ffn2_wg, "ffn2_wu": ffn2_wu, "ffn2_wd": ffn2_wd, "ln3_g": ln3_g, "ln3_b": ln3_b}


def reference(x_prompt, x_sample, state_conv, state_C, state_n, state_m,
              ffn1_wg, ffn1_wu, ffn1_wd, ln1_g, ln1_b, w_in, b_in, gm_ln_g, gm_ln_b, gm_ws,
              gm_bs, conv_w, conv_b, gm_out_g, ml_out_g, w_out, ln2_g, ln2_b,
              ffn2_wg, ffn2_wu, ffn2_wd, ln3_g, ln3_b):
    weights = (ffn1_wg, ffn1_wu, ffn1_wd, ln1_g, ln1_b, w_in, b_in, gm_ln_g, gm_ln_b, gm_ws, gm_bs,
               conv_w, conv_b, gm_out_g, ml_out_g, w_out, ln2_g, ln2_b,
               ffn2_wg, ffn2_wu, ffn2_wd, ln3_g, ln3_b)
    bp = x_prompt.shape[0]
    dt = x_prompt.dtype
    zc = jnp.zeros((bp, CONV_W - 1, 2 * ML_WIDTH), dt)
    zC = jnp.zeros((bp, ML_HEADS, ML_HD, ML_HD), dt)
    zn = jnp.zeros((bp, ML_HEADS, ML_HD), dt)
    zm = jnp.zeros((bp, ML_HEADS), dt)
    y_prompt, y_sample = x_prompt, x_sample
    outs_p, outs_s = [], []
    for l in range(DEPTH):
        prm = tuple(w[l] for w in weights)
        y_prompt, st_p = decoder_layer(y_prompt, zc, zC, zn, zm, prm)
        y_sample, st_s = decoder_layer(y_sample, state_conv[l], state_C[l], state_n[l], state_m[l], prm)
        outs_p.append(st_p)
        outs_s.append(st_s)
    gmv_p, conv_p, C_p, n_p, m_p = [jnp.stack(a) for a in zip(*outs_p)]
    gmv_s, conv_s, C_s, n_s, m_s = [jnp.stack(a) for a in zip(*outs_s)]
    return (y_prompt, y_sample, gmv_p, gmv_s, conv_p, conv_s, C_p, C_s, n_p, n_s, m_p, m_s)
```

```python
import functools

import jax
import jax.numpy as jnp
from jax import lax
from jax.experimental import pallas as pl
from jax.experimental.pallas import tpu as pltpu

F32 = jnp.float32
BF16 = jnp.bfloat16

D_MODEL = 1024
D_FF = 2816
HEAD_DIM = 128
N_HEADS = 4
GROUP_W = N_HEADS * HEAD_DIM
CHUNK = 128
CONV_W = 4
DEC_SEQ = 8
SEQ_PER_TILE = CHUNK // DEC_SEQ
N_MAIN = 6 * GROUP_W
GATE_W = 128
FG_LANE = 64
FF_CHUNK = 256
N_FF_CHUNKS = D_FF // FF_CHUNK
ALPHA = 2.0 ** 0.25
LN_EPS = 1e-5
TOKEN_TILE = 512
VMEM_LIMIT_BYTES = 60 * 1024 * 1024

_NT = (((1,), (1,)), ((), ()))
_TN = (((0,), (0,)), ((), ()))


def _layer_norm(x, g, b):
    mu = jnp.mean(x, axis=-1, keepdims=True)
    xc = x - mu
    var = jnp.mean(xc * xc, axis=-1, keepdims=True)
    return xc * lax.rsqrt(var + LN_EPS) * g + b


def _head_rms(x, g):
    return x * lax.rsqrt(jnp.mean(x * x, axis=-1, keepdims=True) + LN_EPS) * g


def _silu(x):
    return x * jax.nn.sigmoid(x)


def _log_sigmoid(x):
    return jnp.minimum(x, 0.0) - jnp.log1p(jnp.exp(-jnp.abs(x)))


def _swiglu_into(xb_ref, wg_ref, wu_ref, wd_ref, acc_ref):
    acc_ref[...] = jnp.zeros_like(acc_ref)

    def body(c, carry):
        xb = xb_ref[...]
        g = jnp.dot(xb, wg_ref[c], preferred_element_type=F32)
        u = jnp.dot(xb, wu_ref[c], preferred_element_type=F32)
        h = (_silu(g) * u).astype(BF16)
        acc_ref[...] += jnp.dot(h, wd_ref[c], preferred_element_type=F32)
        return carry

    lax.fori_loop(0, N_FF_CHUNKS, body, 0)


def _stage_a_kernel(x_ref, wg_ref, wu_ref, wd_ref, lng_ref, lnb_ref, wmain_ref, bmain_ref, wgate_ref, bgate_ref,
                    vlng_ref, vlnb_ref, x1_ref, zz_ref, gate_ref, xb_ref, acc_ref):
    x = x_ref[...]
    xb_ref[...] = x.astype(BF16)
    _swiglu_into(xb_ref, wg_ref, wu_ref, wd_ref, acc_ref)
    x1 = _layer_norm(ALPHA * x + 0.5 * acc_ref[...], lng_ref[...], lnb_ref[...])
    x1_ref[...] = x1
    xb_ref[...] = x1.astype(BF16)
    for grp in range(N_MAIN // GROUP_W):
        cols = slice(grp * GROUP_W, (grp + 1) * GROUP_W)
        z = jnp.dot(xb_ref[...], wmain_ref[:, cols], preferred_element_type=F32) + bmain_ref[:, cols]
        if grp == 0:
            zz_ref[:, cols] = jax.nn.gelu(z)
        elif grp == 1:
            z = jax.nn.gelu(z)
            for h in range(N_HEADS):
                hc = slice(h * HEAD_DIM, (h + 1) * HEAD_DIM)
                zz_ref[:, GROUP_W + h * HEAD_DIM:GROUP_W + (h + 1) * HEAD_DIM] = _layer_norm(
                    z[:, hc], vlng_ref[:, hc], vlnb_ref[:, hc])
        else:
            zz_ref[:, cols] = z
    gate_ref[...] = jnp.dot(xb_ref[...], wgate_ref[...], preferred_element_type=F32) + bgate_ref[...]


def _const_spec(shape):
    nd = len(shape)
    return pl.BlockSpec(shape, lambda i, _nd=nd: (0,) * _nd, pipeline_mode=pl.Buffered(1))


def _stage_a(x, wg, wu, wd, lng, lnb, wmain, bmain, wgate, bgate, vlng, vlnb):
    n = x.shape[0]
    tm = TOKEN_TILE
    row = lambda w: pl.BlockSpec((tm, w), lambda i: (i, 0))
    return pl.pallas_call(
        _stage_a_kernel,
        grid=(n // tm,),
        in_specs=[row(D_MODEL),
                  _const_spec(wg.shape), _const_spec(wu.shape), _const_spec(wd.shape),
                  _const_spec(lng.shape), _const_spec(lnb.shape),
                  _const_spec(wmain.shape), _const_spec(bmain.shape),
                  _const_spec(wgate.shape), _const_spec(bgate.shape),
                  _const_spec(vlng.shape), _const_spec(vlnb.shape)],
        out_specs=[row(D_MODEL), row(N_MAIN), row(GATE_W)],
        out_shape=[jax.ShapeDtypeStruct((n, D_MODEL), F32),
                   jax.ShapeDtypeStruct((n, N_MAIN), F32),
                   jax.ShapeDtypeStruct((n, GATE_W), F32)],
        scratch_shapes=[pltpu.VMEM((tm, D_MODEL), BF16), pltpu.VMEM((tm, D_MODEL), F32)],
        compiler_params=pltpu.CompilerParams(dimension_semantics=("arbitrary",),
                                             vmem_limit_bytes=VMEM_LIMIT_BYTES),
        name="stage_a",
    )(x, wg, wu, wd, lng, lnb, wmain, bmain, wgate, bgate, vlng, vlnb)


def _stage_c_kernel(mix_ref, x1_ref, wout_ref, ln2g_ref, ln2b_ref, wg_ref, wu_ref, wd_ref, ln3g_ref, ln3b_ref,
                    y_ref, xb_ref, acc_ref):
    proj = jnp.dot(mix_ref[...].astype(BF16), wout_ref[...], preferred_element_type=F32)
    y = _layer_norm(ALPHA * x1_ref[...] + proj, ln2g_ref[...], ln2b_ref[...])
    y_ref[...] = y
    xb_ref[...] = y.astype(BF16)
    _swiglu_into(xb_ref, wg_ref, wu_ref, wd_ref, acc_ref)
    y_ref[...] = _layer_norm(ALPHA * y_ref[...] + 0.5 * acc_ref[...], ln3g_ref[...], ln3b_ref[...])


def _stage_c(mix, x1, wout, ln2g, ln2b, wg, wu, wd, ln3g, ln3b):
    n = x1.shape[0]
    tm = TOKEN_TILE
    row = lambda w: pl.BlockSpec((tm, w), lambda i: (i, 0))
    return pl.pallas_call(
        _stage_c_kernel,
        grid=(n // tm,),
        in_specs=[row(D_MODEL), row(D_MODEL),
                  _const_spec(wout.shape), _const_spec(ln2g.shape), _const_spec(ln2b.shape),
                  _const_spec(wg.shape), _const_spec(wu.shape), _const_spec(wd.shape),
                  _const_spec(ln3g.shape), _const_spec(ln3b.shape)],
        out_specs=row(D_MODEL),
        out_shape=jax.ShapeDtypeStruct((n, D_MODEL), F32),
        scratch_shapes=[pltpu.VMEM((tm, D_MODEL), BF16), pltpu.VMEM((tm, D_MODEL), F32)],
        compiler_params=pltpu.CompilerParams(dimension_semantics=("arbitrary",),
                                             vmem_limit_bytes=VMEM_LIMIT_BYTES),
        name="stage_c",
    )(mix, x1, wout, ln2g, ln2b, wg, wu, wd, ln3g, ln3b)


def _head_cols(h, base=0):
    return slice(base + h * HEAD_DIM, base + (h + 1) * HEAD_DIM)


def _gmlp_heads(zz_ref, ws_ref, bs_ref, gmg_ref, mix_ref, mask):
    for h in range(N_HEADS):
        u = zz_ref[:, _head_cols(h)]
        vn = zz_ref[:, _head_cols(h, GROUP_W)]
        w = jnp.where(mask, ws_ref[h], 0.0).astype(BF16)
        mixed = jnp.dot(w, vn.astype(BF16), preferred_element_type=F32) + bs_ref[:, h:h + 1]
        mix_ref[:, _head_cols(h)] = _head_rms(u * mixed, gmg_ref[:, _head_cols(h)])


def _intra_chunk(q, k, v, bcol, brow, irow, mcol, mask):
    dlog = jnp.where(mask, bcol - brow + irow, -jnp.inf)
    inter = bcol + mcol
    m_t = jnp.maximum(inter, jnp.max(dlog, axis=-1, keepdims=True))
    w_intra = jnp.exp(dlog - m_t)
    w_inter = jnp.exp(inter - m_t)
    s = lax.dot_general(q.astype(BF16), k.astype(BF16), _NT, preferred_element_type=F32) * w_intra
    sv = jnp.dot(s.astype(BF16), v.astype(BF16), preferred_element_type=F32)
    ssum = jnp.sum(s, axis=-1, keepdims=True)
    return m_t, w_inter, sv, ssum


def _conv_taps(x, prev_fn, cw_ref, cb_ref, row_in_seq):
    acc = x * cw_ref[CONV_W - 1:CONV_W, :] + cb_ref[...]
    for j in range(1, CONV_W):
        shifted = jnp.where(row_in_seq < j, prev_fn(j), pltpu.roll(x, j, 0))
        acc = acc + shifted * cw_ref[CONV_W - 1 - j:CONV_W - j, :]
    return acc


def _mix_prompt_kernel(zz_ref, gate_ref, ws_ref, bs_ref, cw_ref, cb_ref, gmg_ref, mlg_ref,
                       mix_ref, c_ref, n_ref, m_ref, tail_ref):
    @pl.when(pl.program_id(1) == 0)
    def _():
        c_ref[...] = jnp.zeros_like(c_ref)
        n_ref[...] = jnp.zeros_like(n_ref)
        m_ref[...] = jnp.zeros_like(m_ref)
        tail_ref[...] = jnp.zeros_like(tail_ref)

    row = lax.broadcasted_iota(jnp.int32, (CHUNK, CHUNK), 0)
    col = lax.broadcasted_iota(jnp.int32, (CHUNK, CHUNK), 1)
    causal = col <= row

    _gmlp_heads(zz_ref, ws_ref, bs_ref, gmg_ref, mix_ref, causal)

    x = zz_ref[:, 2 * GROUP_W:4 * GROUP_W]
    row_w = lax.broadcasted_iota(jnp.int32, (CHUNK, 2 * GROUP_W), 0)
    tail = tail_ref[...]

    def prev_rows(j):
        first = pltpu.roll(tail, j, 0)
        return jnp.concatenate([first, x[8:]], axis=0)

    qk = _silu(_conv_taps(x, prev_rows, cw_ref, cb_ref, row_w))
    tail_ref[...] = x[CHUNK - 8:]

    gates = gate_ref[...]
    bcum = jnp.dot(causal.astype(F32), _log_sigmoid(gates), precision=lax.Precision.HIGHEST,
                   preferred_element_type=F32)
    gates_t = gates.T
    bcum_t = bcum.T
    m_row = m_ref[0]
    lane = lax.broadcasted_iota(jnp.int32, m_row.shape, 1)
    scale = HEAD_DIM ** -0.5

    for h in range(N_HEADS):
        q = qk[:, _head_cols(h)]
        k = qk[:, _head_cols(h, GROUP_W)] * scale
        v = zz_ref[:, _head_cols(h, 4 * GROUP_W)]
        o = zz_ref[:, _head_cols(h, 5 * GROUP_W)]
        icol = gates[:, h:h + 1]
        bcol = bcum[:, FG_LANE + h:FG_LANE + h + 1]
        irow = gates_t[h:h + 1, :]
        brow = bcum_t[FG_LANE + h:FG_LANE + h + 1, :]
        m_prev = m_row[:, h:h + 1]
        c_prev = c_ref[0, h]
        n_prev = n_ref[0, h:h + 1, :]

        m_t, w_inter, sv, ssum = _intra_chunk(q, k, v, bcol, brow, irow, m_prev, causal)
        qc = jnp.dot(q.astype(BF16), c_prev.astype(BF16), preferred_element_type=F32)
        num = sv + w_inter * qc
        den = ssum + w_inter * jnp.sum(q * n_prev, axis=-1, keepdims=True)
        hid = num / jnp.maximum(jnp.abs(den), jnp.exp(-m_t))
        mix_ref[:, _head_cols(h, GROUP_W)] = _head_rms(jax.nn.sigmoid(o) * hid, mlg_ref[:, _head_cols(h)])

        m_new = m_t[CHUNK - 1:CHUNK, :]
        b_last = bcol[CHUNK - 1:CHUNK, :]
        dec = jnp.exp(b_last + m_prev - m_new)
        kw = k * jnp.exp(b_last - bcol + icol - m_new)
        c_ref[0, h] = dec * c_prev + lax.dot_general(kw.astype(BF16), v.astype(BF16), _TN,
                                                     preferred_element_type=F32)
        n_ref[0, h:h + 1, :] = dec * n_prev + jnp.sum(kw, axis=0, keepdims=True)
        m_row = jnp.where(lane == h, m_new, m_row)

    m_ref[0] = m_row


def _mix_prompt(zz, gates, ws, bs_cols, cw, cb, gmg, mlg, batch, seq):
    n_chunks = seq // CHUNK
    row = lambda w: pl.BlockSpec((CHUNK, w), lambda b, c: (b * n_chunks + c, 0))
    const = lambda shape: pl.BlockSpec(shape, lambda b, c, _nd=len(shape): (0,) * _nd)
    return pl.pallas_call(
        _mix_prompt_kernel,
        grid=(batch, n_chunks),
        in_specs=[row(N_MAIN), row(GATE_W), const(ws.shape), const(bs_cols.shape), const(cw.shape),
                  const(cb.shape), const(gmg.shape), const(mlg.shape)],
        out_specs=[row(D_MODEL),
                   pl.BlockSpec((1, N_HEADS, HEAD_DIM, HEAD_DIM), lambda b, c: (b, 0, 0, 0)),
                   pl.BlockSpec((1, N_HEADS, HEAD_DIM), lambda b, c: (b, 0, 0)),
                   pl.BlockSpec((1, 1, GATE_W), lambda b, c: (b, 0, 0))],
        out_shape=[jax.ShapeDtypeStruct((batch * seq, D_MODEL), F32),
                   jax.ShapeDtypeStruct((batch, N_HEADS, HEAD_DIM, HEAD_DIM), F32),
                   jax.ShapeDtypeStruct((batch, N_HEADS, HEAD_DIM), F32),
                   jax.ShapeDtypeStruct((batch, 1, GATE_W), F32)],
        scratch_shapes=[pltpu.VMEM((8, 2 * GROUP_W), F32)],
        compiler_params=pltpu.CompilerParams(dimension_semantics=("arbitrary", "arbitrary")),
        name="mix_prompt",
    )(zz, gates, ws, bs_cols, cw, cb, gmg, mlg)


def _mix_sample_kernel(zz_ref, gate_ref, ws_ref, bs_ref, cw_ref, cb_ref, gmg_ref, mlg_ref, prev_ref,
                       c0_ref, n0_ref, m0_ref, mix_ref, c_ref, n_ref, mt_ref):
    row = lax.broadcasted_iota(jnp.int32, (CHUNK, CHUNK), 0)
    col = lax.broadcasted_iota(jnp.int32, (CHUNK, CHUNK), 1)
    seq_shift = DEC_SEQ.bit_length() - 1
    mask = (col <= row) & ((col >> seq_shift) == (row >> seq_shift))
    last_sel = (col == (row | (DEC_SEQ - 1))).astype(F32)

    _gmlp_heads(zz_ref, ws_ref, bs_ref, gmg_ref, mix_ref, mask)

    x = zz_ref[:, 2 * GROUP_W:4 * GROUP_W]
    row_w = lax.broadcasted_iota(jnp.int32, (CHUNK, 2 * GROUP_W), 0) & (DEC_SEQ - 1)
    prev = prev_ref[...]
    qk = _silu(_conv_taps(x, lambda j: pltpu.roll(prev, CHUNK - DEC_SEQ + j, 0), cw_ref, cb_ref, row_w))

    gates = gate_ref[...]
    bcum = jnp.dot(mask.astype(F32), _log_sigmoid(gates), precision=lax.Precision.HIGHEST,
                   preferred_element_type=F32)
    gates_t = gates.T
    bcum_t = bcum.T
    lane = lax.broadcasted_iota(jnp.int32, (CHUNK, GATE_W), 1)
    mt_all = jnp.zeros((CHUNK, GATE_W), F32)
    scale = HEAD_DIM ** -0.5

    for h in range(N_HEADS):
        q = qk[:, _head_cols(h)]
        k = qk[:, _head_cols(h, GROUP_W)] * scale
        v = zz_ref[:, _head_cols(h, 4 * GROUP_W)]
        o = zz_ref[:, _head_cols(h, 5 * GROUP_W)]
        icol = gates[:, h:h + 1]
        bcol = bcum[:, FG_LANE + h:FG_LANE + h + 1]
        irow = gates_t[h:h + 1, :]
        brow = bcum_t[FG_LANE + h:FG_LANE + h + 1, :]
        m_prev = m0_ref[:, h:h + 1]

        m_t, w_inter, sv, ssum = _intra_chunk(q, k, v, bcol, brow, irow, m_prev, mask)
        qb = q.astype(BF16)
        qc_rows, qn_rows = [], []
        for s in range(SEQ_PER_TILE):
            rows = slice(s * DEC_SEQ, (s + 1) * DEC_SEQ)
            qc_rows.append(jnp.dot(qb[rows], c0_ref[s, h].astype(BF16), preferred_element_type=F32))
            qn_rows.append(jnp.sum(q[rows] * n0_ref[s, h:h + 1, :], axis=-1, keepdims=True))
        qc = jnp.concatenate(qc_rows, axis=0)
        qn = jnp.concatenate(qn_rows, axis=0)
        num = sv + w_inter * qc
        den = ssum + w_inter * qn
        hid = num / jnp.maximum(jnp.abs(den), jnp.exp(-m_t))
        mix_ref[:, _head_cols(h, GROUP_W)] = _head_rms(jax.nn.sigmoid(o) * hid, mlg_ref[:, _head_cols(h)])

        packed = jnp.where(lane == 0, m_t, jnp.where(lane == 1, bcol, 0.0))
        lastv = jnp.dot(last_sel, packed, precision=lax.Precision.HIGHEST, preferred_element_type=F32)
        m_new = lastv[:, 0:1]
        b_last = lastv[:, 1:2]
        dec = jnp.exp(b_last + m_prev - m_new)
        kw = k * jnp.exp(b_last - bcol + icol - m_new)
        kwb = kw.astype(BF16)
        vb = v.astype(BF16)
        for s in range(SEQ_PER_TILE):
            rows = slice(s * DEC_SEQ, (s + 1) * DEC_SEQ)
            dec_s = dec[s * DEC_SEQ:s * DEC_SEQ + 1, :]
            c_ref[s, h] = dec_s * c0_ref[s, h] + lax.dot_general(kwb[rows], vb[rows], _TN,
                                                                preferred_element_type=F32)
            n_ref[s, h:h + 1, :] = dec_s * n0_ref[s, h:h + 1, :] + jnp.sum(kw[rows], axis=0, keepdims=True)
        mt_all = jnp.where(lane == h, m_t, mt_all)

    mt_ref[...] = mt_all


def _mix_sample(zz, gates, ws_t, bs_cols, cw, cb, gmg, mlg, prev, c0, n0, m0_tok):
    n = zz.shape[0]
    n_tiles = n // CHUNK
    row = lambda w: pl.BlockSpec((CHUNK, w), lambda i: (i, 0))
    const = lambda shape: pl.BlockSpec(shape, lambda i, _nd=len(shape): (0,) * _nd)
    c_spec = pl.BlockSpec((SEQ_PER_TILE, N_HEADS, HEAD_DIM, HEAD_DIM), lambda i: (i, 0, 0, 0))
    n_spec = pl.BlockSpec((SEQ_PER_TILE, N_HEADS, HEAD_DIM), lambda i: (i, 0, 0))
    return pl.pallas_call(
        _mix_sample_kernel,
        grid=(n_tiles,),
        in_specs=[row(N_MAIN), row(GATE_W), const(ws_t.shape), const(bs_cols.shape), const(cw.shape),
                  const(cb.shape), const(gmg.shape), const(mlg.shape), row(2 * GROUP_W),
                  c_spec, n_spec, row(N_HEADS)],
        out_specs=[row(D_MODEL), c_spec, n_spec, row(GATE_W)],
        out_shape=[jax.ShapeDtypeStruct((n, D_MODEL), F32),
                   jax.ShapeDtypeStruct(c0.shape, F32),
                   jax.ShapeDtypeStruct(n0.shape, F32),
                   jax.ShapeDtypeStruct((n, GATE_W), F32)],
        compiler_params=pltpu.CompilerParams(dimension_semantics=("arbitrary",),
                                             vmem_limit_bytes=VMEM_LIMIT_BYTES),
        name="mix_sample",
    )(zz, gates, ws_t, bs_cols, cw, cb, gmg, mlg, prev, c0, n0, m0_tok)


def _ffn_weights(wg, wu, wd):
    to_blocks = lambda w: jnp.transpose(w.reshape(D_MODEL, N_FF_CHUNKS, FF_CHUNK), (1, 0, 2)).astype(BF16)
    return to_blocks(wg), to_blocks(wu), wd.reshape(N_FF_CHUNKS, FF_CHUNK, D_MODEL).astype(BF16)


def _gate_columns(w):
    out = jnp.zeros(w.shape[:-1] + (GATE_W,), w.dtype)
    out = out.at[..., 0:N_HEADS].set(w[..., N_MAIN:N_MAIN + N_HEADS])
    return out.at[..., FG_LANE:FG_LANE + N_HEADS].set(w[..., N_MAIN + N_HEADS:N_MAIN + 2 * N_HEADS])


def kernel(x_prompt, x_sample, state_conv, state_C, state_n, state_m, ffn1_wg, ffn1_wu, ffn1_wd, ln1_g, ln1_b, w_in, b_in, gm_ln_g, gm_ln_b, gm_ws, gm_bs, conv_w, conv_b, gm_out_g, ml_out_g, w_out, ln2_g, ln2_b, ffn2_wg, ffn2_wu, ffn2_wd, ln3_g, ln3_b):
    depth = ffn1_wg.shape[0]
    bp, seq, _ = x_prompt.shape
    bs, dec_seq, _ = x_sample.shape
    assert dec_seq == DEC_SEQ and seq % CHUNK == 0 and (bs * dec_seq) % CHUNK == 0
    y_p = x_prompt.reshape(bp * seq, D_MODEL)
    y_s = x_sample.reshape(bs * dec_seq, D_MODEL)
    outs = []
    for l in range(depth):
        f1 = _ffn_weights(ffn1_wg[l], ffn1_wu[l], ffn1_wd[l])
        f2 = _ffn_weights(ffn2_wg[l], ffn2_wu[l], ffn2_wd[l])
        row = lambda a: a.reshape(1, -1)
        a_params = (*f1, row(ln1_g[l]), row(ln1_b[l]),
                    w_in[l][:, :N_MAIN].astype(BF16), row(b_in[l][:N_MAIN]),
                    _gate_columns(w_in[l]).astype(BF16), row(_gate_columns(b_in[l])),
                    row(gm_ln_g[l]), row(gm_ln_b[l]))
        c_params = (w_out[l].astype(BF16), row(ln2_g[l]), row(ln2_b[l]), *f2, row(ln3_g[l]), row(ln3_b[l]))
        mix_params = (conv_w[l], row(conv_b[l]), row(gm_out_g[l]), row(ml_out_g[l]))

        x1_p, zz_p, g_p = _stage_a(y_p, *a_params)
        x1_s, zz_s, g_s = _stage_a(y_s, *a_params)

        mix_p, c_p, n_p, m_p = _mix_prompt(zz_p, g_p, gm_ws[l], jnp.transpose(gm_bs[l]), *mix_params, bp, seq)

        ws_t = jnp.tile(gm_ws[l][:, :DEC_SEQ, :DEC_SEQ], (1, SEQ_PER_TILE, SEQ_PER_TILE))
        bs_t = jnp.tile(jnp.transpose(gm_bs[l][:, :DEC_SEQ]), (SEQ_PER_TILE, 1))
        prev = jnp.pad(state_conv[l], ((0, 0), (DEC_SEQ - (CONV_W - 1), 0), (0, 0))).reshape(bs * DEC_SEQ, 2 * GROUP_W)
        m0_tok = jnp.repeat(state_m[l], DEC_SEQ, axis=0)
        mix_s, c_s, n_s, mt_s = _mix_sample(zz_s, g_s, ws_t, bs_t, *mix_params, prev, state_C[l], state_n[l], m0_tok)

        y_p = _stage_c(mix_p, x1_p, *c_params)
        y_s = _stage_c(mix_s, x1_s, *c_params)

        zz_p3 = zz_p.reshape(bp, seq, N_MAIN)
        zz_s3 = zz_s.reshape(bs, dec_seq, N_MAIN)
        outs.append((
            zz_p3[:, seq - CHUNK:, GROUP_W:2 * GROUP_W].reshape(bp, CHUNK, N_HEADS, HEAD_DIM),
            zz_s3[:, :, GROUP_W:2 * GROUP_W].reshape(bs, dec_seq, N_HEADS, HEAD_DIM),
            zz_p3[:, seq - (CONV_W - 1):, 2 * GROUP_W:4 * GROUP_W],
            zz_s3[:, dec_seq - (CONV_W - 1):, 2 * GROUP_W:4 * GROUP_W],
            c_p, c_s, n_p, n_s,
            m_p[:, 0, :N_HEADS],
            mt_s.reshape(bs, dec_seq, GATE_W)[:, dec_seq - 1, :N_HEADS],
        ))
    stacked = [jnp.stack(a) for a in zip(*outs)]
    return (y_p.reshape(bp, seq, D_MODEL), y_s.reshape(bs, dec_seq, D_MODEL), *stacked)
```

```python
import functools

import jax
import jax.numpy as jnp
from jax import lax
from jax.experimental import pallas as pl
from jax.experimental.pallas import tpu as pltpu

F32 = jnp.float32
BF16 = jnp.bfloat16

D_MODEL = 1024
D_FF = 2816
HEAD_DIM = 128
N_HEADS = 4
GROUP_W = N_HEADS * HEAD_DIM
CHUNK = 128
CONV_W = 4
DEC_SEQ = 8
SEQ_PER_TILE = CHUNK // DEC_SEQ
N_MAIN = 6 * GROUP_W
GATE_W = 128
FG_LANE = 64
FF_CHUNK = 256
N_FF_CHUNKS = D_FF // FF_CHUNK
ALPHA = 2.0 ** 0.25
LN_EPS = 1e-5
TOKEN_TILE = 512
VMEM_LIMIT_BYTES = 60 * 1024 * 1024

_NT = (((1,), (1,)), ((), ()))
_TN = (((0,), (0,)), ((), ()))


def _layer_norm(x, g, b):
    mu = jnp.mean(x, axis=-1, keepdims=True)
    xc = x - mu
    var = jnp.mean(xc * xc, axis=-1, keepdims=True)
    return xc * lax.rsqrt(var + LN_EPS) * g + b


def _head_rms(x, g):
    return x * lax.rsqrt(jnp.mean(x * x, axis=-1, keepdims=True) + LN_EPS) * g


def _silu(x):
    return x * jax.nn.sigmoid(x)


def _log_sigmoid(x):
    return jnp.minimum(x, 0.0) - jnp.log1p(jnp.exp(-jnp.abs(x)))


def _swiglu(xb_ref, wg_ref, wu_ref, wd_ref):
    acc = None
    for c in range(N_FF_CHUNKS):
        cols = slice(c * FF_CHUNK, (c + 1) * FF_CHUNK)
        xb = xb_ref[...]
        g = jnp.dot(xb, wg_ref[:, cols], preferred_element_type=F32)
        u = jnp.dot(xb, wu_ref[:, cols], preferred_element_type=F32)
        h = (_silu(g) * u).astype(BF16)
        d = jnp.dot(h, wd_ref[cols, :], preferred_element_type=F32)
        acc = d if acc is None else acc + d
    return acc


def _stage_a_kernel(x_ref, wg_ref, wu_ref, wd_ref, lng_ref, lnb_ref, wmain_ref, bmain_ref, wgate_ref, bgate_ref,
                    vlng_ref, vlnb_ref, x1_ref, zz_ref, gate_ref, xb_ref):
    x = x_ref[...]
    xb_ref[...] = x.astype(BF16)
    ffn = _swiglu(xb_ref, wg_ref, wu_ref, wd_ref)
    x1 = _layer_norm(ALPHA * x + 0.5 * ffn, lng_ref[...], lnb_ref[...])
    x1_ref[...] = x1
    xb_ref[...] = x1.astype(BF16)
    for grp in range(N_MAIN // GROUP_W):
        cols = slice(grp * GROUP_W, (grp + 1) * GROUP_W)
        z = jnp.dot(xb_ref[...], wmain_ref[:, cols], preferred_element_type=F32) + bmain_ref[:, cols]
        if grp == 0:
            zz_ref[:, cols] = jax.nn.gelu(z)
        elif grp == 1:
            z = jax.nn.gelu(z)
            for h in range(N_HEADS):
                hc = slice(h * HEAD_DIM, (h + 1) * HEAD_DIM)
                zz_ref[:, GROUP_W + h * HEAD_DIM:GROUP_W + (h + 1) * HEAD_DIM] = _layer_norm(
                    z[:, hc], vlng_ref[:, hc], vlnb_ref[:, hc])
        else:
            zz_ref[:, cols] = z
    gate_ref[...] = jnp.dot(xb_ref[...], wgate_ref[...], preferred_element_type=F32) + bgate_ref[...]


def _const_spec(shape):
    nd = len(shape)
    return pl.BlockSpec(shape, lambda i, _nd=nd: (0,) * _nd, pipeline_mode=pl.Buffered(1))


def _stage_a(x, wg, wu, wd, lng, lnb, wmain, bmain, wgate, bgate, vlng, vlnb):
    n = x.shape[0]
    tm = TOKEN_TILE
    row = lambda w: pl.BlockSpec((tm, w), lambda i: (i, 0))
    return pl.pallas_call(
        _stage_a_kernel,
        grid=(n // tm,),
        in_specs=[row(D_MODEL),
                  _const_spec(wg.shape), _const_spec(wu.shape), _const_spec(wd.shape),
                  _const_spec(lng.shape), _const_spec(lnb.shape),
                  _const_spec(wmain.shape), _const_spec(bmain.shape),
                  _const_spec(wgate.shape), _const_spec(bgate.shape),
                  _const_spec(vlng.shape), _const_spec(vlnb.shape)],
        out_specs=[row(D_MODEL), row(N_MAIN), row(GATE_W)],
        out_shape=[jax.ShapeDtypeStruct((n, D_MODEL), F32),
                   jax.ShapeDtypeStruct((n, N_MAIN), F32),
                   jax.ShapeDtypeStruct((n, GATE_W), F32)],
        scratch_shapes=[pltpu.VMEM((tm, D_MODEL), BF16)],
        compiler_params=pltpu.CompilerParams(dimension_semantics=("arbitrary",),
                                             vmem_limit_bytes=VMEM_LIMIT_BYTES),
        name="stage_a",
    )(x, wg, wu, wd, lng, lnb, wmain, bmain, wgate, bgate, vlng, vlnb)


def _stage_c_kernel(mix_ref, x1_ref, wout_ref, ln2g_ref, ln2b_ref, wg_ref, wu_ref, wd_ref, ln3g_ref, ln3b_ref,
                    y_ref, xb_ref):
    proj = jnp.dot(mix_ref[...].astype(BF16), wout_ref[...], preferred_element_type=F32)
    y = _layer_norm(ALPHA * x1_ref[...] + proj, ln2g_ref[...], ln2b_ref[...])
    y_ref[...] = y
    xb_ref[...] = y.astype(BF16)
    ffn = _swiglu(xb_ref, wg_ref, wu_ref, wd_ref)
    y_ref[...] = _layer_norm(ALPHA * y_ref[...] + 0.5 * ffn, ln3g_ref[...], ln3b_ref[...])


def _stage_c(mix, x1, wout, ln2g, ln2b, wg, wu, wd, ln3g, ln3b):
    n = x1.shape[0]
    tm = TOKEN_TILE
    row = lambda w: pl.BlockSpec((tm, w), lambda i: (i, 0))
    return pl.pallas_call(
        _stage_c_kernel,
        grid=(n // tm,),
        in_specs=[row(D_MODEL), row(D_MODEL),
                  _const_spec(wout.shape), _const_spec(ln2g.shape), _const_spec(ln2b.shape),
                  _const_spec(wg.shape), _const_spec(wu.shape), _const_spec(wd.shape),
                  _const_spec(ln3g.shape), _const_spec(ln3b.shape)],
        out_specs=row(D_MODEL),
        out_shape=jax.ShapeDtypeStruct((n, D_MODEL), F32),
        scratch_shapes=[pltpu.VMEM((tm, D_MODEL), BF16)],
        compiler_params=pltpu.CompilerParams(dimension_semantics=("arbitrary",),
                                             vmem_limit_bytes=VMEM_LIMIT_BYTES),
        name="stage_c",
    )(mix, x1, wout, ln2g, ln2b, wg, wu, wd, ln3g, ln3b)


def _head_cols(h, base=0):
    return slice(base + h * HEAD_DIM, base + (h + 1) * HEAD_DIM)


def _gmlp_heads(zz_ref, ws_ref, bs_ref, gmg_ref, mix_ref, mask):
    for h in range(N_HEADS):
        u = zz_ref[:, _head_cols(h)]
        vn = zz_ref[:, _head_cols(h, GROUP_W)]
        w = jnp.where(mask, ws_ref[h], 0.0).astype(BF16)
        mixed = jnp.dot(w, vn.astype(BF16), preferred_element_type=F32) + bs_ref[:, h:h + 1]
        mix_ref[:, _head_cols(h)] = _head_rms(u * mixed, gmg_ref[:, _head_cols(h)])


def _intra_chunk(q, k, v, bcol, brow, irow, mcol, mask):
    dlog = jnp.where(mask, bcol - brow + irow, -jnp.inf)
    inter = bcol + mcol
    m_t = jnp.maximum(inter, jnp.max(dlog, axis=-1, keepdims=True))
    w_intra = jnp.exp(dlog - m_t)
    w_inter = jnp.exp(inter - m_t)
    s = lax.dot_general(q.astype(BF16), k.astype(BF16), _NT, preferred_element_type=F32) * w_intra
    sv = jnp.dot(s.astype(BF16), v.astype(BF16), preferred_element_type=F32)
    ssum = jnp.sum(s, axis=-1, keepdims=True)
    return m_t, w_inter, sv, ssum


def _conv_taps(x, prev_fn, cw_ref, cb_ref, row_in_seq):
    acc = x * cw_ref[CONV_W - 1:CONV_W, :] + cb_ref[...]
    for j in range(1, CONV_W):
        shifted = jnp.where(row_in_seq < j, prev_fn(j), pltpu.roll(x, j, 0))
        acc = acc + shifted * cw_ref[CONV_W - 1 - j:CONV_W - j, :]
    return acc


def _mix_prompt_kernel(zz_ref, gate_ref, ws_ref, bs_ref, cw_ref, cb_ref, gmg_ref, mlg_ref,
                       mix_ref, c_ref, n_ref, m_ref, tail_ref):
    @pl.when(pl.program_id(1) == 0)
    def _():
        c_ref[...] = jnp.zeros_like(c_ref)
        n_ref[...] = jnp.zeros_like(n_ref)
        m_ref[...] = jnp.zeros_like(m_ref)
        tail_ref[...] = jnp.zeros_like(tail_ref)

    row = lax.broadcasted_iota(jnp.int32, (CHUNK, CHUNK), 0)
    col = lax.broadcasted_iota(jnp.int32, (CHUNK, CHUNK), 1)
    causal = col <= row

    _gmlp_heads(zz_ref, ws_ref, bs_ref, gmg_ref, mix_ref, causal)

    x = zz_ref[:, 2 * GROUP_W:4 * GROUP_W]
    row_w = lax.broadcasted_iota(jnp.int32, (CHUNK, 2 * GROUP_W), 0)
    tail = tail_ref[...]

    def prev_rows(j):
        first = pltpu.roll(tail, j, 0)
        return jnp.concatenate([first, x[8:]], axis=0)

    qk = _silu(_conv_taps(x, prev_rows, cw_ref, cb_ref, row_w))
    tail_ref[...] = x[CHUNK - 8:]

    gates = gate_ref[...]
    bcum = jnp.dot(causal.astype(F32), _log_sigmoid(gates), precision=lax.Precision.HIGHEST,
                   preferred_element_type=F32)
    gates_t = gates.T
    bcum_t = bcum.T
    m_row = m_ref[0]
    lane = lax.broadcasted_iota(jnp.int32, m_row.shape, 1)
    scale = HEAD_DIM ** -0.5

    for h in range(N_HEADS):
        q = qk[:, _head_cols(h)]
        k = qk[:, _head_cols(h, GROUP_W)] * scale
        v = zz_ref[:, _head_cols(h, 4 * GROUP_W)]
        o = zz_ref[:, _head_cols(h, 5 * GROUP_W)]
        icol = gates[:, h:h + 1]
        bcol = bcum[:, FG_LANE + h:FG_LANE + h + 1]
        irow = gates_t[h:h + 1, :]
        brow = bcum_t[FG_LANE + h:FG_LANE + h + 1, :]
        m_prev = m_row[:, h:h + 1]
        c_prev = c_ref[0, h]
        n_prev = n_ref[0, h:h + 1, :]

        m_t, w_inter, sv, ssum = _intra_chunk(q, k, v, bcol, brow, irow, m_prev, causal)
        qc = jnp.dot(q.astype(BF16), c_prev.astype(BF16), preferred_element_type=F32)
        num = sv + w_inter * qc
        den = ssum + w_inter * jnp.sum(q * n_prev, axis=-1, keepdims=True)
        hid = num / jnp.maximum(jnp.abs(den), jnp.exp(-m_t))
        mix_ref[:, _head_cols(h, GROUP_W)] = _head_rms(jax.nn.sigmoid(o) * hid, mlg_ref[:, _head_cols(h)])

        m_new = m_t[CHUNK - 1:CHUNK, :]
        b_last = bcol[CHUNK - 1:CHUNK, :]
        dec = jnp.exp(b_last + m_prev - m_new)
        kw = k * jnp.exp(b_last - bcol + icol - m_new)
        c_ref[0, h] = dec * c_prev + lax.dot_general(kw.astype(BF16), v.astype(BF16), _TN,
                                                     preferred_element_type=F32)
        n_ref[0, h:h + 1, :] = dec * n_prev + jnp.sum(kw, axis=0, keepdims=True)
        m_row = jnp.where(lane == h, m_new, m_row)

    m_ref[0] = m_row


def _mix_prompt(zz, gates, ws, bs_cols, cw, cb, gmg, mlg, batch, seq):
    n_chunks = seq // CHUNK
    row = lambda w: pl.BlockSpec((CHUNK, w), lambda b, c: (b * n_chunks + c, 0))
    const = lambda shape: pl.BlockSpec(shape, lambda b, c, _nd=len(shape): (0,) * _nd)
    return pl.pallas_call(
        _mix_prompt_kernel,
        grid=(batch, n_chunks),
        in_specs=[row(N_MAIN), row(GATE_W), const(ws.shape), const(bs_cols.shape), const(cw.shape),
                  const(cb.shape), const(gmg.shape), const(mlg.shape)],
        out_specs=[row(D_MODEL),
                   pl.BlockSpec((1, N_HEADS, HEAD_DIM, HEAD_DIM), lambda b, c: (b, 0, 0, 0)),
                   pl.BlockSpec((1, N_HEADS, HEAD_DIM), lambda b, c: (b, 0, 0)),
                   pl.BlockSpec((1, 1, GATE_W), lambda b, c: (b, 0, 0))],
        out_shape=[jax.ShapeDtypeStruct((batch * seq, D_MODEL), F32),
                   jax.ShapeDtypeStruct((batch, N_HEADS, HEAD_DIM, HEAD_DIM), F32),
                   jax.ShapeDtypeStruct((batch, N_HEADS, HEAD_DIM), F32),
                   jax.ShapeDtypeStruct((batch, 1, GATE_W), F32)],
        scratch_shapes=[pltpu.VMEM((8, 2 * GROUP_W), F32)],
        compiler_params=pltpu.CompilerParams(dimension_semantics=("arbitrary", "arbitrary")),
        name="mix_prompt",
    )(zz, gates, ws, bs_cols, cw, cb, gmg, mlg)


def _mix_sample_kernel(zz_ref, gate_ref, ws_ref, bs_ref, cw_ref, cb_ref, gmg_ref, mlg_ref, prev_ref,
                       c0_ref, n0_ref, m0_ref, mix_ref, c_ref, n_ref, mt_ref):
    row = lax.broadcasted_iota(jnp.int32, (CHUNK, CHUNK), 0)
    col = lax.broadcasted_iota(jnp.int32, (CHUNK, CHUNK), 1)
    seq_shift = DEC_SEQ.bit_length() - 1
    mask = (col <= row) & ((col >> seq_shift) == (row >> seq_shift))
    last_sel = (col == (row | (DEC_SEQ - 1))).astype(F32)

    _gmlp_heads(zz_ref, ws_ref, bs_ref, gmg_ref, mix_ref, mask)

    x = zz_ref[:, 2 * GROUP_W:4 * GROUP_W]
    row_w = lax.broadcasted_iota(jnp.int32, (CHUNK, 2 * GROUP_W), 0) & (DEC_SEQ - 1)
    prev = prev_ref[...]
    qk = _silu(_conv_taps(x, lambda j: pltpu.roll(prev, CHUNK - DEC_SEQ + j, 0), cw_ref, cb_ref, row_w))

    gates = gate_ref[...]
    bcum = jnp.dot(mask.astype(F32), _log_sigmoid(gates), precision=lax.Precision.HIGHEST,
                   preferred_element_type=F32)
    gates_t = gates.T
    bcum_t = bcum.T
    lane = lax.broadcasted_iota(jnp.int32, (CHUNK, GATE_W), 1)
    mt_all = jnp.zeros((CHUNK, GATE_W), F32)
    scale = HEAD_DIM ** -0.5

    for h in range(N_HEADS):
        q = qk[:, _head_cols(h)]
        k = qk[:, _head_cols(h, GROUP_W)] * scale
        v = zz_ref[:, _head_cols(h, 4 * GROUP_W)]
        o = zz_ref[:, _head_cols(h, 5 * GROUP_W)]
        icol = gates[:, h:h + 1]
        bcol = bcum[:, FG_LANE + h:FG_LANE + h + 1]
        irow = gates_t[h:h + 1, :]
        brow = bcum_t[FG_LANE + h:FG_LANE + h + 1, :]
        m_prev = m0_ref[:, h:h + 1]

        m_t, w_inter, sv, ssum = _intra_chunk(q, k, v, bcol, brow, irow, m_prev, mask)
        qb = q.astype(BF16)
        qc_rows, qn_rows = [], []
        for s in range(SEQ_PER_TILE):
            rows = slice(s * DEC_SEQ, (s + 1) * DEC_SEQ)
            qc_rows.append(jnp.dot(qb[rows], c0_ref[s, h].astype(BF16), preferred_element_type=F32))
            qn_rows.append(jnp.sum(q[rows] * n0_ref[s, h:h + 1, :], axis=-1, keepdims=True))
        qc = jnp.concatenate(qc_rows, axis=0)
        qn = jnp.concatenate(qn_rows, axis=0)
        num = sv + w_inter * qc
        den = ssum + w_inter * qn
        hid = num / jnp.maximum(jnp.abs(den), jnp.exp(-m_t))
        mix_ref[:, _head_cols(h, GROUP_W)] = _head_rms(jax.nn.sigmoid(o) * hid, mlg_ref[:, _head_cols(h)])

        packed = jnp.where(lane == 0, m_t, jnp.where(lane == 1, bcol, 0.0))
        lastv = jnp.dot(last_sel, packed, precision=lax.Precision.HIGHEST, preferred_element_type=F32)
        m_new = lastv[:, 0:1]
        b_last = lastv[:, 1:2]
        dec = jnp.exp(b_last + m_prev - m_new)
        kw = k * jnp.exp(b_last - bcol + icol - m_new)
        kwb = kw.astype(BF16)
        vb = v.astype(BF16)
        for s in range(SEQ_PER_TILE):
            rows = slice(s * DEC_SEQ, (s + 1) * DEC_SEQ)
            dec_s = dec[s * DEC_SEQ:s * DEC_SEQ + 1, :]
            c_ref[s, h] = dec_s * c0_ref[s, h] + lax.dot_general(kwb[rows], vb[rows], _TN,
                                                                preferred_element_type=F32)
            n_ref[s, h:h + 1, :] = dec_s * n0_ref[s, h:h + 1, :] + jnp.sum(kw[rows], axis=0, keepdims=True)
        mt_all = jnp.where(lane == h, m_t, mt_all)

    mt_ref[...] = mt_all


def _mix_sample(zz, gates, ws_t, bs_cols, cw, cb, gmg, mlg, prev, c0, n0, m0_tok):
    n = zz.shape[0]
    n_tiles = n // CHUNK
    row = lambda w: pl.BlockSpec((CHUNK, w), lambda i: (i, 0))
    const = lambda shape: pl.BlockSpec(shape, lambda i, _nd=len(shape): (0,) * _nd)
    c_spec = pl.BlockSpec((SEQ_PER_TILE, N_HEADS, HEAD_DIM, HEAD_DIM), lambda i: (i, 0, 0, 0))
    n_spec = pl.BlockSpec((SEQ_PER_TILE, N_HEADS, HEAD_DIM), lambda i: (i, 0, 0))
    return pl.pallas_call(
        _mix_sample_kernel,
        grid=(n_tiles,),
        in_specs=[row(N_MAIN), row(GATE_W), const(ws_t.shape), const(bs_cols.shape), const(cw.shape),
                  const(cb.shape), const(gmg.shape), const(mlg.shape), row(2 * GROUP_W),
                  c_spec, n_spec, row(N_HEADS)],
        out_specs=[row(D_MODEL), c_spec, n_spec, row(GATE_W)],
        out_shape=[jax.ShapeDtypeStruct((n, D_MODEL), F32),
                   jax.ShapeDtypeStruct(c0.shape, F32),
                   jax.ShapeDtypeStruct(n0.shape, F32),
                   jax.ShapeDtypeStruct((n, GATE_W), F32)],
        compiler_params=pltpu.CompilerParams(dimension_semantics=("arbitrary",),
                                             vmem_limit_bytes=VMEM_LIMIT_BYTES),
        name="mix_sample",
    )(zz, gates, ws_t, bs_cols, cw, cb, gmg, mlg, prev, c0, n0, m0_tok)


def _ffn_weights(wg, wu, wd):
    return wg.astype(BF16), wu.astype(BF16), wd.astype(BF16)


def _gate_columns(w):
    out = jnp.zeros(w.shape[:-1] + (GATE_W,), w.dtype)
    out = out.at[..., 0:N_HEADS].set(w[..., N_MAIN:N_MAIN + N_HEADS])
    return out.at[..., FG_LANE:FG_LANE + N_HEADS].set(w[..., N_MAIN + N_HEADS:N_MAIN + 2 * N_HEADS])


def kernel(x_prompt, x_sample, state_conv, state_C, state_n, state_m, ffn1_wg, ffn1_wu, ffn1_wd, ln1_g, ln1_b, w_in, b_in, gm_ln_g, gm_ln_b, gm_ws, gm_bs, conv_w, conv_b, gm_out_g, ml_out_g, w_out, ln2_g, ln2_b, ffn2_wg, ffn2_wu, ffn2_wd, ln3_g, ln3_b):
    depth = ffn1_wg.shape[0]
    bp, seq, _ = x_prompt.shape
    bs, dec_seq, _ = x_sample.shape
    assert dec_seq == DEC_SEQ and seq % CHUNK == 0 and (bs * dec_seq) % CHUNK == 0
    y_p = x_prompt.reshape(bp * seq, D_MODEL)
    y_s = x_sample.reshape(bs * dec_seq, D_MODEL)
    outs = []
    for l in range(depth):
        f1 = _ffn_weights(ffn1_wg[l], ffn1_wu[l], ffn1_wd[l])
        f2 = _ffn_weights(ffn2_wg[l], ffn2_wu[l], ffn2_wd[l])
        row = lambda a: a.reshape(1, -1)
        a_params = (*f1, row(ln1_g[l]), row(ln1_b[l]),
                    w_in[l][:, :N_MAIN].astype(BF16), row(b_in[l][:N_MAIN]),
                    _gate_columns(w_in[l]).astype(BF16), row(_gate_columns(b_in[l])),
                    row(gm_ln_g[l]), row(gm_ln_b[l]))
        c_params = (w_out[l].astype(BF16), row(ln2_g[l]), row(ln2_b[l]), *f2, row(ln3_g[l]), row(ln3_b[l]))
        mix_params = (conv_w[l], row(conv_b[l]), row(gm_out_g[l]), row(ml_out_g[l]))

        x1_p, zz_p, g_p = _stage_a(y_p, *a_params)
        x1_s, zz_s, g_s = _stage_a(y_s, *a_params)

        mix_p, c_p, n_p, m_p = _mix_prompt(zz_p, g_p, gm_ws[l], jnp.transpose(gm_bs[l]), *mix_params, bp, seq)

        ws_t = jnp.tile(gm_ws[l][:, :DEC_SEQ, :DEC_SEQ], (1, SEQ_PER_TILE, SEQ_PER_TILE))
        bs_t = jnp.tile(jnp.transpose(gm_bs[l][:, :DEC_SEQ]), (SEQ_PER_TILE, 1))
        prev = jnp.pad(state_conv[l], ((0, 0), (DEC_SEQ - (CONV_W - 1), 0), (0, 0))).reshape(bs * DEC_SEQ, 2 * GROUP_W)
        m0_tok = jnp.repeat(state_m[l], DEC_SEQ, axis=0)
        mix_s, c_s, n_s, mt_s = _mix_sample(zz_s, g_s, ws_t, bs_t, *mix_params, prev, state_C[l], state_n[l], m0_tok)

        y_p = _stage_c(mix_p, x1_p, *c_params)
        y_s = _stage_c(mix_s, x1_s, *c_params)

        zz_p3 = zz_p.reshape(bp, seq, N_MAIN)
        zz_s3 = zz_s.reshape(bs, dec_seq, N_MAIN)
        outs.append((
            zz_p3[:, seq - CHUNK:, GROUP_W:2 * GROUP_W].reshape(bp, CHUNK, N_HEADS, HEAD_DIM),
            zz_s3[:, :, GROUP_W:2 * GROUP_W].reshape(bs, dec_seq, N_HEADS, HEAD_DIM),
            zz_p3[:, seq - (CONV_W - 1):, 2 * GROUP_W:4 * GROUP_W],
            zz_s3[:, dec_seq - (CONV_W - 1):, 2 * GROUP_W:4 * GROUP_W],
            c_p, c_s, n_p, n_s,
            m_p[:, 0, :N_HEADS],
            mt_s.reshape(bs, dec_seq, GATE_W)[:, dec_seq - 1, :N_HEADS],
        ))
    stacked = [jnp.stack(a) for a in zip(*outs)]
    return (y_p.reshape(bp, seq, D_MODEL), y_s.reshape(bs, dec_seq, D_MODEL), *stacked)
```

```python
import functools

import jax
import jax.numpy as jnp
from jax import lax
from jax.experimental import pallas as pl
from jax.experimental.pallas import tpu as pltpu

F32 = jnp.float32
BF16 = jnp.bfloat16

D_MODEL = 1024
D_FF = 2816
HEAD_DIM = 128
N_HEADS = 4
GROUP_W = N_HEADS * HEAD_DIM
CHUNK = 128
CONV_W = 4
DEC_SEQ = 8
SEQ_PER_TILE = CHUNK // DEC_SEQ
N_MAIN = 6 * GROUP_W
GATE_W = 128
FG_LANE = 64
FF_CHUNK = 256
N_FF_CHUNKS = D_FF // FF_CHUNK
ALPHA = 2.0 ** 0.25
LN_EPS = 1e-5
TOKEN_TILE = 512
PAD_ROWS = 8
MIX_UNITS_PER_TILE = (TOKEN_TILE // CHUNK) * (2 * N_HEADS + 2) + 1
VMEM_LIMIT_BYTES = 60 * 1024 * 1024

_NT = (((1,), (1,)), ((), ()))
_TN = (((0,), (0,)), ((), ()))


def _layer_norm(x, g, b):
    mu = jnp.mean(x, axis=-1, keepdims=True)
    xc = x - mu
    var = jnp.mean(xc * xc, axis=-1, keepdims=True)
    return xc * lax.rsqrt(var + LN_EPS) * g + b


def _head_rms(x, g):
    return x * lax.rsqrt(jnp.mean(x * x, axis=-1, keepdims=True) + LN_EPS) * g


def _silu(x):
    return x * jax.nn.sigmoid(x)


def _log_sigmoid(x):
    return jnp.minimum(x, 0.0) - jnp.log1p(jnp.exp(-jnp.abs(x)))


def _swiglu(xb_ref, wg_ref, wu_ref, wd_ref, between_chunks=None):
    acc = None
    for c in range(N_FF_CHUNKS):
        cols = slice(c * FF_CHUNK, (c + 1) * FF_CHUNK)
        xb = xb_ref[...]
        g = jnp.dot(xb, wg_ref[:, cols], preferred_element_type=F32)
        u = jnp.dot(xb, wu_ref[:, cols], preferred_element_type=F32)
        h = (_silu(g) * u).astype(BF16)
        d = jnp.dot(h, wd_ref[cols, :], preferred_element_type=F32)
        acc = d if acc is None else acc + d
        if between_chunks is not None:
            between_chunks(c)
    return acc


def _ffn1_and_project(x_ref, wg_ref, wu_ref, wd_ref, lng_ref, lnb_ref, wmain_ref, bmain_ref, wgate_ref, bgate_ref,
                      vlng_ref, vlnb_ref, x1_ref, xb_ref, store_z, store_gate, between_chunks=None):
    x = x_ref[...]
    xb_ref[...] = x.astype(BF16)
    ffn = _swiglu(xb_ref, wg_ref, wu_ref, wd_ref, between_chunks)
    x1 = _layer_norm(ALPHA * x + 0.5 * ffn, lng_ref[...], lnb_ref[...])
    x1_ref[...] = x1
    xb_ref[...] = x1.astype(BF16)
    for grp in range(N_MAIN // GROUP_W):
        cols = slice(grp * GROUP_W, (grp + 1) * GROUP_W)
        z = jnp.dot(xb_ref[...], wmain_ref[:, cols], preferred_element_type=F32) + bmain_ref[:, cols]
        if grp == 0:
            store_z(cols, jax.nn.gelu(z))
        elif grp == 1:
            z = jax.nn.gelu(z)
            for h in range(N_HEADS):
                hc = slice(h * HEAD_DIM, (h + 1) * HEAD_DIM)
                store_z(slice(GROUP_W + h * HEAD_DIM, GROUP_W + (h + 1) * HEAD_DIM),
                        _layer_norm(z[:, hc], vlng_ref[:, hc], vlnb_ref[:, hc]))
        else:
            store_z(cols, z)
    store_gate(jnp.dot(xb_ref[...], wgate_ref[...], preferred_element_type=F32) + bgate_ref[...])


def _stage_a_kernel(*refs):
    *in_refs, x1_ref, zz_ref, gate_ref, xb_ref = refs

    def store_z(cols, val):
        zz_ref[:, cols] = val

    def store_gate(val):
        gate_ref[...] = val

    _ffn1_and_project(*in_refs, x1_ref, xb_ref, store_z, store_gate)


def _const_spec(shape):
    nd = len(shape)
    return pl.BlockSpec(shape, lambda i, _nd=nd: (0,) * _nd, pipeline_mode=pl.Buffered(1))


def _stage_a(x, wg, wu, wd, lng, lnb, wmain, bmain, wgate, bgate, vlng, vlnb):
    n = x.shape[0]
    tm = TOKEN_TILE
    row = lambda w: pl.BlockSpec((tm, w), lambda i: (i, 0))
    return pl.pallas_call(
        _stage_a_kernel,
        grid=(n // tm,),
        in_specs=[row(D_MODEL),
                  _const_spec(wg.shape), _const_spec(wu.shape), _const_spec(wd.shape),
                  _const_spec(lng.shape), _const_spec(lnb.shape),
                  _const_spec(wmain.shape), _const_spec(bmain.shape),
                  _const_spec(wgate.shape), _const_spec(bgate.shape),
                  _const_spec(vlng.shape), _const_spec(vlnb.shape)],
        out_specs=[row(D_MODEL), row(N_MAIN), row(GATE_W)],
        out_shape=[jax.ShapeDtypeStruct((n, D_MODEL), F32),
                   jax.ShapeDtypeStruct((n, N_MAIN), F32),
                   jax.ShapeDtypeStruct((n, GATE_W), F32)],
        scratch_shapes=[pltpu.VMEM((tm, D_MODEL), BF16)],
        compiler_params=pltpu.CompilerParams(dimension_semantics=("arbitrary",),
                                             vmem_limit_bytes=VMEM_LIMIT_BYTES),
        name="stage_a",
    )(x, wg, wu, wd, lng, lnb, wmain, bmain, wgate, bgate, vlng, vlnb)


def _stage_c_kernel(mix_ref, x1_ref, wout_ref, ln2g_ref, ln2b_ref, wg_ref, wu_ref, wd_ref, ln3g_ref, ln3b_ref,
                    y_ref, xb_ref):
    proj = jnp.dot(mix_ref[...].astype(BF16), wout_ref[...], preferred_element_type=F32)
    y = _layer_norm(ALPHA * x1_ref[...] + proj, ln2g_ref[...], ln2b_ref[...])
    y_ref[...] = y
    xb_ref[...] = y.astype(BF16)
    ffn = _swiglu(xb_ref, wg_ref, wu_ref, wd_ref)
    y_ref[...] = _layer_norm(ALPHA * y_ref[...] + 0.5 * ffn, ln3g_ref[...], ln3b_ref[...])


def _stage_c(mix, x1, wout, ln2g, ln2b, wg, wu, wd, ln3g, ln3b):
    n = x1.shape[0]
    tm = TOKEN_TILE
    row = lambda w: pl.BlockSpec((tm, w), lambda i: (i, 0))
    return pl.pallas_call(
        _stage_c_kernel,
        grid=(n // tm,),
        in_specs=[row(D_MODEL), row(D_MODEL),
                  _const_spec(wout.shape), _const_spec(ln2g.shape), _const_spec(ln2b.shape),
                  _const_spec(wg.shape), _const_spec(wu.shape), _const_spec(wd.shape),
                  _const_spec(ln3g.shape), _const_spec(ln3b.shape)],
        out_specs=row(D_MODEL),
        out_shape=jax.ShapeDtypeStruct((n, D_MODEL), F32),
        scratch_shapes=[pltpu.VMEM((tm, D_MODEL), BF16)],
        compiler_params=pltpu.CompilerParams(dimension_semantics=("arbitrary",),
                                             vmem_limit_bytes=VMEM_LIMIT_BYTES),
        name="stage_c",
    )(mix, x1, wout, ln2g, ln2b, wg, wu, wd, ln3g, ln3b)


def _head_cols(h, base=0):
    return slice(base + h * HEAD_DIM, base + (h + 1) * HEAD_DIM)


def _gmlp_heads(zz_ref, ws_ref, bs_ref, gmg_ref, mix_ref, mask):
    for h in range(N_HEADS):
        u = zz_ref[:, _head_cols(h)]
        vn = zz_ref[:, _head_cols(h, GROUP_W)]
        w = jnp.where(mask, ws_ref[h], 0.0).astype(BF16)
        mixed = jnp.dot(w, vn.astype(BF16), preferred_element_type=F32) + bs_ref[:, h:h + 1]
        mix_ref[:, _head_cols(h)] = _head_rms(u * mixed, gmg_ref[:, _head_cols(h)])


def _intra_chunk(q, k, v, bcol, brow, irow, mcol, mask):
    dlog = jnp.where(mask, bcol - brow + irow, -jnp.inf)
    inter = bcol + mcol
    m_t = jnp.maximum(inter, jnp.max(dlog, axis=-1, keepdims=True))
    w_intra = jnp.exp(dlog - m_t)
    w_inter = jnp.exp(inter - m_t)
    s = lax.dot_general(q.astype(BF16), k.astype(BF16), _NT, preferred_element_type=F32) * w_intra
    sv = jnp.dot(s.astype(BF16), v.astype(BF16), preferred_element_type=F32)
    ssum = jnp.sum(s, axis=-1, keepdims=True)
    return m_t, w_inter, sv, ssum


def _conv_taps(x, prev_fn, cw_ref, cb_ref, row_in_seq):
    acc = x * cw_ref[CONV_W - 1:CONV_W, :] + cb_ref[...]
    for j in range(1, CONV_W):
        shifted = jnp.where(row_in_seq < j, prev_fn(j), pltpu.roll(x, j, 0))
        acc = acc + shifted * cw_ref[CONV_W - 1 - j:CONV_W - j, :]
    return acc


def _mix_prompt_tile(zt, gt, ws_ref, bs_ref, cw_ref, cb_ref, gmg_ref, mlg_ref,
                     mix_ref, c_ref, n_ref, m_ref, gmv_ref, ctail_ref, tail_ref):
    row = lax.broadcasted_iota(jnp.int32, (CHUNK, CHUNK), 0)
    col = lax.broadcasted_iota(jnp.int32, (CHUNK, CHUNK), 1)
    causal = col <= row
    causal_f = causal.astype(F32)
    qk_cols = slice(2 * GROUP_W, 4 * GROUP_W)
    scale = HEAD_DIM ** -0.5

    zt[0:PAD_ROWS, qk_cols] = tail_ref[...]
    c_state = [c_ref[0, h] for h in range(N_HEADS)]
    n_state = [n_ref[0, h:h + 1, :] for h in range(N_HEADS)]
    m_row = m_ref[0]
    lane = lax.broadcasted_iota(jnp.int32, m_row.shape, 1)

    for c in range(TOKEN_TILE // CHUNK):
        r0 = PAD_ROWS + c * CHUNK
        rows = slice(r0, r0 + CHUNK)
        out_rows = slice(c * CHUNK, (c + 1) * CHUNK)

        for h in range(N_HEADS):
            w = jnp.where(causal, ws_ref[h], 0.0).astype(BF16)
            mixed = jnp.dot(w, zt[rows, _head_cols(h, GROUP_W)].astype(BF16),
                            preferred_element_type=F32) + bs_ref[:, h:h + 1]
            mix_ref[out_rows, _head_cols(h)] = _head_rms(zt[rows, _head_cols(h)] * mixed, gmg_ref[:, _head_cols(h)])
            yield

        acc = zt[rows, qk_cols] * cw_ref[CONV_W - 1:CONV_W, :] + cb_ref[...]
        for j in range(1, CONV_W):
            acc = acc + zt[r0 - j:r0 - j + CHUNK, qk_cols] * cw_ref[CONV_W - 1 - j:CONV_W - j, :]
        qk = _silu(acc)
        yield

        gates = gt[out_rows, :]
        bcum = jnp.dot(causal_f, _log_sigmoid(gates), precision=lax.Precision.HIGHEST, preferred_element_type=F32)
        gates_t = gates.T
        bcum_t = bcum.T
        yield

        for h in range(N_HEADS):
            q = qk[:, _head_cols(h)]
            k = qk[:, _head_cols(h, GROUP_W)] * scale
            v = zt[rows, _head_cols(h, 4 * GROUP_W)]
            o = zt[rows, _head_cols(h, 5 * GROUP_W)]
            icol = gates[:, h:h + 1]
            bcol = bcum[:, FG_LANE + h:FG_LANE + h + 1]
            irow = gates_t[h:h + 1, :]
            brow = bcum_t[FG_LANE + h:FG_LANE + h + 1, :]
            m_prev = m_row[:, h:h + 1]

            m_t, w_inter, sv, ssum = _intra_chunk(q, k, v, bcol, brow, irow, m_prev, causal)
            qc = jnp.dot(q.astype(BF16), c_state[h].astype(BF16), preferred_element_type=F32)
            num = sv + w_inter * qc
            den = ssum + w_inter * jnp.sum(q * n_state[h], axis=-1, keepdims=True)
            hid = num / jnp.maximum(jnp.abs(den), jnp.exp(-m_t))
            mix_ref[out_rows, _head_cols(h, GROUP_W)] = _head_rms(jax.nn.sigmoid(o) * hid,
                                                                 mlg_ref[:, _head_cols(h)])

            m_new = m_t[CHUNK - 1:CHUNK, :]
            b_last = bcol[CHUNK - 1:CHUNK, :]
            dec = jnp.exp(b_last + m_prev - m_new)
            kw = k * jnp.exp(b_last - bcol + icol - m_new)
            c_state[h] = dec * c_state[h] + lax.dot_general(kw.astype(BF16), v.astype(BF16), _TN,
                                                            preferred_element_type=F32)
            n_state[h] = dec * n_state[h] + jnp.sum(kw, axis=0, keepdims=True)
            m_row = jnp.where(lane == h, m_new, m_row)
            yield

    for h in range(N_HEADS):
        c_ref[0, h] = c_state[h]
        n_ref[0, h:h + 1, :] = n_state[h]
    m_ref[0] = m_row
    last8 = slice(PAD_ROWS + TOKEN_TILE - 8, PAD_ROWS + TOKEN_TILE)
    tail_ref[...] = zt[last8, qk_cols]
    ctail_ref[0] = zt[last8, qk_cols]
    gmv_ref[0] = zt[PAD_ROWS + TOKEN_TILE - CHUNK:PAD_ROWS + TOKEN_TILE, GROUP_W:2 * GROUP_W]


def _fused_prompt_kernel(tiles_per_seq, *refs):
    (x_ref, wg_ref, wu_ref, wd_ref, lng_ref, lnb_ref, wmain_ref, bmain_ref, wgate_ref, bgate_ref, vlng_ref, vlnb_ref,
     ws_ref, bs_ref, cw_ref, cb_ref, gmg_ref, mlg_ref,
     x1_ref, mix_ref, c_ref, n_ref, m_ref, gmv_ref, ctail_ref,
     xb_ref, zbuf_ref, gbuf_ref, tail_ref) = refs
    i = pl.program_id(0)
    slot_w = i % 2
    slot_r = 1 - slot_w

    @pl.when(i == 0)
    def _():
        zbuf_ref[1] = jnp.zeros(zbuf_ref.shape[1:], F32)
        gbuf_ref[1] = jnp.zeros(gbuf_ref.shape[1:], F32)

    @pl.when((i == 0) | (i % tiles_per_seq == 1))
    def _():
        c_ref[...] = jnp.zeros_like(c_ref)
        n_ref[...] = jnp.zeros_like(n_ref)
        m_ref[...] = jnp.zeros_like(m_ref)
        tail_ref[...] = jnp.zeros_like(tail_ref)

    def store_z(cols, val):
        zbuf_ref[slot_w, PAD_ROWS:PAD_ROWS + TOKEN_TILE, cols] = val

    def store_gate(val):
        gbuf_ref[slot_w] = val

    mixer = _mix_prompt_tile(zbuf_ref.at[slot_r], gbuf_ref.at[slot_r], ws_ref, bs_ref, cw_ref, cb_ref, gmg_ref,
                             mlg_ref, mix_ref, c_ref, n_ref, m_ref, gmv_ref, ctail_ref, tail_ref)

    def mixer_units(c):
        first = (c * MIX_UNITS_PER_TILE) // N_FF_CHUNKS
        last = ((c + 1) * MIX_UNITS_PER_TILE) // N_FF_CHUNKS
        for _ in range(last - first):
            next(mixer, None)

    _ffn1_and_project(x_ref, wg_ref, wu_ref, wd_ref, lng_ref, lnb_ref, wmain_ref, bmain_ref, wgate_ref, bgate_ref,
                      vlng_ref, vlnb_ref, x1_ref, xb_ref, store_z, store_gate, mixer_units)
    assert next(mixer, "done") == "done"


def _stage_a_mix_prompt(x, a_params, ws, bs_cols, cw, cb, gmg, mlg, batch, seq):
    tm = TOKEN_TILE
    n_tiles = (batch * seq) // tm
    tiles_per_seq = seq // tm
    mix_params = (ws, bs_cols, cw, cb, gmg, mlg)
    tile_in = lambda i: (jnp.minimum(i, n_tiles - 1), 0)
    tile_out = lambda i: (jnp.maximum(i - 1, 0), 0)
    seq_of = lambda i: jnp.maximum(i - 1, 0) // tiles_per_seq
    return pl.pallas_call(
        functools.partial(_fused_prompt_kernel, tiles_per_seq),
        grid=(n_tiles + 1,),
        in_specs=[pl.BlockSpec((tm, D_MODEL), tile_in)] + [_const_spec(p.shape) for p in (*a_params, *mix_params)],
        out_specs=[pl.BlockSpec((tm, D_MODEL), tile_in),
                   pl.BlockSpec((tm, D_MODEL), tile_out),
                   pl.BlockSpec((1, N_HEADS, HEAD_DIM, HEAD_DIM), lambda i: (seq_of(i), 0, 0, 0)),
                   pl.BlockSpec((1, N_HEADS, HEAD_DIM), lambda i: (seq_of(i), 0, 0)),
                   pl.BlockSpec((1, 1, GATE_W), lambda i: (seq_of(i), 0, 0)),
                   pl.BlockSpec((1, CHUNK, GROUP_W), lambda i: (seq_of(i), 0, 0)),
                   pl.BlockSpec((1, 8, 2 * GROUP_W), lambda i: (seq_of(i), 0, 0))],
        out_shape=[jax.ShapeDtypeStruct((batch * seq, D_MODEL), F32),
                   jax.ShapeDtypeStruct((batch * seq, D_MODEL), F32),
                   jax.ShapeDtypeStruct((batch, N_HEADS, HEAD_DIM, HEAD_DIM), F32),
                   jax.ShapeDtypeStruct((batch, N_HEADS, HEAD_DIM), F32),
                   jax.ShapeDtypeStruct((batch, 1, GATE_W), F32),
                   jax.ShapeDtypeStruct((batch, CHUNK, GROUP_W), F32),
                   jax.ShapeDtypeStruct((batch, 8, 2 * GROUP_W), F32)],
        scratch_shapes=[pltpu.VMEM((tm, D_MODEL), BF16),
                        pltpu.VMEM((2, PAD_ROWS + tm, N_MAIN), F32),
                        pltpu.VMEM((2, tm, GATE_W), F32),
                        pltpu.VMEM((8, 2 * GROUP_W), F32)],
        compiler_params=pltpu.CompilerParams(dimension_semantics=("arbitrary",),
                                             vmem_limit_bytes=VMEM_LIMIT_BYTES),
        name="stage_a_mix_prompt",
    )(x, *a_params, *mix_params)


def _mix_sample_kernel(zz_ref, gate_ref, ws_ref, bs_ref, cw_ref, cb_ref, gmg_ref, mlg_ref, prev_ref,
                       c0_ref, n0_ref, m0_ref, mix_ref, c_ref, n_ref, mt_ref):
    row = lax.broadcasted_iota(jnp.int32, (CHUNK, CHUNK), 0)
    col = lax.broadcasted_iota(jnp.int32, (CHUNK, CHUNK), 1)
    seq_shift = DEC_SEQ.bit_length() - 1
    mask = (col <= row) & ((col >> seq_shift) == (row >> seq_shift))
    last_sel = (col == (row | (DEC_SEQ - 1))).astype(F32)

    _gmlp_heads(zz_ref, ws_ref, bs_ref, gmg_ref, mix_ref, mask)

    x = zz_ref[:, 2 * GROUP_W:4 * GROUP_W]
    row_w = lax.broadcasted_iota(jnp.int32, (CHUNK, 2 * GROUP_W), 0) & (DEC_SEQ - 1)
    prev = prev_ref[...]
    qk = _silu(_conv_taps(x, lambda j: pltpu.roll(prev, CHUNK - DEC_SEQ + j, 0), cw_ref, cb_ref, row_w))

    gates = gate_ref[...]
    bcum = jnp.dot(mask.astype(F32), _log_sigmoid(gates), precision=lax.Precision.HIGHEST,
                   preferred_element_type=F32)
    gates_t = gates.T
    bcum_t = bcum.T
    lane = lax.broadcasted_iota(jnp.int32, (CHUNK, GATE_W), 1)
    mt_all = jnp.zeros((CHUNK, GATE_W), F32)
    scale = HEAD_DIM ** -0.5

    for h in range(N_HEADS):
        q = qk[:, _head_cols(h)]
        k = qk[:, _head_cols(h, GROUP_W)] * scale
        v = zz_ref[:, _head_cols(h, 4 * GROUP_W)]
        o = zz_ref[:, _head_cols(h, 5 * GROUP_W)]
        icol = gates[:, h:h + 1]
        bcol = bcum[:, FG_LANE + h:FG_LANE + h + 1]
        irow = gates_t[h:h + 1, :]
        brow = bcum_t[FG_LANE + h:FG_LANE + h + 1, :]
        m_prev = m0_ref[:, h:h + 1]

        m_t, w_inter, sv, ssum = _intra_chunk(q, k, v, bcol, brow, irow, m_prev, mask)
        qb = q.astype(BF16)
        qc_rows, qn_rows = [], []
        for s in range(SEQ_PER_TILE):
            rows = slice(s * DEC_SEQ, (s + 1) * DEC_SEQ)
            qc_rows.append(jnp.dot(qb[rows], c0_ref[s, h].astype(BF16), preferred_element_type=F32))
            qn_rows.append(jnp.sum(q[rows] * n0_ref[s, h:h + 1, :], axis=-1, keepdims=True))
        qc = jnp.concatenate(qc_rows, axis=0)
        qn = jnp.concatenate(qn_rows, axis=0)
        num = sv + w_inter * qc
        den = ssum + w_inter * qn
        hid = num / jnp.maximum(jnp.abs(den), jnp.exp(-m_t))
        mix_ref[:, _head_cols(h, GROUP_W)] = _head_rms(jax.nn.sigmoid(o) * hid, mlg_ref[:, _head_cols(h)])

        packed = jnp.where(lane == 0, m_t, jnp.where(lane == 1, bcol, 0.0))
        lastv = jnp.dot(last_sel, packed, precision=lax.Precision.HIGHEST, preferred_element_type=F32)
        m_new = lastv[:, 0:1]
        b_last = lastv[:, 1:2]
        dec = jnp.exp(b_last + m_prev - m_new)
        kw = k * jnp.exp(b_last - bcol + icol - m_new)
        kwb = kw.astype(BF16)
        vb = v.astype(BF16)
        for s in range(SEQ_PER_TILE):
            rows = slice(s * DEC_SEQ, (s + 1) * DEC_SEQ)
            dec_s = dec[s * DEC_SEQ:s * DEC_SEQ + 1, :]
            c_ref[s, h] = dec_s * c0_ref[s, h] + lax.dot_general(kwb[rows], vb[rows], _TN,
                                                                preferred_element_type=F32)
            n_ref[s, h:h + 1, :] = dec_s * n0_ref[s, h:h + 1, :] + jnp.sum(kw[rows], axis=0, keepdims=True)
        mt_all = jnp.where(lane == h, m_t, mt_all)

    mt_ref[...] = mt_all


def _mix_sample(zz, gates, ws_t, bs_cols, cw, cb, gmg, mlg, prev, c0, n0, m0_tok):
    n = zz.shape[0]
    n_tiles = n // CHUNK
    row = lambda w: pl.BlockSpec((CHUNK, w), lambda i: (i, 0))
    const = lambda shape: pl.BlockSpec(shape, lambda i, _nd=len(shape): (0,) * _nd)
    c_spec = pl.BlockSpec((SEQ_PER_TILE, N_HEADS, HEAD_DIM, HEAD_DIM), lambda i: (i, 0, 0, 0))
    n_spec = pl.BlockSpec((SEQ_PER_TILE, N_HEADS, HEAD_DIM), lambda i: (i, 0, 0))
    return pl.pallas_call(
        _mix_sample_kernel,
        grid=(n_tiles,),
        in_specs=[row(N_MAIN), row(GATE_W), const(ws_t.shape), const(bs_cols.shape), const(cw.shape),
                  const(cb.shape), const(gmg.shape), const(mlg.shape), row(2 * GROUP_W),
                  c_spec, n_spec, row(N_HEADS)],
        out_specs=[row(D_MODEL), c_spec, n_spec, row(GATE_W)],
        out_shape=[jax.ShapeDtypeStruct((n, D_MODEL), F32),
                   jax.ShapeDtypeStruct(c0.shape, F32),
                   jax.ShapeDtypeStruct(n0.shape, F32),
                   jax.ShapeDtypeStruct((n, GATE_W), F32)],
        compiler_params=pltpu.CompilerParams(dimension_semantics=("arbitrary",),
                                             vmem_limit_bytes=VMEM_LIMIT_BYTES),
        name="mix_sample",
    )(zz, gates, ws_t, bs_cols, cw, cb, gmg, mlg, prev, c0, n0, m0_tok)


def _ffn_weights(wg, wu, wd):
    return wg.astype(BF16), wu.astype(BF16), wd.astype(BF16)


def _gate_columns(w):
    out = jnp.zeros(w.shape[:-1] + (GATE_W,), w.dtype)
    out = out.at[..., 0:N_HEADS].set(w[..., N_MAIN:N_MAIN + N_HEADS])
    return out.at[..., FG_LANE:FG_LANE + N_HEADS].set(w[..., N_MAIN + N_HEADS:N_MAIN + 2 * N_HEADS])


def kernel(x_prompt, x_sample, state_conv, state_C, state_n, state_m, ffn1_wg, ffn1_wu, ffn1_wd, ln1_g, ln1_b, w_in, b_in, gm_ln_g, gm_ln_b, gm_ws, gm_bs, conv_w, conv_b, gm_out_g, ml_out_g, w_out, ln2_g, ln2_b, ffn2_wg, ffn2_wu, ffn2_wd, ln3_g, ln3_b):
    depth = ffn1_wg.shape[0]
    bp, seq, _ = x_prompt.shape
    bs, dec_seq, _ = x_sample.shape
    assert dec_seq == DEC_SEQ and seq % TOKEN_TILE == 0 and (bs * dec_seq) % TOKEN_TILE == 0
    y_p = x_prompt.reshape(bp * seq, D_MODEL)
    y_s = x_sample.reshape(bs * dec_seq, D_MODEL)
    outs = []
    for l in range(depth):
        f1 = _ffn_weights(ffn1_wg[l], ffn1_wu[l], ffn1_wd[l])
        f2 = _ffn_weights(ffn2_wg[l], ffn2_wu[l], ffn2_wd[l])
        row = lambda a: a.reshape(1, -1)
        a_params = (*f1, row(ln1_g[l]), row(ln1_b[l]),
                    w_in[l][:, :N_MAIN].astype(BF16), row(b_in[l][:N_MAIN]),
                    _gate_columns(w_in[l]).astype(BF16), row(_gate_columns(b_in[l])),
                    row(gm_ln_g[l]), row(gm_ln_b[l]))
        c_params = (w_out[l].astype(BF16), row(ln2_g[l]), row(ln2_b[l]), *f2, row(ln3_g[l]), row(ln3_b[l]))
        mix_params = (conv_w[l], row(conv_b[l]), row(gm_out_g[l]), row(ml_out_g[l]))

        x1_p, mix_p, c_p, n_p, m_p, gmv_p, ctail_p = _stage_a_mix_prompt(
            y_p, a_params, gm_ws[l], jnp.transpose(gm_bs[l]), *mix_params, bp, seq)
        x1_s, zz_s, g_s = _stage_a(y_s, *a_params)

        ws_t = jnp.tile(gm_ws[l][:, :DEC_SEQ, :DEC_SEQ], (1, SEQ_PER_TILE, SEQ_PER_TILE))
        bs_t = jnp.tile(jnp.transpose(gm_bs[l][:, :DEC_SEQ]), (SEQ_PER_TILE, 1))
        prev = jnp.pad(state_conv[l], ((0, 0), (DEC_SEQ - (CONV_W - 1), 0), (0, 0))).reshape(bs * DEC_SEQ, 2 * GROUP_W)
        m0_tok = jnp.repeat(state_m[l], DEC_SEQ, axis=0)
        mix_s, c_s, n_s, mt_s = _mix_sample(zz_s, g_s, ws_t, bs_t, *mix_params, prev, state_C[l], state_n[l], m0_tok)

        y_p = _stage_c(mix_p, x1_p, *c_params)
        y_s = _stage_c(mix_s, x1_s, *c_params)

        zz_s3 = zz_s.reshape(bs, dec_seq, N_MAIN)
        outs.append((
            gmv_p.reshape(bp, CHUNK, N_HEADS, HEAD_DIM),
            zz_s3[:, :, GROUP_W:2 * GROUP_W].reshape(bs, dec_seq, N_HEADS, HEAD_DIM),
            ctail_p[:, 8 - (CONV_W - 1):, :],
            zz_s3[:, dec_seq - (CONV_W - 1):, 2 * GROUP_W:4 * GROUP_W],
            c_p, c_s, n_p, n_s,
            m_p[:, 0, :N_HEADS],
            mt_s.reshape(bs, dec_seq, GATE_W)[:, dec_seq - 1, :N_HEADS],
        ))
    stacked = [jnp.stack(a) for a in zip(*outs)]
    return (y_p.reshape(bp, seq, D_MODEL), y_s.reshape(bs, dec_seq, D_MODEL), *stacked)
```

```python
import functools

import jax
import jax.numpy as jnp
from jax import lax
from jax.experimental import pallas as pl
from jax.experimental.pallas import tpu as pltpu

F32 = jnp.float32
BF16 = jnp.bfloat16

D_MODEL = 1024
D_FF = 2816
HEAD_DIM = 128
N_HEADS = 4
GROUP_W = N_HEADS * HEAD_DIM
CHUNK = 128
CONV_W = 4
DEC_SEQ = 8
SEQ_PER_TILE = CHUNK // DEC_SEQ
N_MAIN = 6 * GROUP_W
GATE_W = 128
FG_LANE = 64
FF_CHUNK = 256
N_FF_CHUNKS = D_FF // FF_CHUNK
ALPHA = 2.0 ** 0.25
LN_EPS = 1e-5
TOKEN_TILE = 512
PAD_ROWS = 8
MIX_UNITS_PER_TILE = 6 * (TOKEN_TILE // CHUNK) + 3
VMEM_LIMIT_BYTES = 60 * 1024 * 1024

_NT = (((1,), (1,)), ((), ()))
_TN = (((0,), (0,)), ((), ()))


def _layer_norm(x, g, b):
    mu = jnp.mean(x, axis=-1, keepdims=True)
    xc = x - mu
    var = jnp.mean(xc * xc, axis=-1, keepdims=True)
    return xc * lax.rsqrt(var + LN_EPS) * g + b


def _head_rms(x, g):
    return x * lax.rsqrt(jnp.mean(x * x, axis=-1, keepdims=True) + LN_EPS) * g


def _silu(x):
    return x * jax.nn.sigmoid(x)


def _log_sigmoid(x):
    return jnp.minimum(x, 0.0) - jnp.log1p(jnp.exp(-jnp.abs(x)))


def _swiglu(xb_ref, wg_ref, wu_ref, wd_ref, between_chunks=None):
    acc = None
    for c in range(N_FF_CHUNKS):
        cols = slice(c * FF_CHUNK, (c + 1) * FF_CHUNK)
        xb = xb_ref[...]
        g = jnp.dot(xb, wg_ref[:, cols], preferred_element_type=F32)
        u = jnp.dot(xb, wu_ref[:, cols], preferred_element_type=F32)
        h = (_silu(g) * u).astype(BF16)
        d = jnp.dot(h, wd_ref[cols, :], preferred_element_type=F32)
        acc = d if acc is None else acc + d
        if between_chunks is not None:
            between_chunks(c)
    return acc


def _ffn1_and_project(x_ref, wg_ref, wu_ref, wd_ref, lng_ref, lnb_ref, wmain_ref, bmain_ref, wgate_ref, bgate_ref,
                      vlng_ref, vlnb_ref, x1_ref, xb_ref, store_z, store_gate, between_chunks=None):
    x = x_ref[...]
    xb_ref[...] = x.astype(BF16)
    ffn = _swiglu(xb_ref, wg_ref, wu_ref, wd_ref, between_chunks)
    x1 = _layer_norm(ALPHA * x + 0.5 * ffn, lng_ref[...], lnb_ref[...])
    x1_ref[...] = x1
    xb_ref[...] = x1.astype(BF16)
    for grp in range(N_MAIN // GROUP_W):
        cols = slice(grp * GROUP_W, (grp + 1) * GROUP_W)
        z = jnp.dot(xb_ref[...], wmain_ref[:, cols], preferred_element_type=F32) + bmain_ref[:, cols]
        if grp == 0:
            store_z(cols, jax.nn.gelu(z))
        elif grp == 1:
            z = jax.nn.gelu(z)
            for h in range(N_HEADS):
                hc = slice(h * HEAD_DIM, (h + 1) * HEAD_DIM)
                store_z(slice(GROUP_W + h * HEAD_DIM, GROUP_W + (h + 1) * HEAD_DIM),
                        _layer_norm(z[:, hc], vlng_ref[:, hc], vlnb_ref[:, hc]))
        else:
            store_z(cols, z)
    store_gate(jnp.dot(xb_ref[...], wgate_ref[...], preferred_element_type=F32) + bgate_ref[...])


def _stage_a_kernel(*refs):
    *in_refs, x1_ref, zz_ref, gate_ref, xb_ref = refs

    def store_z(cols, val):
        zz_ref[:, cols] = val

    def store_gate(val):
        gate_ref[...] = val

    _ffn1_and_project(*in_refs, x1_ref, xb_ref, store_z, store_gate)


def _const_spec(shape):
    nd = len(shape)
    return pl.BlockSpec(shape, lambda i, _nd=nd: (0,) * _nd, pipeline_mode=pl.Buffered(1))


def _stage_a(x, wg, wu, wd, lng, lnb, wmain, bmain, wgate, bgate, vlng, vlnb):
    n = x.shape[0]
    tm = TOKEN_TILE
    row = lambda w: pl.BlockSpec((tm, w), lambda i: (i, 0))
    return pl.pallas_call(
        _stage_a_kernel,
        grid=(n // tm,),
        in_specs=[row(D_MODEL),
                  _const_spec(wg.shape), _const_spec(wu.shape), _const_spec(wd.shape),
                  _const_spec(lng.shape), _const_spec(lnb.shape),
                  _const_spec(wmain.shape), _const_spec(bmain.shape),
                  _const_spec(wgate.shape), _const_spec(bgate.shape),
                  _const_spec(vlng.shape), _const_spec(vlnb.shape)],
        out_specs=[row(D_MODEL), row(N_MAIN), row(GATE_W)],
        out_shape=[jax.ShapeDtypeStruct((n, D_MODEL), F32),
                   jax.ShapeDtypeStruct((n, N_MAIN), F32),
                   jax.ShapeDtypeStruct((n, GATE_W), F32)],
        scratch_shapes=[pltpu.VMEM((tm, D_MODEL), BF16)],
        compiler_params=pltpu.CompilerParams(dimension_semantics=("arbitrary",),
                                             vmem_limit_bytes=VMEM_LIMIT_BYTES),
        name="stage_a",
    )(x, wg, wu, wd, lng, lnb, wmain, bmain, wgate, bgate, vlng, vlnb)


def _stage_c_kernel(mix_ref, x1_ref, wout_ref, ln2g_ref, ln2b_ref, wg_ref, wu_ref, wd_ref, ln3g_ref, ln3b_ref,
                    y_ref, xb_ref):
    proj = jnp.dot(mix_ref[...].astype(BF16), wout_ref[...], preferred_element_type=F32)
    y = _layer_norm(ALPHA * x1_ref[...] + proj, ln2g_ref[...], ln2b_ref[...])
    y_ref[...] = y
    xb_ref[...] = y.astype(BF16)
    ffn = _swiglu(xb_ref, wg_ref, wu_ref, wd_ref)
    y_ref[...] = _layer_norm(ALPHA * y_ref[...] + 0.5 * ffn, ln3g_ref[...], ln3b_ref[...])


def _stage_c(mix, x1, wout, ln2g, ln2b, wg, wu, wd, ln3g, ln3b):
    n = x1.shape[0]
    tm = TOKEN_TILE
    row = lambda w: pl.BlockSpec((tm, w), lambda i: (i, 0))
    return pl.pallas_call(
        _stage_c_kernel,
        grid=(n // tm,),
        in_specs=[row(D_MODEL), row(D_MODEL),
                  _const_spec(wout.shape), _const_spec(ln2g.shape), _const_spec(ln2b.shape),
                  _const_spec(wg.shape), _const_spec(wu.shape), _const_spec(wd.shape),
                  _const_spec(ln3g.shape), _const_spec(ln3b.shape)],
        out_specs=row(D_MODEL),
        out_shape=jax.ShapeDtypeStruct((n, D_MODEL), F32),
        scratch_shapes=[pltpu.VMEM((tm, D_MODEL), BF16)],
        compiler_params=pltpu.CompilerParams(dimension_semantics=("arbitrary",),
                                             vmem_limit_bytes=VMEM_LIMIT_BYTES),
        name="stage_c",
    )(mix, x1, wout, ln2g, ln2b, wg, wu, wd, ln3g, ln3b)


def _head_cols(h, base=0):
    return slice(base + h * HEAD_DIM, base + (h + 1) * HEAD_DIM)


def _gmlp_heads(zz_ref, ws_ref, bs_ref, gmg_ref, mix_ref, mask):
    for h in range(N_HEADS):
        u = zz_ref[:, _head_cols(h)]
        vn = zz_ref[:, _head_cols(h, GROUP_W)]
        w = jnp.where(mask, ws_ref[h], 0.0).astype(BF16)
        mixed = jnp.dot(w, vn.astype(BF16), preferred_element_type=F32) + bs_ref[:, h:h + 1]
        mix_ref[:, _head_cols(h)] = _head_rms(u * mixed, gmg_ref[:, _head_cols(h)])


def _intra_chunk(q, k, v, bcol, brow, irow, mcol, mask):
    dlog = jnp.where(mask, bcol - brow + irow, -jnp.inf)
    inter = bcol + mcol
    m_t = jnp.maximum(inter, jnp.max(dlog, axis=-1, keepdims=True))
    w_intra = jnp.exp(dlog - m_t)
    w_inter = jnp.exp(inter - m_t)
    s = lax.dot_general(q.astype(BF16), k.astype(BF16), _NT, preferred_element_type=F32) * w_intra
    sv = jnp.dot(s.astype(BF16), v.astype(BF16), preferred_element_type=F32)
    ssum = jnp.sum(s, axis=-1, keepdims=True)
    return m_t, w_inter, sv, ssum


def _conv_taps(x, prev_fn, cw_ref, cb_ref, row_in_seq):
    acc = x * cw_ref[CONV_W - 1:CONV_W, :] + cb_ref[...]
    for j in range(1, CONV_W):
        shifted = jnp.where(row_in_seq < j, prev_fn(j), pltpu.roll(x, j, 0))
        acc = acc + shifted * cw_ref[CONV_W - 1 - j:CONV_W - j, :]
    return acc


def _head_rms_mxu(x, g, ones):
    ss = jnp.dot((x * x).astype(BF16), ones, preferred_element_type=F32)
    return x * lax.rsqrt(ss * (1.0 / HEAD_DIM) + LN_EPS) * g


def _mix_prompt_tile(zt, gt, ws_ref, bs_ref, cw_ref, cb_ref, gmg_ref, mlg_ref,
                     mix_ref, cn_ref, m_ref, gmv_ref, ctail_ref, tail_ref):
    row = lax.broadcasted_iota(jnp.int32, (CHUNK, CHUNK), 0)
    col = lax.broadcasted_iota(jnp.int32, (CHUNK, CHUNK), 1)
    causal = col <= row
    causal_f = causal.astype(F32)
    qk_cols = slice(2 * GROUP_W, 4 * GROUP_W)
    scale = HEAD_DIM ** -0.5

    heads = range(N_HEADS)
    chunks = range(TOKEN_TILE // CHUNK)
    rows = [slice(PAD_ROWS + c * CHUNK, PAD_ROWS + (c + 1) * CHUNK) for c in chunks]
    out_rows = [slice(c * CHUNK, (c + 1) * CHUNK) for c in chunks]

    zt[0:PAD_ROWS, qk_cols] = tail_ref[...]
    ones = jnp.ones((HEAD_DIM, HEAD_DIM), BF16)
    ones_col = jnp.ones((CHUNK, HEAD_DIM), BF16)

    w_masked = [jnp.where(causal, ws_ref[h], 0.0).astype(BF16) for h in heads]
    bias = [jnp.broadcast_to(bs_ref[:, h:h + 1], (CHUNK, HEAD_DIM)) for h in heads]
    mixed = {}
    for c in chunks:
        for h in heads:
            mixed[c, h] = jnp.dot(w_masked[h], zt[rows[c], _head_cols(h, GROUP_W)].astype(BF16),
                                  preferred_element_type=F32)
        yield
    for c in chunks:
        for h in heads:
            gm = zt[rows[c], _head_cols(h)] * (mixed[c, h] + bias[h])
            mix_ref[out_rows[c], _head_cols(h)] = _head_rms_mxu(gm, gmg_ref[:, _head_cols(h)], ones)
        yield

    qk = []
    for c in chunks:
        r0 = rows[c].start
        acc = zt[rows[c], qk_cols] * cw_ref[CONV_W - 1:CONV_W, :] + cb_ref[...]
        for j in range(1, CONV_W):
            acc = acc + zt[r0 - j:r0 - j + CHUNK, qk_cols] * cw_ref[CONV_W - 1 - j:CONV_W - j, :]
        qk.append(_silu(acc))
        yield

    gates, bcum, gates_t, bcum_t = [], [], [], []
    for c in chunks:
        g = gt[out_rows[c], :]
        b = jnp.dot(causal_f, _log_sigmoid(g), precision=lax.Precision.HIGHEST, preferred_element_type=F32)
        gates.append(g)
        bcum.append(b)
        gates_t.append(g.T)
        bcum_t.append(b.T)
    yield
    icol = {(c, h): gates[c][:, h:h + 1] for c in chunks for h in heads}
    bcol = {(c, h): bcum[c][:, FG_LANE + h:FG_LANE + h + 1] for c in chunks for h in heads}

    dlog, mx = {}, {}
    for c in chunks:
        for h in heads:
            irow = gates_t[c][h:h + 1, :]
            brow = bcum_t[c][FG_LANE + h:FG_LANE + h + 1, :]
            dlog[c, h] = jnp.where(causal, bcol[c, h] - brow + irow, -jnp.inf)
            mx[c, h] = jnp.max(dlog[c, h], axis=-1, keepdims=True)
        yield

    m_row = m_ref[0]
    lane = lax.broadcasted_iota(jnp.int32, m_row.shape, 1)
    m_t, w_inter, dec, wk = {}, {}, {}, {}
    for h in heads:
        m_prev = m_row[:, h:h + 1]
        for c in chunks:
            inter = bcol[c, h] + m_prev
            m_t[c, h] = jnp.maximum(inter, mx[c, h])
            w_inter[c, h] = jnp.exp(inter - m_t[c, h])
            m_new = m_t[c, h][CHUNK - 1:CHUNK, :]
            b_last = bcol[c, h][CHUNK - 1:CHUNK, :]
            dec[c, h] = jnp.exp(b_last + m_prev - m_new)
            wk[c, h] = jnp.exp(b_last - bcol[c, h] + icol[c, h] - m_new)
            m_prev = m_new
        m_row = jnp.where(lane == h, m_prev, m_row)
    m_ref[0] = m_row
    yield

    sv, upd = {}, {}
    for c in chunks:
        for h in heads:
            q = qk[c][:, _head_cols(h)]
            k = qk[c][:, _head_cols(h, GROUP_W)] * scale
            v1 = jnp.concatenate([zt[rows[c], _head_cols(h, 4 * GROUP_W)].astype(BF16), ones_col], axis=1)
            s = lax.dot_general(q.astype(BF16), k.astype(BF16), _NT, preferred_element_type=F32) * jnp.exp(
                dlog[c, h] - m_t[c, h])
            sv[c, h] = jnp.dot(s.astype(BF16), v1, preferred_element_type=F32)
            kw = k * wk[c, h]
            upd[c, h] = lax.dot_general(kw.astype(BF16), v1, _TN, preferred_element_type=F32)
        yield

    for h in heads:
        cn_state = cn_ref[0, h]
        for c in chunks:
            q = qk[c][:, _head_cols(h)]
            qcn = jnp.dot(q.astype(BF16), cn_state.astype(BF16), preferred_element_type=F32)
            nd = sv[c, h] + jnp.broadcast_to(w_inter[c, h], (CHUNK, 2 * HEAD_DIM)) * qcn
            num, den = nd[:, :HEAD_DIM], nd[:, HEAD_DIM:]
            hid = num / jnp.maximum(jnp.abs(den), jnp.exp(-m_t[c, h]))
            o = zt[rows[c], _head_cols(h, 5 * GROUP_W)]
            mix_ref[out_rows[c], _head_cols(h, GROUP_W)] = _head_rms_mxu(jax.nn.sigmoid(o) * hid,
                                                                        mlg_ref[:, _head_cols(h)], ones)
            cn_state = dec[c, h] * cn_state + upd[c, h]
        cn_ref[0, h] = cn_state
        yield

    last8 = slice(PAD_ROWS + TOKEN_TILE - 8, PAD_ROWS + TOKEN_TILE)
    tail_ref[...] = zt[last8, qk_cols]
    ctail_ref[0] = zt[last8, qk_cols]
    gmv_ref[0] = zt[PAD_ROWS + TOKEN_TILE - CHUNK:PAD_ROWS + TOKEN_TILE, GROUP_W:2 * GROUP_W]


def _fused_prompt_kernel(tiles_per_seq, *refs):
    (x_ref, wg_ref, wu_ref, wd_ref, lng_ref, lnb_ref, wmain_ref, bmain_ref, wgate_ref, bgate_ref, vlng_ref, vlnb_ref,
     ws_ref, bs_ref, cw_ref, cb_ref, gmg_ref, mlg_ref,
     x1_ref, mix_ref, cn_ref, m_ref, gmv_ref, ctail_ref,
     xb_ref, zbuf_ref, gbuf_ref, tail_ref) = refs
    i = pl.program_id(0)
    slot_w = i % 2
    slot_r = 1 - slot_w

    @pl.when(i == 0)
    def _():
        zbuf_ref[1] = jnp.zeros(zbuf_ref.shape[1:], F32)
        gbuf_ref[1] = jnp.zeros(gbuf_ref.shape[1:], F32)

    @pl.when((i == 0) | (i % tiles_per_seq == 1))
    def _():
        cn_ref[...] = jnp.zeros_like(cn_ref)
        m_ref[...] = jnp.zeros_like(m_ref)
        tail_ref[...] = jnp.zeros_like(tail_ref)

    def store_z(cols, val):
        zbuf_ref[slot_w, PAD_ROWS:PAD_ROWS + TOKEN_TILE, cols] = val

    def store_gate(val):
        gbuf_ref[slot_w] = val

    mixer = _mix_prompt_tile(zbuf_ref.at[slot_r], gbuf_ref.at[slot_r], ws_ref, bs_ref, cw_ref, cb_ref, gmg_ref,
                             mlg_ref, mix_ref, cn_ref, m_ref, gmv_ref, ctail_ref, tail_ref)

    def mixer_units(c):
        first = (c * MIX_UNITS_PER_TILE) // N_FF_CHUNKS
        last = ((c + 1) * MIX_UNITS_PER_TILE) // N_FF_CHUNKS
        for _ in range(last - first):
            next(mixer, None)

    _ffn1_and_project(x_ref, wg_ref, wu_ref, wd_ref, lng_ref, lnb_ref, wmain_ref, bmain_ref, wgate_ref, bgate_ref,
                      vlng_ref, vlnb_ref, x1_ref, xb_ref, store_z, store_gate, mixer_units)
    assert next(mixer, "done") == "done"


def _stage_a_mix_prompt(x, a_params, ws, bs_cols, cw, cb, gmg, mlg, batch, seq):
    tm = TOKEN_TILE
    n_tiles = (batch * seq) // tm
    tiles_per_seq = seq // tm
    mix_params = (ws, bs_cols, cw, cb, gmg, mlg)
    tile_in = lambda i: (jnp.minimum(i, n_tiles - 1), 0)
    tile_out = lambda i: (jnp.maximum(i - 1, 0), 0)
    seq_of = lambda i: jnp.maximum(i - 1, 0) // tiles_per_seq
    return pl.pallas_call(
        functools.partial(_fused_prompt_kernel, tiles_per_seq),
        grid=(n_tiles + 1,),
        in_specs=[pl.BlockSpec((tm, D_MODEL), tile_in)] + [_const_spec(p.shape) for p in (*a_params, *mix_params)],
        out_specs=[pl.BlockSpec((tm, D_MODEL), tile_in),
                   pl.BlockSpec((tm, D_MODEL), tile_out),
                   pl.BlockSpec((1, N_HEADS, HEAD_DIM, 2 * HEAD_DIM), lambda i: (seq_of(i), 0, 0, 0)),
                   pl.BlockSpec((1, 1, GATE_W), lambda i: (seq_of(i), 0, 0)),
                   pl.BlockSpec((1, CHUNK, GROUP_W), lambda i: (seq_of(i), 0, 0)),
                   pl.BlockSpec((1, 8, 2 * GROUP_W), lambda i: (seq_of(i), 0, 0))],
        out_shape=[jax.ShapeDtypeStruct((batch * seq, D_MODEL), F32),
                   jax.ShapeDtypeStruct((batch * seq, D_MODEL), F32),
                   jax.ShapeDtypeStruct((batch, N_HEADS, HEAD_DIM, 2 * HEAD_DIM), F32),
                   jax.ShapeDtypeStruct((batch, 1, GATE_W), F32),
                   jax.ShapeDtypeStruct((batch, CHUNK, GROUP_W), F32),
                   jax.ShapeDtypeStruct((batch, 8, 2 * GROUP_W), F32)],
        scratch_shapes=[pltpu.VMEM((tm, D_MODEL), BF16),
                        pltpu.VMEM((2, PAD_ROWS + tm, N_MAIN), F32),
                        pltpu.VMEM((2, tm, GATE_W), F32),
                        pltpu.VMEM((8, 2 * GROUP_W), F32)],
        compiler_params=pltpu.CompilerParams(dimension_semantics=("arbitrary",),
                                             vmem_limit_bytes=VMEM_LIMIT_BYTES),
        name="stage_a_mix_prompt",
    )(x, *a_params, *mix_params)


def _mix_sample_kernel(zz_ref, gate_ref, ws_ref, bs_ref, cw_ref, cb_ref, gmg_ref, mlg_ref, prev_ref,
                       c0_ref, n0_ref, m0_ref, mix_ref, c_ref, n_ref, mt_ref):
    row = lax.broadcasted_iota(jnp.int32, (CHUNK, CHUNK), 0)
    col = lax.broadcasted_iota(jnp.int32, (CHUNK, CHUNK), 1)
    seq_shift = DEC_SEQ.bit_length() - 1
    mask = (col <= row) & ((col >> seq_shift) == (row >> seq_shift))
    last_sel = (col == (row | (DEC_SEQ - 1))).astype(F32)

    _gmlp_heads(zz_ref, ws_ref, bs_ref, gmg_ref, mix_ref, mask)

    x = zz_ref[:, 2 * GROUP_W:4 * GROUP_W]
    row_w = lax.broadcasted_iota(jnp.int32, (CHUNK, 2 * GROUP_W), 0) & (DEC_SEQ - 1)
    prev = prev_ref[...]
    qk = _silu(_conv_taps(x, lambda j: pltpu.roll(prev, CHUNK - DEC_SEQ + j, 0), cw_ref, cb_ref, row_w))

    gates = gate_ref[...]
    bcum = jnp.dot(mask.astype(F32), _log_sigmoid(gates), precision=lax.Precision.HIGHEST,
                   preferred_element_type=F32)
    gates_t = gates.T
    bcum_t = bcum.T
    lane = lax.broadcasted_iota(jnp.int32, (CHUNK, GATE_W), 1)
    mt_all = jnp.zeros((CHUNK, GATE_W), F32)
    scale = HEAD_DIM ** -0.5

    for h in range(N_HEADS):
        q = qk[:, _head_cols(h)]
        k = qk[:, _head_cols(h, GROUP_W)] * scale
        v = zz_ref[:, _head_cols(h, 4 * GROUP_W)]
        o = zz_ref[:, _head_cols(h, 5 * GROUP_W)]
        icol = gates[:, h:h + 1]
        bcol = bcum[:, FG_LANE + h:FG_LANE + h + 1]
        irow = gates_t[h:h + 1, :]
        brow = bcum_t[FG_LANE + h:FG_LANE + h + 1, :]
        m_prev = m0_ref[:, h:h + 1]

        m_t, w_inter, sv, ssum = _intra_chunk(q, k, v, bcol, brow, irow, m_prev, mask)
        qb = q.astype(BF16)
        qc_rows, qn_rows = [], []
        for s in range(SEQ_PER_TILE):
            rows = slice(s * DEC_SEQ, (s + 1) * DEC_SEQ)
            qc_rows.append(jnp.dot(qb[rows], c0_ref[s, h].astype(BF16), preferred_element_type=F32))
            qn_rows.append(jnp.sum(q[rows] * n0_ref[s, h:h + 1, :], axis=-1, keepdims=True))
        qc = jnp.concatenate(qc_rows, axis=0)
        qn = jnp.concatenate(qn_rows, axis=0)
        num = sv + w_inter * qc
        den = ssum + w_inter * qn
        hid = num / jnp.maximum(jnp.abs(den), jnp.exp(-m_t))
        mix_ref[:, _head_cols(h, GROUP_W)] = _head_rms(jax.nn.sigmoid(o) * hid, mlg_ref[:, _head_cols(h)])

        packed = jnp.where(lane == 0, m_t, jnp.where(lane == 1, bcol, 0.0))
        lastv = jnp.dot(last_sel, packed, precision=lax.Precision.HIGHEST, preferred_element_type=F32)
        m_new = lastv[:, 0:1]
        b_last = lastv[:, 1:2]
        dec = jnp.exp(b_last + m_prev - m_new)
        kw = k * jnp.exp(b_last - bcol + icol - m_new)
        kwb = kw.astype(BF16)
        vb = v.astype(BF16)
        for s in range(SEQ_PER_TILE):
            rows = slice(s * DEC_SEQ, (s + 1) * DEC_SEQ)
            dec_s = dec[s * DEC_SEQ:s * DEC_SEQ + 1, :]
            c_ref[s, h] = dec_s * c0_ref[s, h] + lax.dot_general(kwb[rows], vb[rows], _TN,
                                                                preferred_element_type=F32)
            n_ref[s, h:h + 1, :] = dec_s * n0_ref[s, h:h + 1, :] + jnp.sum(kw[rows], axis=0, keepdims=True)
        mt_all = jnp.where(lane == h, m_t, mt_all)

    mt_ref[...] = mt_all


def _mix_sample(zz, gates, ws_t, bs_cols, cw, cb, gmg, mlg, prev, c0, n0, m0_tok):
    n = zz.shape[0]
    n_tiles = n // CHUNK
    row = lambda w: pl.BlockSpec((CHUNK, w), lambda i: (i, 0))
    const = lambda shape: pl.BlockSpec(shape, lambda i, _nd=len(shape): (0,) * _nd)
    c_spec = pl.BlockSpec((SEQ_PER_TILE, N_HEADS, HEAD_DIM, HEAD_DIM), lambda i: (i, 0, 0, 0))
    n_spec = pl.BlockSpec((SEQ_PER_TILE, N_HEADS, HEAD_DIM), lambda i: (i, 0, 0))
    return pl.pallas_call(
        _mix_sample_kernel,
        grid=(n_tiles,),
        in_specs=[row(N_MAIN), row(GATE_W), const(ws_t.shape), const(bs_cols.shape), const(cw.shape),
                  const(cb.shape), const(gmg.shape), const(mlg.shape), row(2 * GROUP_W),
                  c_spec, n_spec, row(N_HEADS)],
        out_specs=[row(D_MODEL), c_spec, n_spec, row(GATE_W)],
        out_shape=[jax.ShapeDtypeStruct((n, D_MODEL), F32),
                   jax.ShapeDtypeStruct(c0.shape, F32),
                   jax.ShapeDtypeStruct(n0.shape, F32),
                   jax.ShapeDtypeStruct((n, GATE_W), F32)],
        compiler_params=pltpu.CompilerParams(dimension_semantics=("arbitrary",),
                                             vmem_limit_bytes=VMEM_LIMIT_BYTES),
        name="mix_sample",
    )(zz, gates, ws_t, bs_cols, cw, cb, gmg, mlg, prev, c0, n0, m0_tok)


def _ffn_weights(wg, wu, wd):
    return wg.astype(BF16), wu.astype(BF16), wd.astype(BF16)


def _gate_columns(w):
    out = jnp.zeros(w.shape[:-1] + (GATE_W,), w.dtype)
    out = out.at[..., 0:N_HEADS].set(w[..., N_MAIN:N_MAIN + N_HEADS])
    return out.at[..., FG_LANE:FG_LANE + N_HEADS].set(w[..., N_MAIN + N_HEADS:N_MAIN + 2 * N_HEADS])


def kernel(x_prompt, x_sample, state_conv, state_C, state_n, state_m, ffn1_wg, ffn1_wu, ffn1_wd, ln1_g, ln1_b, w_in, b_in, gm_ln_g, gm_ln_b, gm_ws, gm_bs, conv_w, conv_b, gm_out_g, ml_out_g, w_out, ln2_g, ln2_b, ffn2_wg, ffn2_wu, ffn2_wd, ln3_g, ln3_b):
    depth = ffn1_wg.shape[0]
    bp, seq, _ = x_prompt.shape
    bs, dec_seq, _ = x_sample.shape
    assert dec_seq == DEC_SEQ and seq % TOKEN_TILE == 0 and (bs * dec_seq) % TOKEN_TILE == 0
    y_p = x_prompt.reshape(bp * seq, D_MODEL)
    y_s = x_sample.reshape(bs * dec_seq, D_MODEL)
    outs = []
    for l in range(depth):
        f1 = _ffn_weights(ffn1_wg[l], ffn1_wu[l], ffn1_wd[l])
        f2 = _ffn_weights(ffn2_wg[l], ffn2_wu[l], ffn2_wd[l])
        row = lambda a: a.reshape(1, -1)
        a_params = (*f1, row(ln1_g[l]), row(ln1_b[l]),
                    w_in[l][:, :N_MAIN].astype(BF16), row(b_in[l][:N_MAIN]),
                    _gate_columns(w_in[l]).astype(BF16), row(_gate_columns(b_in[l])),
                    row(gm_ln_g[l]), row(gm_ln_b[l]))
        c_params = (w_out[l].astype(BF16), row(ln2_g[l]), row(ln2_b[l]), *f2, row(ln3_g[l]), row(ln3_b[l]))
        mix_params = (conv_w[l], row(conv_b[l]), row(gm_out_g[l]), row(ml_out_g[l]))

        x1_p, mix_p, cn_p, m_p, gmv_p, ctail_p = _stage_a_mix_prompt(
            y_p, a_params, gm_ws[l], jnp.transpose(gm_bs[l]), *mix_params, bp, seq)
        x1_s, zz_s, g_s = _stage_a(y_s, *a_params)

        ws_t = jnp.tile(gm_ws[l][:, :DEC_SEQ, :DEC_SEQ], (1, SEQ_PER_TILE, SEQ_PER_TILE))
        bs_t = jnp.tile(jnp.transpose(gm_bs[l][:, :DEC_SEQ]), (SEQ_PER_TILE, 1))
        prev = jnp.pad(state_conv[l], ((0, 0), (DEC_SEQ - (CONV_W - 1), 0), (0, 0))).reshape(bs * DEC_SEQ, 2 * GROUP_W)
        m0_tok = jnp.repeat(state_m[l], DEC_SEQ, axis=0)
        mix_s, c_s, n_s, mt_s = _mix_sample(zz_s, g_s, ws_t, bs_t, *mix_params, prev, state_C[l], state_n[l], m0_tok)

        y_p = _stage_c(mix_p, x1_p, *c_params)
        y_s = _stage_c(mix_s, x1_s, *c_params)

        zz_s3 = zz_s.reshape(bs, dec_seq, N_MAIN)
        outs.append((
            gmv_p.reshape(bp, CHUNK, N_HEADS, HEAD_DIM),
            zz_s3[:, :, GROUP_W:2 * GROUP_W].reshape(bs, dec_seq, N_HEADS, HEAD_DIM),
            ctail_p[:, 8 - (CONV_W - 1):, :],
            zz_s3[:, dec_seq - (CONV_W - 1):, 2 * GROUP_W:4 * GROUP_W],
            cn_p[..., :HEAD_DIM], c_s, cn_p[..., HEAD_DIM], n_s,
            m_p[:, 0, :N_HEADS],
            mt_s.reshape(bs, dec_seq, GATE_W)[:, dec_seq - 1, :N_HEADS],
        ))
    stacked = [jnp.stack(a) for a in zip(*outs)]
    return (y_p.reshape(bp, seq, D_MODEL), y_s.reshape(bs, dec_seq, D_MODEL), *stacked)
```

```python
import functools

import jax
import jax.numpy as jnp
from jax import lax
from jax.experimental import pallas as pl
from jax.experimental.pallas import tpu as pltpu

F32 = jnp.float32
BF16 = jnp.bfloat16

D_MODEL = 1024
D_FF = 2816
HEAD_DIM = 128
N_HEADS = 4
GROUP_W = N_HEADS * HEAD_DIM
CHUNK = 128
CONV_W = 4
DEC_SEQ = 8
SEQ_PER_TILE = CHUNK // DEC_SEQ
N_MAIN = 6 * GROUP_W
GATE_W = 128
FG_LANE = 64
FF_CHUNK = 256
N_FF_CHUNKS = D_FF // FF_CHUNK
ALPHA = 2.0 ** 0.25
LN_EPS = 1e-5
TOKEN_TILE = 512
PAD_ROWS = 8
MIX_PIECES_PER_TILE = 6 * (TOKEN_TILE // CHUNK) + N_HEADS
VMEM_LIMIT_BYTES = 60 * 1024 * 1024

_NT = (((1,), (1,)), ((), ()))
_TN = (((0,), (0,)), ((), ()))


def _layer_norm(x, g, b):
    mu = jnp.mean(x, axis=-1, keepdims=True)
    xc = x - mu
    var = jnp.mean(xc * xc, axis=-1, keepdims=True)
    return xc * lax.rsqrt(var + LN_EPS) * g + b


def _head_rms(x, g):
    return x * lax.rsqrt(jnp.mean(x * x, axis=-1, keepdims=True) + LN_EPS) * g


def _silu(x):
    return x * jax.nn.sigmoid(x)


def _log_sigmoid(x):
    return jnp.minimum(x, 0.0) - jnp.log1p(jnp.exp(-jnp.abs(x)))


def _swiglu(xb_ref, wg_ref, wu_ref, wd_ref, between_chunks=None):
    acc = None
    for c in range(N_FF_CHUNKS):
        cols = slice(c * FF_CHUNK, (c + 1) * FF_CHUNK)
        xb = xb_ref[...]
        g = jnp.dot(xb, wg_ref[:, cols], preferred_element_type=F32)
        u = jnp.dot(xb, wu_ref[:, cols], preferred_element_type=F32)
        h = (_silu(g) * u).astype(BF16)
        d = jnp.dot(h, wd_ref[cols, :], preferred_element_type=F32)
        acc = d if acc is None else acc + d
        if between_chunks is not None:
            between_chunks(c)
    return acc


def _ffn1_and_project(x_ref, wg_ref, wu_ref, wd_ref, lng_ref, lnb_ref, wmain_ref, bmain_ref, wgate_ref, bgate_ref,
                      vlng_ref, vlnb_ref, x1_ref, xb_ref, store_z, store_gate, between_chunks=None):
    x = x_ref[...]
    xb_ref[...] = x.astype(BF16)
    ffn = _swiglu(xb_ref, wg_ref, wu_ref, wd_ref, between_chunks)
    x1 = _layer_norm(ALPHA * x + 0.5 * ffn, lng_ref[...], lnb_ref[...])
    x1_ref[...] = x1
    xb_ref[...] = x1.astype(BF16)
    for grp in range(N_MAIN // GROUP_W):
        cols = slice(grp * GROUP_W, (grp + 1) * GROUP_W)
        z = jnp.dot(xb_ref[...], wmain_ref[:, cols], preferred_element_type=F32) + bmain_ref[:, cols]
        if grp == 0:
            store_z(cols, jax.nn.gelu(z))
        elif grp == 1:
            z = jax.nn.gelu(z)
            for h in range(N_HEADS):
                hc = slice(h * HEAD_DIM, (h + 1) * HEAD_DIM)
                store_z(slice(GROUP_W + h * HEAD_DIM, GROUP_W + (h + 1) * HEAD_DIM),
                        _layer_norm(z[:, hc], vlng_ref[:, hc], vlnb_ref[:, hc]))
        else:
            store_z(cols, z)
    store_gate(jnp.dot(xb_ref[...], wgate_ref[...], preferred_element_type=F32) + bgate_ref[...])


def _stage_a_kernel(*refs):
    *in_refs, x1_ref, zz_ref, gate_ref, xb_ref = refs

    def store_z(cols, val):
        zz_ref[:, cols] = val

    def store_gate(val):
        gate_ref[...] = val

    _ffn1_and_project(*in_refs, x1_ref, xb_ref, store_z, store_gate)


def _const_spec(shape):
    nd = len(shape)
    return pl.BlockSpec(shape, lambda i, _nd=nd: (0,) * _nd, pipeline_mode=pl.Buffered(1))


def _stage_a(x, wg, wu, wd, lng, lnb, wmain, bmain, wgate, bgate, vlng, vlnb):
    n = x.shape[0]
    tm = TOKEN_TILE
    row = lambda w: pl.BlockSpec((tm, w), lambda i: (i, 0))
    return pl.pallas_call(
        _stage_a_kernel,
        grid=(n // tm,),
        in_specs=[row(D_MODEL),
                  _const_spec(wg.shape), _const_spec(wu.shape), _const_spec(wd.shape),
                  _const_spec(lng.shape), _const_spec(lnb.shape),
                  _const_spec(wmain.shape), _const_spec(bmain.shape),
                  _const_spec(wgate.shape), _const_spec(bgate.shape),
                  _const_spec(vlng.shape), _const_spec(vlnb.shape)],
        out_specs=[row(D_MODEL), row(N_MAIN), row(GATE_W)],
        out_shape=[jax.ShapeDtypeStruct((n, D_MODEL), F32),
                   jax.ShapeDtypeStruct((n, N_MAIN), F32),
                   jax.ShapeDtypeStruct((n, GATE_W), F32)],
        scratch_shapes=[pltpu.VMEM((tm, D_MODEL), BF16)],
        compiler_params=pltpu.CompilerParams(dimension_semantics=("arbitrary",),
                                             vmem_limit_bytes=VMEM_LIMIT_BYTES),
        name="stage_a",
    )(x, wg, wu, wd, lng, lnb, wmain, bmain, wgate, bgate, vlng, vlnb)


def _stage_c_kernel(mix_ref, x1_ref, wout_ref, ln2g_ref, ln2b_ref, wg_ref, wu_ref, wd_ref, ln3g_ref, ln3b_ref,
                    y_ref, xb_ref):
    proj = jnp.dot(mix_ref[...].astype(BF16), wout_ref[...], preferred_element_type=F32)
    y = _layer_norm(ALPHA * x1_ref[...] + proj, ln2g_ref[...], ln2b_ref[...])
    y_ref[...] = y
    xb_ref[...] = y.astype(BF16)
    ffn = _swiglu(xb_ref, wg_ref, wu_ref, wd_ref)
    y_ref[...] = _layer_norm(ALPHA * y_ref[...] + 0.5 * ffn, ln3g_ref[...], ln3b_ref[...])


def _stage_c(mix, x1, wout, ln2g, ln2b, wg, wu, wd, ln3g, ln3b):
    n = x1.shape[0]
    tm = TOKEN_TILE
    row = lambda w: pl.BlockSpec((tm, w), lambda i: (i, 0))
    return pl.pallas_call(
        _stage_c_kernel,
        grid=(n // tm,),
        in_specs=[row(D_MODEL), row(D_MODEL),
                  _const_spec(wout.shape), _const_spec(ln2g.shape), _const_spec(ln2b.shape),
                  _const_spec(wg.shape), _const_spec(wu.shape), _const_spec(wd.shape),
                  _const_spec(ln3g.shape), _const_spec(ln3b.shape)],
        out_specs=row(D_MODEL),
        out_shape=jax.ShapeDtypeStruct((n, D_MODEL), F32),
        scratch_shapes=[pltpu.VMEM((tm, D_MODEL), BF16)],
        compiler_params=pltpu.CompilerParams(dimension_semantics=("arbitrary",),
                                             vmem_limit_bytes=VMEM_LIMIT_BYTES),
        name="stage_c",
    )(mix, x1, wout, ln2g, ln2b, wg, wu, wd, ln3g, ln3b)


def _head_cols(h, base=0):
    return slice(base + h * HEAD_DIM, base + (h + 1) * HEAD_DIM)


def _gmlp_heads(zz_ref, ws_ref, bs_ref, gmg_ref, mix_ref, mask):
    for h in range(N_HEADS):
        u = zz_ref[:, _head_cols(h)]
        vn = zz_ref[:, _head_cols(h, GROUP_W)]
        w = jnp.where(mask, ws_ref[h], 0.0).astype(BF16)
        mixed = jnp.dot(w, vn.astype(BF16), preferred_element_type=F32) + bs_ref[:, h:h + 1]
        mix_ref[:, _head_cols(h)] = _head_rms(u * mixed, gmg_ref[:, _head_cols(h)])


def _intra_chunk(q, k, v, bcol, brow, irow, mcol, mask):
    dlog = jnp.where(mask, bcol - brow + irow, -jnp.inf)
    inter = bcol + mcol
    m_t = jnp.maximum(inter, jnp.max(dlog, axis=-1, keepdims=True))
    w_intra = jnp.exp(dlog - m_t)
    w_inter = jnp.exp(inter - m_t)
    s = lax.dot_general(q.astype(BF16), k.astype(BF16), _NT, preferred_element_type=F32) * w_intra
    sv = jnp.dot(s.astype(BF16), v.astype(BF16), preferred_element_type=F32)
    ssum = jnp.sum(s, axis=-1, keepdims=True)
    return m_t, w_inter, sv, ssum


def _conv_taps(x, prev_fn, cw_ref, cb_ref, row_in_seq):
    acc = x * cw_ref[CONV_W - 1:CONV_W, :] + cb_ref[...]
    for j in range(1, CONV_W):
        shifted = jnp.where(row_in_seq < j, prev_fn(j), pltpu.roll(x, j, 0))
        acc = acc + shifted * cw_ref[CONV_W - 1 - j:CONV_W - j, :]
    return acc


def _head_rms_mxu(x, g, ones):
    ss = jnp.dot((x * x).astype(BF16), ones, preferred_element_type=F32)
    return x * lax.rsqrt(ss * (1.0 / HEAD_DIM) + LN_EPS) * g


def _mix_prompt_tile(zt, gt, ws_ref, bs_ref, cw_ref, cb_ref, gmg_ref, mlg_ref,
                     mix_ref, cn_ref, m_ref, gmv_ref, ctail_ref, tail_ref):
    row = lax.broadcasted_iota(jnp.int32, (CHUNK, CHUNK), 0)
    col = lax.broadcasted_iota(jnp.int32, (CHUNK, CHUNK), 1)
    causal = col <= row
    qk_cols = slice(2 * GROUP_W, 4 * GROUP_W)
    scale = HEAD_DIM ** -0.5

    heads = range(N_HEADS)
    chunks = range(TOKEN_TILE // CHUNK)
    rows = [slice(PAD_ROWS + c * CHUNK, PAD_ROWS + (c + 1) * CHUNK) for c in chunks]
    out_rows = [slice(c * CHUNK, (c + 1) * CHUNK) for c in chunks]

    zt[0:PAD_ROWS, qk_cols] = tail_ref[...]
    ones = jnp.ones((HEAD_DIM, HEAD_DIM), BF16)
    ones_col = jnp.ones((CHUNK, HEAD_DIM), BF16)

    w_masked = [jnp.where(causal, ws_ref[h], 0.0).astype(BF16) for h in heads]
    bias = [jnp.broadcast_to(bs_ref[:, h:h + 1], (CHUNK, HEAD_DIM)) for h in heads]
    mixed = {}
    for c in chunks:
        for h in heads:
            mixed[c, h] = jnp.dot(w_masked[h], zt[rows[c], _head_cols(h, GROUP_W)].astype(BF16),
                                  preferred_element_type=F32)
        yield
    for c in chunks:
        for h in heads:
            gm = zt[rows[c], _head_cols(h)] * (mixed[c, h] + bias[h])
            mix_ref[out_rows[c], _head_cols(h)] = _head_rms_mxu(gm, gmg_ref[:, _head_cols(h)], ones)
        yield

    qk = {}
    for c in chunks:
        r0 = rows[c].start
        for part in range(2):
            cols = slice((2 + part) * GROUP_W, (3 + part) * GROUP_W)
            wcols = slice(part * GROUP_W, (part + 1) * GROUP_W)
            acc = zt[rows[c], cols] * cw_ref[CONV_W - 1:CONV_W, wcols] + cb_ref[:, wcols]
            for j in range(1, CONV_W):
                acc = acc + zt[r0 - j:r0 - j + CHUNK, cols] * cw_ref[CONV_W - 1 - j:CONV_W - j, wcols]
            qk[c, part] = _silu(acc)
            yield

    lane8 = lax.broadcasted_iota(jnp.int32, (8, CHUNK), 1)
    triu_f = (row <= col).astype(F32)
    arow, ccol, wicol, flcol, wkcol, dec = {}, {}, {}, {}, {}, {}
    m8 = m_ref[0]
    for c in chunks:
        g_t = gt[out_rows[c], :].T
        ig_r = g_t[0:8, :]
        bcum_r = jnp.dot(_log_sigmoid(g_t[FG_LANE:FG_LANE + 8, :]), triu_f, precision=lax.Precision.HIGHEST,
                         preferred_element_type=F32)
        a_r = ig_r - bcum_r
        cm = a_r
        for sh in (1, 2, 4, 8, 16, 32, 64):
            cm = jnp.maximum(cm, jnp.where(lane8 >= sh, pltpu.roll(cm, sh, 1), -jnp.inf))
        big = jnp.maximum(m8, cm)
        m_t_r = bcum_r + big
        m_new8 = m_t_r[:, CHUNK - 1:CHUNK]
        b_last8 = bcum_r[:, CHUNK - 1:CHUNK]
        packed = jnp.concatenate([-big,
                                  jnp.exp(m8 - big),
                                  jnp.exp(-m_t_r),
                                  jnp.exp(a_r + (b_last8 - m_new8)),
                                  jnp.broadcast_to(jnp.exp(b_last8 + m8 - m_new8), (8, CHUNK)),
                                  jnp.zeros((CHUNK - 40, CHUNK), F32)], axis=0)
        cols = packed.T
        for h in heads:
            arow[c, h] = a_r[h:h + 1, :]
            ccol[c, h] = cols[:, h:h + 1]
            wicol[c, h] = cols[:, 8 + h:9 + h]
            flcol[c, h] = cols[:, 16 + h:17 + h]
            wkcol[c, h] = cols[:, 24 + h:25 + h]
            dec[c, h] = cols[0:1, 32 + h:33 + h]
        m8 = m_new8
        yield
    m_ref[0] = m8

    sv, upd = {}, {}
    for c in chunks:
        for h in heads:
            q = qk[c, 0][:, _head_cols(h)]
            k = qk[c, 1][:, _head_cols(h)] * scale
            v1 = jnp.concatenate([zt[rows[c], _head_cols(h, 4 * GROUP_W)].astype(BF16), ones_col], axis=1)
            s = lax.dot_general(q.astype(BF16), k.astype(BF16), _NT, preferred_element_type=F32) * jnp.exp(
                jnp.where(causal, ccol[c, h] + arow[c, h], -jnp.inf))
            sv[c, h] = jnp.dot(s.astype(BF16), v1, preferred_element_type=F32)
            kw = k * wkcol[c, h]
            upd[c, h] = lax.dot_general(kw.astype(BF16), v1, _TN, preferred_element_type=F32)
        yield

    for h in heads:
        cn_state = cn_ref[0, h]
        for c in chunks:
            q = qk[c, 0][:, _head_cols(h)]
            qcn = jnp.dot(q.astype(BF16), cn_state.astype(BF16), preferred_element_type=F32)
            nd = sv[c, h] + jnp.broadcast_to(wicol[c, h], (CHUNK, 2 * HEAD_DIM)) * qcn
            num, den = nd[:, :HEAD_DIM], nd[:, HEAD_DIM:]
            hid = num / jnp.maximum(jnp.abs(den), flcol[c, h])
            o = zt[rows[c], _head_cols(h, 5 * GROUP_W)]
            mix_ref[out_rows[c], _head_cols(h, GROUP_W)] = _head_rms_mxu(jax.nn.sigmoid(o) * hid,
                                                                        mlg_ref[:, _head_cols(h)], ones)
            cn_state = dec[c, h] * cn_state + upd[c, h]
        cn_ref[0, h] = cn_state
        yield

    last8 = slice(PAD_ROWS + TOKEN_TILE - 8, PAD_ROWS + TOKEN_TILE)
    tail_ref[...] = zt[last8, qk_cols]
    ctail_ref[0] = zt[last8, qk_cols]
    gmv_ref[0] = zt[PAD_ROWS + TOKEN_TILE - CHUNK:PAD_ROWS + TOKEN_TILE, GROUP_W:2 * GROUP_W]


def _fused_prompt_kernel(tiles_per_seq, *refs):
    (x_ref, wg_ref, wu_ref, wd_ref, lng_ref, lnb_ref, wmain_ref, bmain_ref, wgate_ref, bgate_ref, vlng_ref, vlnb_ref,
     ws_ref, bs_ref, cw_ref, cb_ref, gmg_ref, mlg_ref,
     x1_ref, mix_ref, cn_ref, m_ref, gmv_ref, ctail_ref,
     xb_ref, zbuf_ref, gbuf_ref, tail_ref) = refs
    i = pl.program_id(0)
    slot_w = i % 2
    slot_r = 1 - slot_w

    @pl.when(i == 0)
    def _():
        zbuf_ref[1] = jnp.zeros(zbuf_ref.shape[1:], F32)
        gbuf_ref[1] = jnp.zeros(gbuf_ref.shape[1:], F32)

    @pl.when((i == 0) | (i % tiles_per_seq == 1))
    def _():
        cn_ref[...] = jnp.zeros_like(cn_ref)
        m_ref[...] = jnp.zeros_like(m_ref)
        tail_ref[...] = jnp.zeros_like(tail_ref)

    def store_z(cols, val):
        zbuf_ref[slot_w, PAD_ROWS:PAD_ROWS + TOKEN_TILE, cols] = val

    def store_gate(val):
        gbuf_ref[slot_w] = val

    mixer = _mix_prompt_tile(zbuf_ref.at[slot_r], gbuf_ref.at[slot_r], ws_ref, bs_ref, cw_ref, cb_ref, gmg_ref,
                             mlg_ref, mix_ref, cn_ref, m_ref, gmv_ref, ctail_ref, tail_ref)

    traced = [0]

    def mixer_units(c):
        while traced[0] * N_FF_CHUNKS < (c + 1) * MIX_PIECES_PER_TILE:
            next(mixer)
            traced[0] += 1

    _ffn1_and_project(x_ref, wg_ref, wu_ref, wd_ref, lng_ref, lnb_ref, wmain_ref, bmain_ref, wgate_ref, bgate_ref,
                      vlng_ref, vlnb_ref, x1_ref, xb_ref, store_z, store_gate, mixer_units)
    after_last_piece = next(mixer, None)
    assert after_last_piece is None and traced[0] == MIX_PIECES_PER_TILE


def _stage_a_mix_prompt(x, a_params, ws, bs_cols, cw, cb, gmg, mlg, batch, seq):
    tm = TOKEN_TILE
    n_tiles = (batch * seq) // tm
    tiles_per_seq = seq // tm
    mix_params = (ws, bs_cols, cw, cb, gmg, mlg)
    tile_in = lambda i: (jnp.minimum(i, n_tiles - 1), 0)
    tile_out = lambda i: (jnp.maximum(i - 1, 0), 0)
    seq_of = lambda i: jnp.maximum(i - 1, 0) // tiles_per_seq
    return pl.pallas_call(
        functools.partial(_fused_prompt_kernel, tiles_per_seq),
        grid=(n_tiles + 1,),
        in_specs=[pl.BlockSpec((tm, D_MODEL), tile_in)] + [_const_spec(p.shape) for p in (*a_params, *mix_params)],
        out_specs=[pl.BlockSpec((tm, D_MODEL), tile_in),
                   pl.BlockSpec((tm, D_MODEL), tile_out),
                   pl.BlockSpec((1, N_HEADS, HEAD_DIM, 2 * HEAD_DIM), lambda i: (seq_of(i), 0, 0, 0)),
                   pl.BlockSpec((1, 8, 1), lambda i: (seq_of(i), 0, 0)),
                   pl.BlockSpec((1, CHUNK, GROUP_W), lambda i: (seq_of(i), 0, 0)),
                   pl.BlockSpec((1, 8, 2 * GROUP_W), lambda i: (seq_of(i), 0, 0))],
        out_shape=[jax.ShapeDtypeStruct((batch * seq, D_MODEL), F32),
                   jax.ShapeDtypeStruct((batch * seq, D_MODEL), F32),
                   jax.ShapeDtypeStruct((batch, N_HEADS, HEAD_DIM, 2 * HEAD_DIM), F32),
                   jax.ShapeDtypeStruct((batch, 8, 1), F32),
                   jax.ShapeDtypeStruct((batch, CHUNK, GROUP_W), F32),
                   jax.ShapeDtypeStruct((batch, 8, 2 * GROUP_W), F32)],
        scratch_shapes=[pltpu.VMEM((tm, D_MODEL), BF16),
                        pltpu.VMEM((2, PAD_ROWS + tm, N_MAIN), F32),
                        pltpu.VMEM((2, tm, GATE_W), F32),
                        pltpu.VMEM((8, 2 * GROUP_W), F32)],
        compiler_params=pltpu.CompilerParams(dimension_semantics=("arbitrary",),
                                             vmem_limit_bytes=VMEM_LIMIT_BYTES),
        name="stage_a_mix_prompt",
    )(x, *a_params, *mix_params)


def _mix_sample_kernel(zz_ref, gate_ref, ws_ref, bs_ref, cw_ref, cb_ref, gmg_ref, mlg_ref, prev_ref,
                       c0_ref, n0_ref, m0_ref, mix_ref, c_ref, n_ref, mt_ref):
    row = lax.broadcasted_iota(jnp.int32, (CHUNK, CHUNK), 0)
    col = lax.broadcasted_iota(jnp.int32, (CHUNK, CHUNK), 1)
    seq_shift = DEC_SEQ.bit_length() - 1
    mask = (col <= row) & ((col >> seq_shift) == (row >> seq_shift))
    last_sel = (col == (row | (DEC_SEQ - 1))).astype(F32)

    _gmlp_heads(zz_ref, ws_ref, bs_ref, gmg_ref, mix_ref, mask)

    x = zz_ref[:, 2 * GROUP_W:4 * GROUP_W]
    row_w = lax.broadcasted_iota(jnp.int32, (CHUNK, 2 * GROUP_W), 0) & (DEC_SEQ - 1)
    prev = prev_ref[...]
    qk = _silu(_conv_taps(x, lambda j: pltpu.roll(prev, CHUNK - DEC_SEQ + j, 0), cw_ref, cb_ref, row_w))

    gates = gate_ref[...]
    bcum = jnp.dot(mask.astype(F32), _log_sigmoid(gates), precision=lax.Precision.HIGHEST,
                   preferred_element_type=F32)
    gates_t = gates.T
    bcum_t = bcum.T
    lane = lax.broadcasted_iota(jnp.int32, (CHUNK, GATE_W), 1)
    mt_all = jnp.zeros((CHUNK, GATE_W), F32)
    scale = HEAD_DIM ** -0.5

    for h in range(N_HEADS):
        q = qk[:, _head_cols(h)]
        k = qk[:, _head_cols(h, GROUP_W)] * scale
        v = zz_ref[:, _head_cols(h, 4 * GROUP_W)]
        o = zz_ref[:, _head_cols(h, 5 * GROUP_W)]
        icol = gates[:, h:h + 1]
        bcol = bcum[:, FG_LANE + h:FG_LANE + h + 1]
        irow = gates_t[h:h + 1, :]
        brow = bcum_t[FG_LANE + h:FG_LANE + h + 1, :]
        m_prev = m0_ref[:, h:h + 1]

        m_t, w_inter, sv, ssum = _intra_chunk(q, k, v, bcol, brow, irow, m_prev, mask)
        qb = q.astype(BF16)
        qc_rows, qn_rows = [], []
        for s in range(SEQ_PER_TILE):
            rows = slice(s * DEC_SEQ, (s + 1) * DEC_SEQ)
            qc_rows.append(jnp.dot(qb[rows], c0_ref[s, h].astype(BF16), preferred_element_type=F32))
            qn_rows.append(jnp.sum(q[rows] * n0_ref[s, h:h + 1, :], axis=-1, keepdims=True))
        qc = jnp.concatenate(qc_rows, axis=0)
        qn = jnp.concatenate(qn_rows, axis=0)
        num = sv + w_inter * qc
        den = ssum + w_inter * qn
        hid = num / jnp.maximum(jnp.abs(den), jnp.exp(-m_t))
        mix_ref[:, _head_cols(h, GROUP_W)] = _head_rms(jax.nn.sigmoid(o) * hid, mlg_ref[:, _head_cols(h)])

        packed = jnp.where(lane == 0, m_t, jnp.where(lane == 1, bcol, 0.0))
        lastv = jnp.dot(last_sel, packed, precision=lax.Precision.HIGHEST, preferred_element_type=F32)
        m_new = lastv[:, 0:1]
        b_last = lastv[:, 1:2]
        dec = jnp.exp(b_last + m_prev - m_new)
        kw = k * jnp.exp(b_last - bcol + icol - m_new)
        kwb = kw.astype(BF16)
        vb = v.astype(BF16)
        for s in range(SEQ_PER_TILE):
            rows = slice(s * DEC_SEQ, (s + 1) * DEC_SEQ)
            dec_s = dec[s * DEC_SEQ:s * DEC_SEQ + 1, :]
            c_ref[s, h] = dec_s * c0_ref[s, h] + lax.dot_general(kwb[rows], vb[rows], _TN,
                                                                preferred_element_type=F32)
            n_ref[s, h:h + 1, :] = dec_s * n0_ref[s, h:h + 1, :] + jnp.sum(kw[rows], axis=0, keepdims=True)
        mt_all = jnp.where(lane == h, m_t, mt_all)

    mt_ref[...] = mt_all


def _mix_sample(zz, gates, ws_t, bs_cols, cw, cb, gmg, mlg, prev, c0, n0, m0_tok):
    n = zz.shape[0]
    n_tiles = n // CHUNK
    row = lambda w: pl.BlockSpec((CHUNK, w), lambda i: (i, 0))
    const = lambda shape: pl.BlockSpec(shape, lambda i, _nd=len(shape): (0,) * _nd)
    c_spec = pl.BlockSpec((SEQ_PER_TILE, N_HEADS, HEAD_DIM, HEAD_DIM), lambda i: (i, 0, 0, 0))
    n_spec = pl.BlockSpec((SEQ_PER_TILE, N_HEADS, HEAD_DIM), lambda i: (i, 0, 0))
    return pl.pallas_call(
        _mix_sample_kernel,
        grid=(n_tiles,),
        in_specs=[row(N_MAIN), row(GATE_W), const(ws_t.shape), const(bs_cols.shape), const(cw.shape),
                  const(cb.shape), const(gmg.shape), const(mlg.shape), row(2 * GROUP_W),
                  c_spec, n_spec, row(N_HEADS)],
        out_specs=[row(D_MODEL), c_spec, n_spec, row(GATE_W)],
        out_shape=[jax.ShapeDtypeStruct((n, D_MODEL), F32),
                   jax.ShapeDtypeStruct(c0.shape, F32),
                   jax.ShapeDtypeStruct(n0.shape, F32),
                   jax.ShapeDtypeStruct((n, GATE_W), F32)],
        compiler_params=pltpu.CompilerParams(dimension_semantics=("arbitrary",),
                                             vmem_limit_bytes=VMEM_LIMIT_BYTES),
        name="mix_sample",
    )(zz, gates, ws_t, bs_cols, cw, cb, gmg, mlg, prev, c0, n0, m0_tok)


def _ffn_weights(wg, wu, wd):
    return wg.astype(BF16), wu.astype(BF16), wd.astype(BF16)


def _gate_columns(w):
    out = jnp.zeros(w.shape[:-1] + (GATE_W,), w.dtype)
    out = out.at[..., 0:N_HEADS].set(w[..., N_MAIN:N_MAIN + N_HEADS])
    return out.at[..., FG_LANE:FG_LANE + N_HEADS].set(w[..., N_MAIN + N_HEADS:N_MAIN + 2 * N_HEADS])


def kernel(x_prompt, x_sample, state_conv, state_C, state_n, state_m, ffn1_wg, ffn1_wu, ffn1_wd, ln1_g, ln1_b, w_in, b_in, gm_ln_g, gm_ln_b, gm_ws, gm_bs, conv_w, conv_b, gm_out_g, ml_out_g, w_out, ln2_g, ln2_b, ffn2_wg, ffn2_wu, ffn2_wd, ln3_g, ln3_b):
    depth = ffn1_wg.shape[0]
    bp, seq, _ = x_prompt.shape
    bs, dec_seq, _ = x_sample.shape
    assert dec_seq == DEC_SEQ and seq % TOKEN_TILE == 0 and (bs * dec_seq) % TOKEN_TILE == 0
    y_p = x_prompt.reshape(bp * seq, D_MODEL)
    y_s = x_sample.reshape(bs * dec_seq, D_MODEL)
    outs = []
    for l in range(depth):
        f1 = _ffn_weights(ffn1_wg[l], ffn1_wu[l], ffn1_wd[l])
        f2 = _ffn_weights(ffn2_wg[l], ffn2_wu[l], ffn2_wd[l])
        row = lambda a: a.reshape(1, -1)
        a_params = (*f1, row(ln1_g[l]), row(ln1_b[l]),
                    w_in[l][:, :N_MAIN].astype(BF16), row(b_in[l][:N_MAIN]),
                    _gate_columns(w_in[l]).astype(BF16), row(_gate_columns(b_in[l])),
                    row(gm_ln_g[l]), row(gm_ln_b[l]))
        c_params = (w_out[l].astype(BF16), row(ln2_g[l]), row(ln2_b[l]), *f2, row(ln3_g[l]), row(ln3_b[l]))
        mix_params = (conv_w[l], row(conv_b[l]), row(gm_out_g[l]), row(ml_out_g[l]))

        x1_p, mix_p, cn_p, m_p, gmv_p, ctail_p = _stage_a_mix_prompt(
            y_p, a_params, gm_ws[l], jnp.transpose(gm_bs[l]), *mix_params, bp, seq)
        x1_s, zz_s, g_s = _stage_a(y_s, *a_params)

        ws_t = jnp.tile(gm_ws[l][:, :DEC_SEQ, :DEC_SEQ], (1, SEQ_PER_TILE, SEQ_PER_TILE))
        bs_t = jnp.tile(jnp.transpose(gm_bs[l][:, :DEC_SEQ]), (SEQ_PER_TILE, 1))
        prev = jnp.pad(state_conv[l], ((0, 0), (DEC_SEQ - (CONV_W - 1), 0), (0, 0))).reshape(bs * DEC_SEQ, 2 * GROUP_W)
        m0_tok = jnp.repeat(state_m[l], DEC_SEQ, axis=0)
        mix_s, c_s, n_s, mt_s = _mix_sample(zz_s, g_s, ws_t, bs_t, *mix_params, prev, state_C[l], state_n[l], m0_tok)

        y_p = _stage_c(mix_p, x1_p, *c_params)
        y_s = _stage_c(mix_s, x1_s, *c_params)

        zz_s3 = zz_s.reshape(bs, dec_seq, N_MAIN)
        outs.append((
            gmv_p.reshape(bp, CHUNK, N_HEADS, HEAD_DIM),
            zz_s3[:, :, GROUP_W:2 * GROUP_W].reshape(bs, dec_seq, N_HEADS, HEAD_DIM),
            ctail_p[:, 8 - (CONV_W - 1):, :],
            zz_s3[:, dec_seq - (CONV_W - 1):, 2 * GROUP_W:4 * GROUP_W],
            cn_p[..., :HEAD_DIM], c_s, cn_p[..., HEAD_DIM], n_s,
            m_p[:, :N_HEADS, 0],
            mt_s.reshape(bs, dec_seq, GATE_W)[:, dec_seq - 1, :N_HEADS],
        ))
    stacked = [jnp.stack(a) for a in zip(*outs)]
    return (y_p.reshape(bp, seq, D_MODEL), y_s.reshape(bs, dec_seq, D_MODEL), *stacked)
```

```python
import functools

import jax
import jax.numpy as jnp
from jax import lax
from jax.experimental import pallas as pl
from jax.experimental.pallas import tpu as pltpu

F32 = jnp.float32
BF16 = jnp.bfloat16

D_MODEL = 1024
D_FF = 2816
HEAD_DIM = 128
N_HEADS = 4
GROUP_W = N_HEADS * HEAD_DIM
CHUNK = 128
CONV_W = 4
DEC_SEQ = 8
SEQ_PER_TILE = CHUNK // DEC_SEQ
N_MAIN = 6 * GROUP_W
GATE_W = 128
FG_LANE = 64
FF_CHUNK = 256
N_FF_CHUNKS = D_FF // FF_CHUNK
ALPHA = 2.0 ** 0.25
LN_EPS = 1e-5
TOKEN_TILE = 512
STAGE_C_SUBTILES = 2
PAD_ROWS = 8
MIX_UNITS_PER_TILE = 6 * (TOKEN_TILE // CHUNK) + 3
VMEM_LIMIT_BYTES = 60 * 1024 * 1024

_NT = (((1,), (1,)), ((), ()))
_TN = (((0,), (0,)), ((), ()))


def _layer_norm(x, g, b):
    mu = jnp.mean(x, axis=-1, keepdims=True)
    xc = x - mu
    var = jnp.mean(xc * xc, axis=-1, keepdims=True)
    return xc * lax.rsqrt(var + LN_EPS) * g + b


def _head_rms(x, g):
    return x * lax.rsqrt(jnp.mean(x * x, axis=-1, keepdims=True) + LN_EPS) * g


def _silu(x):
    return x * jax.nn.sigmoid(x)


def _log_sigmoid(x):
    return jnp.minimum(x, 0.0) - jnp.log1p(jnp.exp(-jnp.abs(x)))


def _swiglu(xb_ref, wg_ref, wu_ref, wd_ref, between_chunks=None):
    acc = None
    for c in range(N_FF_CHUNKS):
        cols = slice(c * FF_CHUNK, (c + 1) * FF_CHUNK)
        xb = xb_ref[...]
        g = jnp.dot(xb, wg_ref[:, cols], preferred_element_type=F32)
        u = jnp.dot(xb, wu_ref[:, cols], preferred_element_type=F32)
        h = (_silu(g) * u).astype(BF16)
        d = jnp.dot(h, wd_ref[cols, :], preferred_element_type=F32)
        acc = d if acc is None else acc + d
        if between_chunks is not None:
            between_chunks(c)
    return acc


def _ffn1_and_project(x_ref, wg_ref, wu_ref, wd_ref, lng_ref, lnb_ref, wmain_ref, bmain_ref, wgate_ref, bgate_ref,
                      vlng_ref, vlnb_ref, x1_ref, xb_ref, store_z, store_gate, between_chunks=None):
    x = x_ref[...]
    xb_ref[...] = x.astype(BF16)
    ffn = _swiglu(xb_ref, wg_ref, wu_ref, wd_ref, between_chunks)
    x1 = _layer_norm(ALPHA * x + 0.5 * ffn, lng_ref[...], lnb_ref[...])
    x1_ref[...] = x1
    xb_ref[...] = x1.astype(BF16)
    for grp in range(N_MAIN // GROUP_W):
        cols = slice(grp * GROUP_W, (grp + 1) * GROUP_W)
        z = jnp.dot(xb_ref[...], wmain_ref[:, cols], preferred_element_type=F32) + bmain_ref[:, cols]
        if grp == 0:
            store_z(cols, jax.nn.gelu(z))
        elif grp == 1:
            z = jax.nn.gelu(z)
            for h in range(N_HEADS):
                hc = slice(h * HEAD_DIM, (h + 1) * HEAD_DIM)
                store_z(slice(GROUP_W + h * HEAD_DIM, GROUP_W + (h + 1) * HEAD_DIM),
                        _layer_norm(z[:, hc], vlng_ref[:, hc], vlnb_ref[:, hc]))
        else:
            store_z(cols, z)
    store_gate(jnp.dot(xb_ref[...], wgate_ref[...], preferred_element_type=F32) + bgate_ref[...])


def _stage_a_kernel(*refs):
    *in_refs, x1_ref, zz_ref, gate_ref, xb_ref = refs

    def store_z(cols, val):
        zz_ref[:, cols] = val

    def store_gate(val):
        gate_ref[...] = val

    _ffn1_and_project(*in_refs, x1_ref, xb_ref, store_z, store_gate)


def _const_spec(shape):
    nd = len(shape)
    return pl.BlockSpec(shape, lambda i, _nd=nd: (0,) * _nd, pipeline_mode=pl.Buffered(1))


def _stage_a(x, wg, wu, wd, lng, lnb, wmain, bmain, wgate, bgate, vlng, vlnb):
    n = x.shape[0]
    tm = TOKEN_TILE
    row = lambda w: pl.BlockSpec((tm, w), lambda i: (i, 0))
    return pl.pallas_call(
        _stage_a_kernel,
        grid=(n // tm,),
        in_specs=[row(D_MODEL),
                  _const_spec(wg.shape), _const_spec(wu.shape), _const_spec(wd.shape),
                  _const_spec(lng.shape), _const_spec(lnb.shape),
                  _const_spec(wmain.shape), _const_spec(bmain.shape),
                  _const_spec(wgate.shape), _const_spec(bgate.shape),
                  _const_spec(vlng.shape), _const_spec(vlnb.shape)],
        out_specs=[row(D_MODEL), row(N_MAIN), row(GATE_W)],
        out_shape=[jax.ShapeDtypeStruct((n, D_MODEL), F32),
                   jax.ShapeDtypeStruct((n, N_MAIN), F32),
                   jax.ShapeDtypeStruct((n, GATE_W), F32)],
        scratch_shapes=[pltpu.VMEM((tm, D_MODEL), BF16)],
        compiler_params=pltpu.CompilerParams(dimension_semantics=("arbitrary",),
                                             vmem_limit_bytes=VMEM_LIMIT_BYTES),
        name="stage_a",
    )(x, wg, wu, wd, lng, lnb, wmain, bmain, wgate, bgate, vlng, vlnb)


def _stage_c_kernel(mix_ref, x1_ref, wout_ref, ln2g_ref, ln2b_ref, wg_ref, wu_ref, wd_ref, ln3g_ref, ln3b_ref,
                    y_ref, xb_ref):
    tiles = [slice(s * TOKEN_TILE, (s + 1) * TOKEN_TILE) for s in range(STAGE_C_SUBTILES)]
    for rows in tiles:
        proj = jnp.dot(mix_ref[rows, :].astype(BF16), wout_ref[...], preferred_element_type=F32)
        y = _layer_norm(ALPHA * x1_ref[rows, :] + proj, ln2g_ref[...], ln2b_ref[...])
        y_ref[rows, :] = y
        xb_ref[rows, :] = y.astype(BF16)
    for rows in tiles:
        ffn = _swiglu(xb_ref.at[rows, :], wg_ref, wu_ref, wd_ref)
        y_ref[rows, :] = _layer_norm(ALPHA * y_ref[rows, :] + 0.5 * ffn, ln3g_ref[...], ln3b_ref[...])


def _stage_c(mix, x1, wout, ln2g, ln2b, wg, wu, wd, ln3g, ln3b):
    n = x1.shape[0]
    tm = TOKEN_TILE * STAGE_C_SUBTILES
    row = lambda w: pl.BlockSpec((tm, w), lambda i: (i, 0))
    return pl.pallas_call(
        _stage_c_kernel,
        grid=(n // tm,),
        in_specs=[row(D_MODEL), row(D_MODEL),
                  _const_spec(wout.shape), _const_spec(ln2g.shape), _const_spec(ln2b.shape),
                  _const_spec(wg.shape), _const_spec(wu.shape), _const_spec(wd.shape),
                  _const_spec(ln3g.shape), _const_spec(ln3b.shape)],
        out_specs=row(D_MODEL),
        out_shape=jax.ShapeDtypeStruct((n, D_MODEL), F32),
        scratch_shapes=[pltpu.VMEM((tm, D_MODEL), BF16)],
        compiler_params=pltpu.CompilerParams(dimension_semantics=("arbitrary",),
                                             vmem_limit_bytes=VMEM_LIMIT_BYTES),
        name="stage_c",
    )(mix, x1, wout, ln2g, ln2b, wg, wu, wd, ln3g, ln3b)


def _head_cols(h, base=0):
    return slice(base + h * HEAD_DIM, base + (h + 1) * HEAD_DIM)


def _gmlp_heads(zz_ref, ws_ref, bs_ref, gmg_ref, mix_ref, mask):
    for h in range(N_HEADS):
        u = zz_ref[:, _head_cols(h)]
        vn = zz_ref[:, _head_cols(h, GROUP_W)]
        w = jnp.where(mask, ws_ref[h], 0.0).astype(BF16)
        mixed = jnp.dot(w, vn.astype(BF16), preferred_element_type=F32) + bs_ref[:, h:h + 1]
        mix_ref[:, _head_cols(h)] = _head_rms(u * mixed, gmg_ref[:, _head_cols(h)])


def _intra_chunk(q, k, v, bcol, brow, irow, mcol, mask):
    dlog = jnp.where(mask, bcol - brow + irow, -jnp.inf)
    inter = bcol + mcol
    m_t = jnp.maximum(inter, jnp.max(dlog, axis=-1, keepdims=True))
    w_intra = jnp.exp(dlog - m_t)
    w_inter = jnp.exp(inter - m_t)
    s = lax.dot_general(q.astype(BF16), k.astype(BF16), _NT, preferred_element_type=F32) * w_intra
    sv = jnp.dot(s.astype(BF16), v.astype(BF16), preferred_element_type=F32)
    ssum = jnp.sum(s, axis=-1, keepdims=True)
    return m_t, w_inter, sv, ssum


def _conv_taps(x, prev_fn, cw_ref, cb_ref, row_in_seq):
    acc = x * cw_ref[CONV_W - 1:CONV_W, :] + cb_ref[...]
    for j in range(1, CONV_W):
        shifted = jnp.where(row_in_seq < j, prev_fn(j), pltpu.roll(x, j, 0))
        acc = acc + shifted * cw_ref[CONV_W - 1 - j:CONV_W - j, :]
    return acc


def _head_rms_mxu(x, g, ones):
    ss = jnp.dot((x * x).astype(BF16), ones, preferred_element_type=F32)
    return x * lax.rsqrt(ss * (1.0 / HEAD_DIM) + LN_EPS) * g


def _mix_prompt_tile(zt, gt, ws_ref, bs_ref, cw_ref, cb_ref, gmg_ref, mlg_ref,
                     mix_ref, cn_ref, m_ref, gmv_ref, ctail_ref, tail_ref):
    row = lax.broadcasted_iota(jnp.int32, (CHUNK, CHUNK), 0)
    col = lax.broadcasted_iota(jnp.int32, (CHUNK, CHUNK), 1)
    causal = col <= row
    causal_f = causal.astype(F32)
    qk_cols = slice(2 * GROUP_W, 4 * GROUP_W)
    scale = HEAD_DIM ** -0.5

    heads = range(N_HEADS)
    chunks = range(TOKEN_TILE // CHUNK)
    rows = [slice(PAD_ROWS + c * CHUNK, PAD_ROWS + (c + 1) * CHUNK) for c in chunks]
    out_rows = [slice(c * CHUNK, (c + 1) * CHUNK) for c in chunks]

    zt[0:PAD_ROWS, qk_cols] = tail_ref[...]
    ones = jnp.ones((HEAD_DIM, HEAD_DIM), BF16)
    ones_col = jnp.ones((CHUNK, HEAD_DIM), BF16)

    w_masked = [jnp.where(causal, ws_ref[h], 0.0).astype(BF16) for h in heads]
    bias = [jnp.broadcast_to(bs_ref[:, h:h + 1], (CHUNK, HEAD_DIM)) for h in heads]
    mixed = {}
    for c in chunks:
        for h in heads:
            mixed[c, h] = jnp.dot(w_masked[h], zt[rows[c], _head_cols(h, GROUP_W)].astype(BF16),
                                  preferred_element_type=F32)
        yield
    for c in chunks:
        for h in heads:
            gm = zt[rows[c], _head_cols(h)] * (mixed[c, h] + bias[h])
            mix_ref[out_rows[c], _head_cols(h)] = _head_rms_mxu(gm, gmg_ref[:, _head_cols(h)], ones)
        yield

    qk = []
    for c in chunks:
        r0 = rows[c].start
        acc = zt[rows[c], qk_cols] * cw_ref[CONV_W - 1:CONV_W, :] + cb_ref[...]
        for j in range(1, CONV_W):
            acc = acc + zt[r0 - j:r0 - j + CHUNK, qk_cols] * cw_ref[CONV_W - 1 - j:CONV_W - j, :]
        qk.append(_silu(acc))
        yield

    gates, bcum, gates_t, bcum_t = [], [], [], []
    for c in chunks:
        g = gt[out_rows[c], :]
        b = jnp.dot(causal_f, _log_sigmoid(g), precision=lax.Precision.HIGHEST, preferred_element_type=F32)
        gates.append(g)
        bcum.append(b)
        gates_t.append(g.T)
        bcum_t.append(b.T)
    yield
    icol = {(c, h): gates[c][:, h:h + 1] for c in chunks for h in heads}
    bcol = {(c, h): bcum[c][:, FG_LANE + h:FG_LANE + h + 1] for c in chunks for h in heads}

    dlog, mx = {}, {}
    for c in chunks:
        for h in heads:
            irow = gates_t[c][h:h + 1, :]
            brow = bcum_t[c][FG_LANE + h:FG_LANE + h + 1, :]
            dlog[c, h] = jnp.where(causal, bcol[c, h] - brow + irow, -jnp.inf)
            mx[c, h] = jnp.max(dlog[c, h], axis=-1, keepdims=True)
        yield

    m_row = m_ref[0]
    lane = lax.broadcasted_iota(jnp.int32, m_row.shape, 1)
    m_t, w_inter, dec, wk = {}, {}, {}, {}
    for h in heads:
        m_prev = m_row[:, h:h + 1]
        for c in chunks:
            inter = bcol[c, h] + m_prev
            m_t[c, h] = jnp.maximum(inter, mx[c, h])
            w_inter[c, h] = jnp.exp(inter - m_t[c, h])
            m_new = m_t[c, h][CHUNK - 1:CHUNK, :]
            b_last = bcol[c, h][CHUNK - 1:CHUNK, :]
            dec[c, h] = jnp.exp(b_last + m_prev - m_new)
            wk[c, h] = jnp.exp(b_last - bcol[c, h] + icol[c, h] - m_new)
            m_prev = m_new
        m_row = jnp.where(lane == h, m_prev, m_row)
    m_ref[0] = m_row
    yield

    sv, upd = {}, {}
    for c in chunks:
        for h in heads:
            q = qk[c][:, _head_cols(h)]
            k = qk[c][:, _head_cols(h, GROUP_W)] * scale
            v1 = jnp.concatenate([zt[rows[c], _head_cols(h, 4 * GROUP_W)].astype(BF16), ones_col], axis=1)
            s = lax.dot_general(q.astype(BF16), k.astype(BF16), _NT, preferred_element_type=F32) * jnp.exp(
                dlog[c, h] - m_t[c, h])
            sv[c, h] = jnp.dot(s.astype(BF16), v1, preferred_element_type=F32)
            kw = k * wk[c, h]
            upd[c, h] = lax.dot_general(kw.astype(BF16), v1, _TN, preferred_element_type=F32)
        yield

    for h in heads:
        cn_state = cn_ref[0, h]
        for c in chunks:
            q = qk[c][:, _head_cols(h)]
            qcn = jnp.dot(q.astype(BF16), cn_state.astype(BF16), preferred_element_type=F32)
            nd = sv[c, h] + jnp.broadcast_to(w_inter[c, h], (CHUNK, 2 * HEAD_DIM)) * qcn
            num, den = nd[:, :HEAD_DIM], nd[:, HEAD_DIM:]
            hid = num / jnp.maximum(jnp.abs(den), jnp.exp(-m_t[c, h]))
            o = zt[rows[c], _head_cols(h, 5 * GROUP_W)]
            mix_ref[out_rows[c], _head_cols(h, GROUP_W)] = _head_rms_mxu(jax.nn.sigmoid(o) * hid,
                                                                        mlg_ref[:, _head_cols(h)], ones)
            cn_state = dec[c, h] * cn_state + upd[c, h]
        cn_ref[0, h] = cn_state
        yield

    last8 = slice(PAD_ROWS + TOKEN_TILE - 8, PAD_ROWS + TOKEN_TILE)
    tail_ref[...] = zt[last8, qk_cols]
    ctail_ref[0] = zt[last8, qk_cols]
    gmv_ref[0] = zt[PAD_ROWS + TOKEN_TILE - CHUNK:PAD_ROWS + TOKEN_TILE, GROUP_W:2 * GROUP_W]


def _fused_prompt_kernel(tiles_per_seq, *refs):
    (x_ref, wg_ref, wu_ref, wd_ref, lng_ref, lnb_ref, wmain_ref, bmain_ref, wgate_ref, bgate_ref, vlng_ref, vlnb_ref,
     ws_ref, bs_ref, cw_ref, cb_ref, gmg_ref, mlg_ref,
     x1_ref, mix_ref, cn_ref, m_ref, gmv_ref, ctail_ref,
     xb_ref, zbuf_ref, gbuf_ref, tail_ref) = refs
    i = pl.program_id(0)
    slot_w = i % 2
    slot_r = 1 - slot_w

    @pl.when(i == 0)
    def _():
        zbuf_ref[1] = jnp.zeros(zbuf_ref.shape[1:], F32)
        gbuf_ref[1] = jnp.zeros(gbuf_ref.shape[1:], F32)

    @pl.when((i == 0) | (i % tiles_per_seq == 1))
    def _():
        cn_ref[...] = jnp.zeros_like(cn_ref)
        m_ref[...] = jnp.zeros_like(m_ref)
        tail_ref[...] = jnp.zeros_like(tail_ref)

    def store_z(cols, val):
        zbuf_ref[slot_w, PAD_ROWS:PAD_ROWS + TOKEN_TILE, cols] = val

    def store_gate(val):
        gbuf_ref[slot_w] = val

    mixer = _mix_prompt_tile(zbuf_ref.at[slot_r], gbuf_ref.at[slot_r], ws_ref, bs_ref, cw_ref, cb_ref, gmg_ref,
                             mlg_ref, mix_ref, cn_ref, m_ref, gmv_ref, ctail_ref, tail_ref)

    def mixer_units(c):
        first = (c * MIX_UNITS_PER_TILE) // N_FF_CHUNKS
        last = ((c + 1) * MIX_UNITS_PER_TILE) // N_FF_CHUNKS
        for _ in range(last - first):
            next(mixer, None)

    _ffn1_and_project(x_ref, wg_ref, wu_ref, wd_ref, lng_ref, lnb_ref, wmain_ref, bmain_ref, wgate_ref, bgate_ref,
                      vlng_ref, vlnb_ref, x1_ref, xb_ref, store_z, store_gate, mixer_units)
    assert next(mixer, "done") == "done"


def _stage_a_mix_prompt(x, a_params, ws, bs_cols, cw, cb, gmg, mlg, batch, seq):
    tm = TOKEN_TILE
    n_tiles = (batch * seq) // tm
    tiles_per_seq = seq // tm
    mix_params = (ws, bs_cols, cw, cb, gmg, mlg)
    tile_in = lambda i: (jnp.minimum(i, n_tiles - 1), 0)
    tile_out = lambda i: (jnp.maximum(i - 1, 0), 0)
    seq_of = lambda i: jnp.maximum(i - 1, 0) // tiles_per_seq
    return pl.pallas_call(
        functools.partial(_fused_prompt_kernel, tiles_per_seq),
        grid=(n_tiles + 1,),
        in_specs=[pl.BlockSpec((tm, D_MODEL), tile_in)] + [_const_spec(p.shape) for p in (*a_params, *mix_params)],
        out_specs=[pl.BlockSpec((tm, D_MODEL), tile_in),
                   pl.BlockSpec((tm, D_MODEL), tile_out),
                   pl.BlockSpec((1, N_HEADS, HEAD_DIM, 2 * HEAD_DIM), lambda i: (seq_of(i), 0, 0, 0)),
                   pl.BlockSpec((1, 1, GATE_W), lambda i: (seq_of(i), 0, 0)),
                   pl.BlockSpec((1, CHUNK, GROUP_W), lambda i: (seq_of(i), 0, 0)),
                   pl.BlockSpec((1, 8, 2 * GROUP_W), lambda i: (seq_of(i), 0, 0))],
        out_shape=[jax.ShapeDtypeStruct((batch * seq, D_MODEL), F32),
                   jax.ShapeDtypeStruct((batch * seq, D_MODEL), F32),
                   jax.ShapeDtypeStruct((batch, N_HEADS, HEAD_DIM, 2 * HEAD_DIM), F32),
                   jax.ShapeDtypeStruct((batch, 1, GATE_W), F32),
                   jax.ShapeDtypeStruct((batch, CHUNK, GROUP_W), F32),
                   jax.ShapeDtypeStruct((batch, 8, 2 * GROUP_W), F32)],
        scratch_shapes=[pltpu.VMEM((tm, D_MODEL), BF16),
                        pltpu.VMEM((2, PAD_ROWS + tm, N_MAIN), F32),
                        pltpu.VMEM((2, tm, GATE_W), F32),
                        pltpu.VMEM((8, 2 * GROUP_W), F32)],
        compiler_params=pltpu.CompilerParams(dimension_semantics=("arbitrary",),
                                             vmem_limit_bytes=VMEM_LIMIT_BYTES),
        name="stage_a_mix_prompt",
    )(x, *a_params, *mix_params)


def _mix_sample_kernel(zz_ref, gate_ref, ws_ref, bs_ref, cw_ref, cb_ref, gmg_ref, mlg_ref, prev_ref,
                       c0_ref, n0_ref, m0_ref, mix_ref, c_ref, n_ref, mt_ref):
    row = lax.broadcasted_iota(jnp.int32, (CHUNK, CHUNK), 0)
    col = lax.broadcasted_iota(jnp.int32, (CHUNK, CHUNK), 1)
    seq_shift = DEC_SEQ.bit_length() - 1
    mask = (col <= row) & ((col >> seq_shift) == (row >> seq_shift))
    last_sel = (col == (row | (DEC_SEQ - 1))).astype(F32)

    _gmlp_heads(zz_ref, ws_ref, bs_ref, gmg_ref, mix_ref, mask)

    x = zz_ref[:, 2 * GROUP_W:4 * GROUP_W]
    row_w = lax.broadcasted_iota(jnp.int32, (CHUNK, 2 * GROUP_W), 0) & (DEC_SEQ - 1)
    prev = prev_ref[...]
    qk = _silu(_conv_taps(x, lambda j: pltpu.roll(prev, CHUNK - DEC_SEQ + j, 0), cw_ref, cb_ref, row_w))

    gates = gate_ref[...]
    bcum = jnp.dot(mask.astype(F32), _log_sigmoid(gates), precision=lax.Precision.HIGHEST,
                   preferred_element_type=F32)
    gates_t = gates.T
    bcum_t = bcum.T
    lane = lax.broadcasted_iota(jnp.int32, (CHUNK, GATE_W), 1)
    mt_all = jnp.zeros((CHUNK, GATE_W), F32)
    scale = HEAD_DIM ** -0.5

    for h in range(N_HEADS):
        q = qk[:, _head_cols(h)]
        k = qk[:, _head_cols(h, GROUP_W)] * scale
        v = zz_ref[:, _head_cols(h, 4 * GROUP_W)]
        o = zz_ref[:, _head_cols(h, 5 * GROUP_W)]
        icol = gates[:, h:h + 1]
        bcol = bcum[:, FG_LANE + h:FG_LANE + h + 1]
        irow = gates_t[h:h + 1, :]
        brow = bcum_t[FG_LANE + h:FG_LANE + h + 1, :]
        m_prev = m0_ref[:, h:h + 1]

        m_t, w_inter, sv, ssum = _intra_chunk(q, k, v, bcol, brow, irow, m_prev, mask)
        qb = q.astype(BF16)
        qc_rows, qn_rows = [], []
        for s in range(SEQ_PER_TILE):
            rows = slice(s * DEC_SEQ, (s + 1) * DEC_SEQ)
            qc_rows.append(jnp.dot(qb[rows], c0_ref[s, h].astype(BF16), preferred_element_type=F32))
            qn_rows.append(jnp.sum(q[rows] * n0_ref[s, h:h + 1, :], axis=-1, keepdims=True))
        qc = jnp.concatenate(qc_rows, axis=0)
        qn = jnp.concatenate(qn_rows, axis=0)
        num = sv + w_inter * qc
        den = ssum + w_inter * qn
        hid = num / jnp.maximum(jnp.abs(den), jnp.exp(-m_t))
        mix_ref[:, _head_cols(h, GROUP_W)] = _head_rms(jax.nn.sigmoid(o) * hid, mlg_ref[:, _head_cols(h)])

        packed = jnp.where(lane == 0, m_t, jnp.where(lane == 1, bcol, 0.0))
        lastv = jnp.dot(last_sel, packed, precision=lax.Precision.HIGHEST, preferred_element_type=F32)
        m_new = lastv[:, 0:1]
        b_last = lastv[:, 1:2]
        dec = jnp.exp(b_last + m_prev - m_new)
        kw = k * jnp.exp(b_last - bcol + icol - m_new)
        kwb = kw.astype(BF16)
        vb = v.astype(BF16)
        for s in range(SEQ_PER_TILE):
            rows = slice(s * DEC_SEQ, (s + 1) * DEC_SEQ)
            dec_s = dec[s * DEC_SEQ:s * DEC_SEQ + 1, :]
            c_ref[s, h] = dec_s * c0_ref[s, h] + lax.dot_general(kwb[rows], vb[rows], _TN,
                                                                preferred_element_type=F32)
            n_ref[s, h:h + 1, :] = dec_s * n0_ref[s, h:h + 1, :] + jnp.sum(kw[rows], axis=0, keepdims=True)
        mt_all = jnp.where(lane == h, m_t, mt_all)

    mt_ref[...] = mt_all


def _mix_sample(zz, gates, ws_t, bs_cols, cw, cb, gmg, mlg, prev, c0, n0, m0_tok):
    n = zz.shape[0]
    n_tiles = n // CHUNK
    row = lambda w: pl.BlockSpec((CHUNK, w), lambda i: (i, 0))
    const = lambda shape: pl.BlockSpec(shape, lambda i, _nd=len(shape): (0,) * _nd)
    c_spec = pl.BlockSpec((SEQ_PER_TILE, N_HEADS, HEAD_DIM, HEAD_DIM), lambda i: (i, 0, 0, 0))
    n_spec = pl.BlockSpec((SEQ_PER_TILE, N_HEADS, HEAD_DIM), lambda i: (i, 0, 0))
    return pl.pallas_call(
        _mix_sample_kernel,
        grid=(n_tiles,),
        in_specs=[row(N_MAIN), row(GATE_W), const(ws_t.shape), const(bs_cols.shape), const(cw.shape),
                  const(cb.shape), const(gmg.shape), const(mlg.shape), row(2 * GROUP_W),
                  c_spec, n_spec, row(N_HEADS)],
        out_specs=[row(D_MODEL), c_spec, n_spec, row(GATE_W)],
        out_shape=[jax.ShapeDtypeStruct((n, D_MODEL), F32),
                   jax.ShapeDtypeStruct(c0.shape, F32),
                   jax.ShapeDtypeStruct(n0.shape, F32),
                   jax.ShapeDtypeStruct((n, GATE_W), F32)],
        compiler_params=pltpu.CompilerParams(dimension_semantics=("arbitrary",),
                                             vmem_limit_bytes=VMEM_LIMIT_BYTES),
        name="mix_sample",
    )(zz, gates, ws_t, bs_cols, cw, cb, gmg, mlg, prev, c0, n0, m0_tok)


def _ffn_weights(wg, wu, wd):
    return wg.astype(BF16), wu.astype(BF16), wd.astype(BF16)


def _gate_columns(w):
    out = jnp.zeros(w.shape[:-1] + (GATE_W,), w.dtype)
    out = out.at[..., 0:N_HEADS].set(w[..., N_MAIN:N_MAIN + N_HEADS])
    return out.at[..., FG_LANE:FG_LANE + N_HEADS].set(w[..., N_MAIN + N_HEADS:N_MAIN + 2 * N_HEADS])


def kernel(x_prompt, x_sample, state_conv, state_C, state_n, state_m, ffn1_wg, ffn1_wu, ffn1_wd, ln1_g, ln1_b, w_in, b_in, gm_ln_g, gm_ln_b, gm_ws, gm_bs, conv_w, conv_b, gm_out_g, ml_out_g, w_out, ln2_g, ln2_b, ffn2_wg, ffn2_wu, ffn2_wd, ln3_g, ln3_b):
    depth = ffn1_wg.shape[0]
    bp, seq, _ = x_prompt.shape
    bs, dec_seq, _ = x_sample.shape
    assert dec_seq == DEC_SEQ and seq % TOKEN_TILE == 0
    assert (bs * dec_seq) % (TOKEN_TILE * STAGE_C_SUBTILES) == 0 and (bp * seq) % (TOKEN_TILE * STAGE_C_SUBTILES) == 0
    y_p = x_prompt.reshape(bp * seq, D_MODEL)
    y_s = x_sample.reshape(bs * dec_seq, D_MODEL)
    outs = []
    for l in range(depth):
        f1 = _ffn_weights(ffn1_wg[l], ffn1_wu[l], ffn1_wd[l])
        f2 = _ffn_weights(ffn2_wg[l], ffn2_wu[l], ffn2_wd[l])
        row = lambda a: a.reshape(1, -1)
        a_params = (*f1, row(ln1_g[l]), row(ln1_b[l]),
                    w_in[l].astype(BF16), row(b_in[l]),
                    _gate_columns(w_in[l]).astype(BF16), row(_gate_columns(b_in[l])),
                    row(gm_ln_g[l]), row(gm_ln_b[l]))
        c_params = (w_out[l].astype(BF16), row(ln2_g[l]), row(ln2_b[l]), *f2, row(ln3_g[l]), row(ln3_b[l]))
        mix_params = (conv_w[l], row(conv_b[l]), row(gm_out_g[l]), row(ml_out_g[l]))

        x1_p, mix_p, cn_p, m_p, gmv_p, ctail_p = _stage_a_mix_prompt(
            y_p, a_params, gm_ws[l], jnp.transpose(gm_bs[l]), *mix_params, bp, seq)
        x1_s, zz_s, g_s = _stage_a(y_s, *a_params)

        ws_t = jnp.tile(gm_ws[l][:, :DEC_SEQ, :DEC_SEQ], (1, SEQ_PER_TILE, SEQ_PER_TILE))
        bs_t = jnp.tile(jnp.transpose(gm_bs[l][:, :DEC_SEQ]), (SEQ_PER_TILE, 1))
        prev = jnp.pad(state_conv[l], ((0, 0), (DEC_SEQ - (CONV_W - 1), 0), (0, 0))).reshape(bs * DEC_SEQ, 2 * GROUP_W)
        m0_tok = jnp.repeat(state_m[l], DEC_SEQ, axis=0)
        mix_s, c_s, n_s, mt_s = _mix_sample(zz_s, g_s, ws_t, bs_t, *mix_params, prev, state_C[l], state_n[l], m0_tok)

        y_p = _stage_c(mix_p, x1_p, *c_params)
        y_s = _stage_c(mix_s, x1_s, *c_params)

        zz_s3 = zz_s.reshape(bs, dec_seq, N_MAIN)
        outs.append((
            gmv_p.reshape(bp, CHUNK, N_HEADS, HEAD_DIM),
            zz_s3[:, :, GROUP_W:2 * GROUP_W].reshape(bs, dec_seq, N_HEADS, HEAD_DIM),
            ctail_p[:, 8 - (CONV_W - 1):, :],
            zz_s3[:, dec_seq - (CONV_W - 1):, 2 * GROUP_W:4 * GROUP_W],
            cn_p[..., :HEAD_DIM], c_s, cn_p[..., HEAD_DIM], n_s,
            m_p[:, 0, :N_HEADS],
            mt_s.reshape(bs, dec_seq, GATE_W)[:, dec_seq - 1, :N_HEADS],
        ))
    stacked = [jnp.stack(a) for a in zip(*outs)]
    return (y_p.reshape(bp, seq, D_MODEL), y_s.reshape(bs, dec_seq, D_MODEL), *stacked)
```

```python
import functools

import jax
import jax.numpy as jnp
from jax import lax
from jax.experimental import pallas as pl
from jax.experimental.pallas import tpu as pltpu

F32 = jnp.float32
BF16 = jnp.bfloat16

D_MODEL = 1024
D_FF = 2816
HEAD_DIM = 128
N_HEADS = 4
GROUP_W = N_HEADS * HEAD_DIM
CHUNK = 128
CONV_W = 4
DEC_SEQ = 8
SEQ_PER_TILE = CHUNK // DEC_SEQ
N_MAIN = 6 * GROUP_W
GATE_W = 128
FG_LANE = 64
FF_CHUNK = 256
N_FF_CHUNKS = D_FF // FF_CHUNK
FF_GROUP = 4
ALPHA = 2.0 ** 0.25
LN_EPS = 1e-5
TOKEN_TILE = 512
STAGE_C_SUBTILES = 2
PAD_ROWS = 8
MIX_UNITS_PER_TILE = 6 * (TOKEN_TILE // CHUNK) + 3
MIX_SHARE_IN_FFN = (4, 5)
VMEM_LIMIT_BYTES = 60 * 1024 * 1024

_NT = (((1,), (1,)), ((), ()))
_TN = (((0,), (0,)), ((), ()))


def _layer_norm(x, g, b):
    mu = jnp.mean(x, axis=-1, keepdims=True)
    xc = x - mu
    var = jnp.mean(xc * xc, axis=-1, keepdims=True)
    return xc * lax.rsqrt(var + LN_EPS) * g + b


def _head_rms(x, g):
    return x * lax.rsqrt(jnp.mean(x * x, axis=-1, keepdims=True) + LN_EPS) * g


def _silu(x):
    return x * jax.nn.sigmoid(x)


def _log_sigmoid(x):
    return jnp.minimum(x, 0.0) - jnp.log1p(jnp.exp(-jnp.abs(x)))


def _swiglu(xb_ref, wg_ref, wu_ref, wd_ref, h_ref, between_chunks=None):
    acc = None
    for c in range(N_FF_CHUNKS):
        cols = slice(c * FF_CHUNK, (c + 1) * FF_CHUNK)
        xb = xb_ref[...]
        g = jnp.dot(xb, wg_ref[:, cols], preferred_element_type=F32)
        u = jnp.dot(xb, wu_ref[:, cols], preferred_element_type=F32)
        h_ref[:, cols] = (_silu(g) * u).astype(BF16)
        if (c + 1) % FF_GROUP == 0 or c == N_FF_CHUNKS - 1:
            grp = slice((c // FF_GROUP) * FF_GROUP * FF_CHUNK, (c + 1) * FF_CHUNK)
            d = jnp.dot(h_ref[:, grp], wd_ref[grp, :], preferred_element_type=F32)
            acc = d if acc is None else acc + d
        if between_chunks is not None:
            between_chunks(c)
    return acc


def _ffn1_and_project(x_ref, wg_ref, wu_ref, wd_ref, lng_ref, lnb_ref, wmain_ref, bmain_ref, wgate_ref, bgate_ref,
                      vlng_ref, vlnb_ref, x1_ref, xb_ref, h_ref, store_z, store_gate, between_chunks=None):
    x = x_ref[...]
    xb_ref[...] = x.astype(BF16)
    ffn = _swiglu(xb_ref, wg_ref, wu_ref, wd_ref, h_ref, between_chunks)
    x1 = _layer_norm(ALPHA * x + 0.5 * ffn, lng_ref[...], lnb_ref[...])
    x1_ref[...] = x1
    xb_ref[...] = x1.astype(BF16)
    for grp in range(N_MAIN // GROUP_W):
        cols = slice(grp * GROUP_W, (grp + 1) * GROUP_W)
        z = jnp.dot(xb_ref[...], wmain_ref[:, cols], preferred_element_type=F32) + bmain_ref[:, cols]
        if grp == 0:
            store_z(cols, jax.nn.gelu(z))
        elif grp == 1:
            z = jax.nn.gelu(z)
            for h in range(N_HEADS):
                hc = slice(h * HEAD_DIM, (h + 1) * HEAD_DIM)
                store_z(slice(GROUP_W + h * HEAD_DIM, GROUP_W + (h + 1) * HEAD_DIM),
                        _layer_norm(z[:, hc], vlng_ref[:, hc], vlnb_ref[:, hc]))
        else:
            store_z(cols, z)
    store_gate(jnp.dot(xb_ref[...], wgate_ref[...], preferred_element_type=F32) + bgate_ref[...])


def _stage_a_kernel(*refs):
    *in_refs, x1_ref, zz_ref, gate_ref, xb_ref, h_ref = refs

    def store_z(cols, val):
        zz_ref[:, cols] = val

    def store_gate(val):
        gate_ref[...] = val

    _ffn1_and_project(*in_refs, x1_ref, xb_ref, h_ref, store_z, store_gate)


def _const_spec(shape):
    nd = len(shape)
    return pl.BlockSpec(shape, lambda i, _nd=nd: (0,) * _nd, pipeline_mode=pl.Buffered(1))


def _stage_a(x, wg, wu, wd, lng, lnb, wmain, bmain, wgate, bgate, vlng, vlnb):
    n = x.shape[0]
    tm = TOKEN_TILE
    row = lambda w: pl.BlockSpec((tm, w), lambda i: (i, 0))
    return pl.pallas_call(
        _stage_a_kernel,
        grid=(n // tm,),
        in_specs=[row(D_MODEL),
                  _const_spec(wg.shape), _const_spec(wu.shape), _const_spec(wd.shape),
                  _const_spec(lng.shape), _const_spec(lnb.shape),
                  _const_spec(wmain.shape), _const_spec(bmain.shape),
                  _const_spec(wgate.shape), _const_spec(bgate.shape),
                  _const_spec(vlng.shape), _const_spec(vlnb.shape)],
        out_specs=[row(D_MODEL), row(N_MAIN), row(GATE_W)],
        out_shape=[jax.ShapeDtypeStruct((n, D_MODEL), F32),
                   jax.ShapeDtypeStruct((n, N_MAIN), F32),
                   jax.ShapeDtypeStruct((n, GATE_W), F32)],
        scratch_shapes=[pltpu.VMEM((tm, D_MODEL), BF16), pltpu.VMEM((tm, D_FF), BF16)],
        compiler_params=pltpu.CompilerParams(dimension_semantics=("arbitrary",),
                                             vmem_limit_bytes=VMEM_LIMIT_BYTES),
        name="stage_a",
    )(x, wg, wu, wd, lng, lnb, wmain, bmain, wgate, bgate, vlng, vlnb)


def _stage_c_kernel(mix_ref, x1_ref, wout_ref, ln2g_ref, ln2b_ref, wg_ref, wu_ref, wd_ref, ln3g_ref, ln3b_ref,
                    y_ref, xb_ref, h_ref):
    tiles = [slice(s * TOKEN_TILE, (s + 1) * TOKEN_TILE) for s in range(STAGE_C_SUBTILES)]
    for rows in tiles:
        proj = jnp.dot(mix_ref[rows, :].astype(BF16), wout_ref[...], preferred_element_type=F32)
        y = _layer_norm(ALPHA * x1_ref[rows, :] + proj, ln2g_ref[...], ln2b_ref[...])
        y_ref[rows, :] = y
        xb_ref[rows, :] = y.astype(BF16)
    for rows in tiles:
        ffn = _swiglu(xb_ref.at[rows, :], wg_ref, wu_ref, wd_ref, h_ref.at[rows, :])
        y_ref[rows, :] = _layer_norm(ALPHA * y_ref[rows, :] + 0.5 * ffn, ln3g_ref[...], ln3b_ref[...])


def _stage_c(mix, x1, wout, ln2g, ln2b, wg, wu, wd, ln3g, ln3b):
    n = x1.shape[0]
    tm = TOKEN_TILE * STAGE_C_SUBTILES
    row = lambda w: pl.BlockSpec((tm, w), lambda i: (i, 0))
    return pl.pallas_call(
        _stage_c_kernel,
        grid=(n // tm,),
        in_specs=[row(D_MODEL), row(D_MODEL),
                  _const_spec(wout.shape), _const_spec(ln2g.shape), _const_spec(ln2b.shape),
                  _const_spec(wg.shape), _const_spec(wu.shape), _const_spec(wd.shape),
                  _const_spec(ln3g.shape), _const_spec(ln3b.shape)],
        out_specs=row(D_MODEL),
        out_shape=jax.ShapeDtypeStruct((n, D_MODEL), F32),
        scratch_shapes=[pltpu.VMEM((tm, D_MODEL), BF16), pltpu.VMEM((tm, D_FF), BF16)],
        compiler_params=pltpu.CompilerParams(dimension_semantics=("arbitrary",),
                                             vmem_limit_bytes=VMEM_LIMIT_BYTES),
        name="stage_c",
    )(mix, x1, wout, ln2g, ln2b, wg, wu, wd, ln3g, ln3b)


def _head_cols(h, base=0):
    return slice(base + h * HEAD_DIM, base + (h + 1) * HEAD_DIM)


def _gmlp_heads(zz_ref, ws_ref, bs_ref, gmg_ref, mix_ref, mask):
    for h in range(N_HEADS):
        u = zz_ref[:, _head_cols(h)]
        vn = zz_ref[:, _head_cols(h, GROUP_W)]
        w = jnp.where(mask, ws_ref[h], 0.0).astype(BF16)
        mixed = jnp.dot(w, vn.astype(BF16), preferred_element_type=F32) + bs_ref[:, h:h + 1]
        mix_ref[:, _head_cols(h)] = _head_rms(u * mixed, gmg_ref[:, _head_cols(h)])


def _intra_chunk(q, k, v, bcol, brow, irow, mcol, mask):
    dlog = jnp.where(mask, bcol - brow + irow, -jnp.inf)
    inter = bcol + mcol
    m_t = jnp.maximum(inter, jnp.max(dlog, axis=-1, keepdims=True))
    w_intra = jnp.exp(dlog - m_t)
    w_inter = jnp.exp(inter - m_t)
    s = lax.dot_general(q.astype(BF16), k.astype(BF16), _NT, preferred_element_type=F32) * w_intra
    sv = jnp.dot(s.astype(BF16), v.astype(BF16), preferred_element_type=F32)
    ssum = jnp.sum(s, axis=-1, keepdims=True)
    return m_t, w_inter, sv, ssum


def _conv_taps(x, prev_fn, cw_ref, cb_ref, row_in_seq):
    acc = x * cw_ref[CONV_W - 1:CONV_W, :] + cb_ref[...]
    for j in range(1, CONV_W):
        shifted = jnp.where(row_in_seq < j, prev_fn(j), pltpu.roll(x, j, 0))
        acc = acc + shifted * cw_ref[CONV_W - 1 - j:CONV_W - j, :]
    return acc


def _head_rms_mxu(x, g, ones):
    ss = jnp.dot((x * x).astype(BF16), ones, preferred_element_type=F32)
    return x * lax.rsqrt(ss * (1.0 / HEAD_DIM) + LN_EPS) * g


def _mix_prompt_tile(zt, gt, ws_ref, bs_ref, cw_ref, cb_ref, gmg_ref, mlg_ref,
                     mix_ref, cn_ref, m_ref, gmv_ref, ctail_ref, tail_ref):
    row = lax.broadcasted_iota(jnp.int32, (CHUNK, CHUNK), 0)
    col = lax.broadcasted_iota(jnp.int32, (CHUNK, CHUNK), 1)
    causal = col <= row
    causal_f = causal.astype(F32)
    qk_cols = slice(2 * GROUP_W, 4 * GROUP_W)
    scale = HEAD_DIM ** -0.5

    heads = range(N_HEADS)
    chunks = range(TOKEN_TILE // CHUNK)
    rows = [slice(PAD_ROWS + c * CHUNK, PAD_ROWS + (c + 1) * CHUNK) for c in chunks]
    out_rows = [slice(c * CHUNK, (c + 1) * CHUNK) for c in chunks]

    zt[0:PAD_ROWS, qk_cols] = tail_ref[...]
    ones = jnp.ones((HEAD_DIM, HEAD_DIM), BF16)
    ones_col = jnp.ones((CHUNK, HEAD_DIM), BF16)

    w_masked = [jnp.where(causal, ws_ref[h], 0.0).astype(BF16) for h in heads]
    bias = [jnp.broadcast_to(bs_ref[:, h:h + 1], (CHUNK, HEAD_DIM)) for h in heads]
    mixed = {}
    for c in chunks:
        for h in heads:
            mixed[c, h] = jnp.dot(w_masked[h], zt[rows[c], _head_cols(h, GROUP_W)].astype(BF16),
                                  preferred_element_type=F32)
        yield
    for c in chunks:
        for h in heads:
            gm = zt[rows[c], _head_cols(h)] * (mixed[c, h] + bias[h])
            mix_ref[out_rows[c], _head_cols(h)] = _head_rms_mxu(gm, gmg_ref[:, _head_cols(h)], ones)
        yield

    qk = []
    for c in chunks:
        r0 = rows[c].start
        acc = zt[rows[c], qk_cols] * cw_ref[CONV_W - 1:CONV_W, :] + cb_ref[...]
        for j in range(1, CONV_W):
            acc = acc + zt[r0 - j:r0 - j + CHUNK, qk_cols] * cw_ref[CONV_W - 1 - j:CONV_W - j, :]
        qk.append(_silu(acc))
        yield

    gates, bcum, gates_t, bcum_t = [], [], [], []
    for c in chunks:
        g = gt[out_rows[c], :]
        b = jnp.dot(causal_f, _log_sigmoid(g), precision=lax.Precision.HIGHEST, preferred_element_type=F32)
        gates.append(g)
        bcum.append(b)
        gates_t.append(g.T)
        bcum_t.append(b.T)
    yield
    icol = {(c, h): gates[c][:, h:h + 1] for c in chunks for h in heads}
    bcol = {(c, h): bcum[c][:, FG_LANE + h:FG_LANE + h + 1] for c in chunks for h in heads}

    dlog, mx = {}, {}
    for c in chunks:
        for h in heads:
            irow = gates_t[c][h:h + 1, :]
            brow = bcum_t[c][FG_LANE + h:FG_LANE + h + 1, :]
            dlog[c, h] = jnp.where(causal, bcol[c, h] - brow + irow, -jnp.inf)
            mx[c, h] = jnp.max(dlog[c, h], axis=-1, keepdims=True)
        yield

    m_row = m_ref[0]
    lane = lax.broadcasted_iota(jnp.int32, m_row.shape, 1)
    m_t, w_inter, dec, wk = {}, {}, {}, {}
    for h in heads:
        m_prev = m_row[:, h:h + 1]
        for c in chunks:
            inter = bcol[c, h] + m_prev
            m_t[c, h] = jnp.maximum(inter, mx[c, h])
            w_inter[c, h] = jnp.exp(inter - m_t[c, h])
            m_new = m_t[c, h][CHUNK - 1:CHUNK, :]
            b_last = bcol[c, h][CHUNK - 1:CHUNK, :]
            dec[c, h] = jnp.exp(b_last + m_prev - m_new)
            wk[c, h] = jnp.exp(b_last - bcol[c, h] + icol[c, h] - m_new)
            m_prev = m_new
        m_row = jnp.where(lane == h, m_prev, m_row)
    m_ref[0] = m_row
    yield

    sv, upd = {}, {}
    for c in chunks:
        for h in heads:
            q = qk[c][:, _head_cols(h)]
            k = qk[c][:, _head_cols(h, GROUP_W)] * scale
            v1 = jnp.concatenate([zt[rows[c], _head_cols(h, 4 * GROUP_W)].astype(BF16), ones_col], axis=1)
            s = lax.dot_general(q.astype(BF16), k.astype(BF16), _NT, preferred_element_type=F32) * jnp.exp(
                dlog[c, h] - m_t[c, h])
            sv[c, h] = jnp.dot(s.astype(BF16), v1, preferred_element_type=F32)
            kw = k * wk[c, h]
            upd[c, h] = lax.dot_general(kw.astype(BF16), v1, _TN, preferred_element_type=F32)
        yield

    for h in heads:
        cn_state = cn_ref[0, h]
        for c in chunks:
            q = qk[c][:, _head_cols(h)]
            qcn = jnp.dot(q.astype(BF16), cn_state.astype(BF16), preferred_element_type=F32)
            nd = sv[c, h] + jnp.broadcast_to(w_inter[c, h], (CHUNK, 2 * HEAD_DIM)) * qcn
            num, den = nd[:, :HEAD_DIM], nd[:, HEAD_DIM:]
            hid = num / jnp.maximum(jnp.abs(den), jnp.exp(-m_t[c, h]))
            o = zt[rows[c], _head_cols(h, 5 * GROUP_W)]
            mix_ref[out_rows[c], _head_cols(h, GROUP_W)] = _head_rms_mxu(jax.nn.sigmoid(o) * hid,
                                                                        mlg_ref[:, _head_cols(h)], ones)
            cn_state = dec[c, h] * cn_state + upd[c, h]
        cn_ref[0, h] = cn_state
        yield

    last8 = slice(PAD_ROWS + TOKEN_TILE - 8, PAD_ROWS + TOKEN_TILE)
    tail_ref[...] = zt[last8, qk_cols]
    ctail_ref[0] = zt[last8, qk_cols]
    gmv_ref[0] = zt[PAD_ROWS + TOKEN_TILE - CHUNK:PAD_ROWS + TOKEN_TILE, GROUP_W:2 * GROUP_W]


def _fused_prompt_kernel(tiles_per_seq, *refs):
    (x_ref, wg_ref, wu_ref, wd_ref, lng_ref, lnb_ref, wmain_ref, bmain_ref, wgate_ref, bgate_ref, vlng_ref, vlnb_ref,
     ws_ref, bs_ref, cw_ref, cb_ref, gmg_ref, mlg_ref,
     x1_ref, mix_ref, cn_ref, m_ref, gmv_ref, ctail_ref,
     xb_ref, h_ref, zbuf_ref, gbuf_ref, tail_ref) = refs
    i = pl.program_id(0)
    slot_w = i % 2
    slot_r = 1 - slot_w

    @pl.when(i == 0)
    def _():
        zbuf_ref[1] = jnp.zeros(zbuf_ref.shape[1:], F32)
        gbuf_ref[1] = jnp.zeros(gbuf_ref.shape[1:], F32)

    @pl.when((i == 0) | (i % tiles_per_seq == 1))
    def _():
        cn_ref[...] = jnp.zeros_like(cn_ref)
        m_ref[...] = jnp.zeros_like(m_ref)
        tail_ref[...] = jnp.zeros_like(tail_ref)

    def store_z(cols, val):
        zbuf_ref[slot_w, PAD_ROWS:PAD_ROWS + TOKEN_TILE, cols] = val

    def store_gate(val):
        gbuf_ref[slot_w] = val

    mixer = _mix_prompt_tile(zbuf_ref.at[slot_r], gbuf_ref.at[slot_r], ws_ref, bs_ref, cw_ref, cb_ref, gmg_ref,
                             mlg_ref, mix_ref, cn_ref, m_ref, gmv_ref, ctail_ref, tail_ref)

    traced = [0]

    def mixer_units(c):
        target = (MIX_SHARE_IN_FFN[0] * (c + 1) * MIX_UNITS_PER_TILE) // (MIX_SHARE_IN_FFN[1] * N_FF_CHUNKS)
        while traced[0] < target:
            next(mixer)
            traced[0] += 1

    _ffn1_and_project(x_ref, wg_ref, wu_ref, wd_ref, lng_ref, lnb_ref, wmain_ref, bmain_ref, wgate_ref, bgate_ref,
                      vlng_ref, vlnb_ref, x1_ref, xb_ref, h_ref, store_z, store_gate, mixer_units)
    for _ in mixer:
        traced[0] += 1
    assert traced[0] == MIX_UNITS_PER_TILE - 1


def _stage_a_mix_prompt(x, a_params, ws, bs_cols, cw, cb, gmg, mlg, batch, seq):
    tm = TOKEN_TILE
    n_tiles = (batch * seq) // tm
    tiles_per_seq = seq // tm
    mix_params = (ws, bs_cols, cw, cb, gmg, mlg)
    tile_in = lambda i: (jnp.minimum(i, n_tiles - 1), 0)
    tile_out = lambda i: (jnp.maximum(i - 1, 0), 0)
    seq_of = lambda i: jnp.maximum(i - 1, 0) // tiles_per_seq
    return pl.pallas_call(
        functools.partial(_fused_prompt_kernel, tiles_per_seq),
        grid=(n_tiles + 1,),
        in_specs=[pl.BlockSpec((tm, D_MODEL), tile_in)] + [_const_spec(p.shape) for p in (*a_params, *mix_params)],
        out_specs=[pl.BlockSpec((tm, D_MODEL), tile_in),
                   pl.BlockSpec((tm, D_MODEL), tile_out),
                   pl.BlockSpec((1, N_HEADS, HEAD_DIM, 2 * HEAD_DIM), lambda i: (seq_of(i), 0, 0, 0)),
                   pl.BlockSpec((1, 1, GATE_W), lambda i: (seq_of(i), 0, 0)),
                   pl.BlockSpec((1, CHUNK, GROUP_W), lambda i: (seq_of(i), 0, 0)),
                   pl.BlockSpec((1, 8, 2 * GROUP_W), lambda i: (seq_of(i), 0, 0))],
        out_shape=[jax.ShapeDtypeStruct((batch * seq, D_MODEL), F32),
                   jax.ShapeDtypeStruct((batch * seq, D_MODEL), F32),
                   jax.ShapeDtypeStruct((batch, N_HEADS, HEAD_DIM, 2 * HEAD_DIM), F32),
                   jax.ShapeDtypeStruct((batch, 1, GATE_W), F32),
                   jax.ShapeDtypeStruct((batch, CHUNK, GROUP_W), F32),
                   jax.ShapeDtypeStruct((batch, 8, 2 * GROUP_W), F32)],
        scratch_shapes=[pltpu.VMEM((tm, D_MODEL), BF16),
                        pltpu.VMEM((tm, D_FF), BF16),
                        pltpu.VMEM((2, PAD_ROWS + tm, N_MAIN), F32),
                        pltpu.VMEM((2, tm, GATE_W), F32),
                        pltpu.VMEM((8, 2 * GROUP_W), F32)],
        compiler_params=pltpu.CompilerParams(dimension_semantics=("arbitrary",),
                                             vmem_limit_bytes=VMEM_LIMIT_BYTES),
        name="stage_a_mix_prompt",
    )(x, *a_params, *mix_params)


def _mix_sample_kernel(zz_ref, gate_ref, ws_ref, bs_ref, cw_ref, cb_ref, gmg_ref, mlg_ref, prev_ref,
                       c0_ref, n0_ref, m0_ref, mix_ref, c_ref, n_ref, mt_ref):
    row = lax.broadcasted_iota(jnp.int32, (CHUNK, CHUNK), 0)
    col = lax.broadcasted_iota(jnp.int32, (CHUNK, CHUNK), 1)
    seq_shift = DEC_SEQ.bit_length() - 1
    mask = (col <= row) & ((col >> seq_shift) == (row >> seq_shift))
    last_sel = (col == (row | (DEC_SEQ - 1))).astype(F32)

    _gmlp_heads(zz_ref, ws_ref, bs_ref, gmg_ref, mix_ref, mask)

    x = zz_ref[:, 2 * GROUP_W:4 * GROUP_W]
    row_w = lax.broadcasted_iota(jnp.int32, (CHUNK, 2 * GROUP_W), 0) & (DEC_SEQ - 1)
    prev = prev_ref[...]
    qk = _silu(_conv_taps(x, lambda j: pltpu.roll(prev, CHUNK - DEC_SEQ + j, 0), cw_ref, cb_ref, row_w))

    gates = gate_ref[...]
    bcum = jnp.dot(mask.astype(F32), _log_sigmoid(gates), precision=lax.Precision.HIGHEST,
                   preferred_element_type=F32)
    gates_t = gates.T
    bcum_t = bcum.T
    lane = lax.broadcasted_iota(jnp.int32, (CHUNK, GATE_W), 1)
    mt_all = jnp.zeros((CHUNK, GATE_W), F32)
    scale = HEAD_DIM ** -0.5

    for h in range(N_HEADS):
        q = qk[:, _head_cols(h)]
        k = qk[:, _head_cols(h, GROUP_W)] * scale
        v = zz_ref[:, _head_cols(h, 4 * GROUP_W)]
        o = zz_ref[:, _head_cols(h, 5 * GROUP_W)]
        icol = gates[:, h:h + 1]
        bcol = bcum[:, FG_LANE + h:FG_LANE + h + 1]
        irow = gates_t[h:h + 1, :]
        brow = bcum_t[FG_LANE + h:FG_LANE + h + 1, :]
        m_prev = m0_ref[:, h:h + 1]

        m_t, w_inter, sv, ssum = _intra_chunk(q, k, v, bcol, brow, irow, m_prev, mask)
        qb = q.astype(BF16)
        qc_rows, qn_rows = [], []
        for s in range(SEQ_PER_TILE):
            rows = slice(s * DEC_SEQ, (s + 1) * DEC_SEQ)
            qc_rows.append(jnp.dot(qb[rows], c0_ref[s, h].astype(BF16), preferred_element_type=F32))
            qn_rows.append(jnp.sum(q[rows] * n0_ref[s, h:h + 1, :], axis=-1, keepdims=True))
        qc = jnp.concatenate(qc_rows, axis=0)
        qn = jnp.concatenate(qn_rows, axis=0)
        num = sv + w_inter * qc
        den = ssum + w_inter * qn
        hid = num / jnp.maximum(jnp.abs(den), jnp.exp(-m_t))
        mix_ref[:, _head_cols(h, GROUP_W)] = _head_rms(jax.nn.sigmoid(o) * hid, mlg_ref[:, _head_cols(h)])

        packed = jnp.where(lane == 0, m_t, jnp.where(lane == 1, bcol, 0.0))
        lastv = jnp.dot(last_sel, packed, precision=lax.Precision.HIGHEST, preferred_element_type=F32)
        m_new = lastv[:, 0:1]
        b_last = lastv[:, 1:2]
        dec = jnp.exp(b_last + m_prev - m_new)
        kw = k * jnp.exp(b_last - bcol + icol - m_new)
        kwb = kw.astype(BF16)
        vb = v.astype(BF16)
        for s in range(SEQ_PER_TILE):
            rows = slice(s * DEC_SEQ, (s + 1) * DEC_SEQ)
            dec_s = dec[s * DEC_SEQ:s * DEC_SEQ + 1, :]
            c_ref[s, h] = dec_s * c0_ref[s, h] + lax.dot_general(kwb[rows], vb[rows], _TN,
                                                                preferred_element_type=F32)
            n_ref[s, h:h + 1, :] = dec_s * n0_ref[s, h:h + 1, :] + jnp.sum(kw[rows], axis=0, keepdims=True)
        mt_all = jnp.where(lane == h, m_t, mt_all)

    mt_ref[...] = mt_all


def _mix_sample(zz, gates, ws_t, bs_cols, cw, cb, gmg, mlg, prev, c0, n0, m0_tok):
    n = zz.shape[0]
    n_tiles = n // CHUNK
    row = lambda w: pl.BlockSpec((CHUNK, w), lambda i: (i, 0))
    const = lambda shape: pl.BlockSpec(shape, lambda i, _nd=len(shape): (0,) * _nd)
    c_spec = pl.BlockSpec((SEQ_PER_TILE, N_HEADS, HEAD_DIM, HEAD_DIM), lambda i: (i, 0, 0, 0))
    n_spec = pl.BlockSpec((SEQ_PER_TILE, N_HEADS, HEAD_DIM), lambda i: (i, 0, 0))
    return pl.pallas_call(
        _mix_sample_kernel,
        grid=(n_tiles,),
        in_specs=[row(N_MAIN), row(GATE_W), const(ws_t.shape), const(bs_cols.shape), const(cw.shape),
                  const(cb.shape), const(gmg.shape), const(mlg.shape), row(2 * GROUP_W),
                  c_spec, n_spec, row(N_HEADS)],
        out_specs=[row(D_MODEL), c_spec, n_spec, row(GATE_W)],
        out_shape=[jax.ShapeDtypeStruct((n, D_MODEL), F32),
                   jax.ShapeDtypeStruct(c0.shape, F32),
                   jax.ShapeDtypeStruct(n0.shape, F32),
                   jax.ShapeDtypeStruct((n, GATE_W), F32)],
        compiler_params=pltpu.CompilerParams(dimension_semantics=("arbitrary",),
                                             vmem_limit_bytes=VMEM_LIMIT_BYTES),
        name="mix_sample",
    )(zz, gates, ws_t, bs_cols, cw, cb, gmg, mlg, prev, c0, n0, m0_tok)


def _ffn_weights(wg, wu, wd):
    return wg.astype(BF16), wu.astype(BF16), wd.astype(BF16)


def _gate_columns(w):
    out = jnp.zeros(w.shape[:-1] + (GATE_W,), w.dtype)
    out = out.at[..., 0:N_HEADS].set(w[..., N_MAIN:N_MAIN + N_HEADS])
    return out.at[..., FG_LANE:FG_LANE + N_HEADS].set(w[..., N_MAIN + N_HEADS:N_MAIN + 2 * N_HEADS])


def kernel(x_prompt, x_sample, state_conv, state_C, state_n, state_m, ffn1_wg, ffn1_wu, ffn1_wd, ln1_g, ln1_b, w_in, b_in, gm_ln_g, gm_ln_b, gm_ws, gm_bs, conv_w, conv_b, gm_out_g, ml_out_g, w_out, ln2_g, ln2_b, ffn2_wg, ffn2_wu, ffn2_wd, ln3_g, ln3_b):
    depth = ffn1_wg.shape[0]
    bp, seq, _ = x_prompt.shape
    bs, dec_seq, _ = x_sample.shape
    assert dec_seq == DEC_SEQ and seq % TOKEN_TILE == 0
    assert (bs * dec_seq) % (TOKEN_TILE * STAGE_C_SUBTILES) == 0 and (bp * seq) % (TOKEN_TILE * STAGE_C_SUBTILES) == 0
    y_p = x_prompt.reshape(bp * seq, D_MODEL)
    y_s = x_sample.reshape(bs * dec_seq, D_MODEL)
    outs = []
    for l in range(depth):
        f1 = _ffn_weights(ffn1_wg[l], ffn1_wu[l], ffn1_wd[l])
        f2 = _ffn_weights(ffn2_wg[l], ffn2_wu[l], ffn2_wd[l])
        row = lambda a: a.reshape(1, -1)
        a_params = (*f1, row(ln1_g[l]), row(ln1_b[l]),
                    w_in[l].astype(BF16), row(b_in[l]),
                    _gate_columns(w_in[l]).astype(BF16), row(_gate_columns(b_in[l])),
                    row(gm_ln_g[l]), row(gm_ln_b[l]))
        c_params = (w_out[l].astype(BF16), row(ln2_g[l]), row(ln2_b[l]), *f2, row(ln3_g[l]), row(ln3_b[l]))
        mix_params = (conv_w[l], row(conv_b[l]), row(gm_out_g[l]), row(ml_out_g[l]))

        x1_p, mix_p, cn_p, m_p, gmv_p, ctail_p = _stage_a_mix_prompt(
            y_p, a_params, gm_ws[l], jnp.transpose(gm_bs[l]), *mix_params, bp, seq)
        x1_s, zz_s, g_s = _stage_a(y_s, *a_params)

        ws_t = jnp.tile(gm_ws[l][:, :DEC_SEQ, :DEC_SEQ], (1, SEQ_PER_TILE, SEQ_PER_TILE))
        bs_t = jnp.tile(jnp.transpose(gm_bs[l][:, :DEC_SEQ]), (SEQ_PER_TILE, 1))
        prev = jnp.pad(state_conv[l], ((0, 0), (DEC_SEQ - (CONV_W - 1), 0), (0, 0))).reshape(bs * DEC_SEQ, 2 * GROUP_W)
        m0_tok = jnp.repeat(state_m[l], DEC_SEQ, axis=0)
        mix_s, c_s, n_s, mt_s = _mix_sample(zz_s, g_s, ws_t, bs_t, *mix_params, prev, state_C[l], state_n[l], m0_tok)

        y_p = _stage_c(mix_p, x1_p, *c_params)
        y_s = _stage_c(mix_s, x1_s, *c_params)

        zz_s3 = zz_s.reshape(bs, dec_seq, N_MAIN)
        outs.append((
            gmv_p.reshape(bp, CHUNK, N_HEADS, HEAD_DIM),
            zz_s3[:, :, GROUP_W:2 * GROUP_W].reshape(bs, dec_seq, N_HEADS, HEAD_DIM),
            ctail_p[:, 8 - (CONV_W - 1):, :],
            zz_s3[:, dec_seq - (CONV_W - 1):, 2 * GROUP_W:4 * GROUP_W],
            cn_p[..., :HEAD_DIM], c_s, cn_p[..., HEAD_DIM], n_s,
            m_p[:, 0, :N_HEADS],
            mt_s.reshape(bs, dec_seq, GATE_W)[:, dec_seq - 1, :N_HEADS],
        ))
    stacked = [jnp.stack(a) for a in zip(*outs)]
    return (y_p.reshape(bp, seq, D_MODEL), y_s.reshape(bs, dec_seq, D_MODEL), *stacked)
```

```python
import functools

import jax
import jax.numpy as jnp
from jax import lax
from jax.experimental import pallas as pl
from jax.experimental.pallas import tpu as pltpu

F32 = jnp.float32
BF16 = jnp.bfloat16

D_MODEL = 1024
D_FF = 2816
HEAD_DIM = 128
N_HEADS = 4
GROUP_W = N_HEADS * HEAD_DIM
CHUNK = 128
CONV_W = 4
DEC_SEQ = 8
SEQ_PER_TILE = CHUNK // DEC_SEQ
N_MAIN = 6 * GROUP_W
GATE_W = 128
FG_LANE = 64
FF_CHUNK = 256
N_FF_CHUNKS = D_FF // FF_CHUNK
FF_GROUP = 4
ALPHA = 2.0 ** 0.25
LN_EPS = 1e-5
TOKEN_TILE = 512
STAGE_C_SUBTILES = 2
PAD_ROWS = 8
MIX_UNITS_PER_TILE = 6 * (TOKEN_TILE // CHUNK) + 3
MIX_SHARE_IN_FFN = (4, 5)
VMEM_LIMIT_BYTES = 62 * 1024 * 1024

_NT = (((1,), (1,)), ((), ()))
_TN = (((0,), (0,)), ((), ()))


def _layer_norm(x, g, b):
    mu = jnp.mean(x, axis=-1, keepdims=True)
    xc = x - mu
    var = jnp.mean(xc * xc, axis=-1, keepdims=True)
    return xc * lax.rsqrt(var + LN_EPS) * g + b


def _head_rms(x, g):
    return x * lax.rsqrt(jnp.mean(x * x, axis=-1, keepdims=True) + LN_EPS) * g


def _silu(x):
    return x * jax.nn.sigmoid(x)


def _log_sigmoid(x):
    return jnp.minimum(x, 0.0) - jnp.log1p(jnp.exp(-jnp.abs(x)))


def _swiglu(xb_ref, wg_ref, wu_ref, wd_ref, h_ref, between_chunks=None):
    acc = None
    for c in range(N_FF_CHUNKS):
        cols = slice(c * FF_CHUNK, (c + 1) * FF_CHUNK)
        xb = xb_ref[...]
        g = jnp.dot(xb, wg_ref[:, cols], preferred_element_type=F32)
        u = jnp.dot(xb, wu_ref[:, cols], preferred_element_type=F32)
        h_ref[:, cols] = (_silu(g) * u).astype(BF16)
        if (c + 1) % FF_GROUP == 0 or c == N_FF_CHUNKS - 1:
            grp = slice((c // FF_GROUP) * FF_GROUP * FF_CHUNK, (c + 1) * FF_CHUNK)
            d = jnp.dot(h_ref[:, grp], wd_ref[grp, :], preferred_element_type=F32)
            acc = d if acc is None else acc + d
        if between_chunks is not None:
            between_chunks(c)
    return acc


def _ffn1_and_project(x_ref, wg_ref, wu_ref, wd_ref, lng_ref, lnb_ref, wmain_ref, bmain_ref, wgate_ref, bgate_ref,
                      vlng_ref, vlnb_ref, x1_ref, xb_ref, h_ref, store_z, store_gate, between_chunks=None):
    x = x_ref[...]
    xb_ref[...] = x.astype(BF16)
    ffn = _swiglu(xb_ref, wg_ref, wu_ref, wd_ref, h_ref, between_chunks)
    x1 = _layer_norm(ALPHA * x + 0.5 * ffn, lng_ref[...], lnb_ref[...])
    x1_ref[...] = x1
    xb_ref[...] = x1.astype(BF16)
    for grp in range(N_MAIN // GROUP_W):
        cols = slice(grp * GROUP_W, (grp + 1) * GROUP_W)
        z = jnp.dot(xb_ref[...], wmain_ref[:, cols], preferred_element_type=F32) + bmain_ref[:, cols]
        if grp == 0:
            store_z(cols, jax.nn.gelu(z))
        elif grp == 1:
            z = jax.nn.gelu(z)
            for h in range(N_HEADS):
                hc = slice(h * HEAD_DIM, (h + 1) * HEAD_DIM)
                store_z(slice(GROUP_W + h * HEAD_DIM, GROUP_W + (h + 1) * HEAD_DIM),
                        _layer_norm(z[:, hc], vlng_ref[:, hc], vlnb_ref[:, hc]))
        else:
            store_z(cols, z)
    store_gate(jnp.dot(xb_ref[...], wgate_ref[...], preferred_element_type=F32) + bgate_ref[...])


def _stage_a_kernel(*refs):
    *in_refs, x1_ref, zz_ref, gate_ref, xb_ref, h_ref = refs

    def store_z(cols, val):
        zz_ref[:, cols] = val

    def store_gate(val):
        gate_ref[...] = val

    _ffn1_and_project(*in_refs, x1_ref, xb_ref, h_ref, store_z, store_gate)


def _const_spec(shape):
    nd = len(shape)
    return pl.BlockSpec(shape, lambda i, _nd=nd: (0,) * _nd, pipeline_mode=pl.Buffered(1))


def _stage_a(x, wg, wu, wd, lng, lnb, wmain, bmain, wgate, bgate, vlng, vlnb):
    n = x.shape[0]
    tm = TOKEN_TILE
    row = lambda w: pl.BlockSpec((tm, w), lambda i: (i, 0))
    return pl.pallas_call(
        _stage_a_kernel,
        grid=(n // tm,),
        in_specs=[row(D_MODEL),
                  _const_spec(wg.shape), _const_spec(wu.shape), _const_spec(wd.shape),
                  _const_spec(lng.shape), _const_spec(lnb.shape),
                  _const_spec(wmain.shape), _const_spec(bmain.shape),
                  _const_spec(wgate.shape), _const_spec(bgate.shape),
                  _const_spec(vlng.shape), _const_spec(vlnb.shape)],
        out_specs=[row(D_MODEL), row(N_MAIN), row(GATE_W)],
        out_shape=[jax.ShapeDtypeStruct((n, D_MODEL), F32),
                   jax.ShapeDtypeStruct((n, N_MAIN), F32),
                   jax.ShapeDtypeStruct((n, GATE_W), F32)],
        scratch_shapes=[pltpu.VMEM((tm, D_MODEL), BF16), pltpu.VMEM((tm, D_FF), BF16)],
        compiler_params=pltpu.CompilerParams(dimension_semantics=("arbitrary",),
                                             vmem_limit_bytes=VMEM_LIMIT_BYTES),
        name="stage_a",
    )(x, wg, wu, wd, lng, lnb, wmain, bmain, wgate, bgate, vlng, vlnb)


def _stage_c_kernel(mix_ref, x1_ref, wout_ref, ln2g_ref, ln2b_ref, wg_ref, wu_ref, wd_ref, ln3g_ref, ln3b_ref,
                    y_ref, xb_ref, h_ref):
    tiles = [slice(s * TOKEN_TILE, (s + 1) * TOKEN_TILE) for s in range(STAGE_C_SUBTILES)]
    for rows in tiles:
        proj = jnp.dot(mix_ref[rows, :].astype(BF16), wout_ref[...], preferred_element_type=F32)
        y = _layer_norm(ALPHA * x1_ref[rows, :] + proj, ln2g_ref[...], ln2b_ref[...])
        y_ref[rows, :] = y
        xb_ref[rows, :] = y.astype(BF16)
    for rows in tiles:
        ffn = _swiglu(xb_ref.at[rows, :], wg_ref, wu_ref, wd_ref, h_ref.at[rows, :])
        y_ref[rows, :] = _layer_norm(ALPHA * y_ref[rows, :] + 0.5 * ffn, ln3g_ref[...], ln3b_ref[...])


def _stage_c(mix, x1, wout, ln2g, ln2b, wg, wu, wd, ln3g, ln3b):
    n = x1.shape[0]
    tm = TOKEN_TILE * STAGE_C_SUBTILES
    row = lambda w: pl.BlockSpec((tm, w), lambda i: (i, 0))
    return pl.pallas_call(
        _stage_c_kernel,
        grid=(n // tm,),
        in_specs=[row(D_MODEL), row(D_MODEL),
                  _const_spec(wout.shape), _const_spec(ln2g.shape), _const_spec(ln2b.shape),
                  _const_spec(wg.shape), _const_spec(wu.shape), _const_spec(wd.shape),
                  _const_spec(ln3g.shape), _const_spec(ln3b.shape)],
        out_specs=row(D_MODEL),
        out_shape=jax.ShapeDtypeStruct((n, D_MODEL), F32),
        scratch_shapes=[pltpu.VMEM((tm, D_MODEL), BF16), pltpu.VMEM((tm, D_FF), BF16)],
        compiler_params=pltpu.CompilerParams(dimension_semantics=("arbitrary",),
                                             vmem_limit_bytes=VMEM_LIMIT_BYTES),
        name="stage_c",
    )(mix, x1, wout, ln2g, ln2b, wg, wu, wd, ln3g, ln3b)


def _head_cols(h, base=0):
    return slice(base + h * HEAD_DIM, base + (h + 1) * HEAD_DIM)


def _gmlp_heads(zz_ref, ws_ref, bs_ref, gmg_ref, mix_ref, mask):
    for h in range(N_HEADS):
        u = zz_ref[:, _head_cols(h)]
        vn = zz_ref[:, _head_cols(h, GROUP_W)]
        w = jnp.where(mask, ws_ref[h], 0.0).astype(BF16)
        mixed = jnp.dot(w, vn.astype(BF16), preferred_element_type=F32) + bs_ref[:, h:h + 1]
        mix_ref[:, _head_cols(h)] = _head_rms(u * mixed, gmg_ref[:, _head_cols(h)])


def _intra_chunk(q, k, v, bcol, brow, irow, mcol, mask):
    dlog = jnp.where(mask, bcol - brow + irow, -jnp.inf)
    inter = bcol + mcol
    m_t = jnp.maximum(inter, jnp.max(dlog, axis=-1, keepdims=True))
    w_intra = jnp.exp(dlog - m_t)
    w_inter = jnp.exp(inter - m_t)
    s = lax.dot_general(q.astype(BF16), k.astype(BF16), _NT, preferred_element_type=F32) * w_intra
    sv = jnp.dot(s.astype(BF16), v.astype(BF16), preferred_element_type=F32)
    ssum = jnp.sum(s, axis=-1, keepdims=True)
    return m_t, w_inter, sv, ssum


def _conv_taps(x, prev_fn, cw_ref, cb_ref, row_in_seq):
    acc = x * cw_ref[CONV_W - 1:CONV_W, :] + cb_ref[...]
    for j in range(1, CONV_W):
        shifted = jnp.where(row_in_seq < j, prev_fn(j), pltpu.roll(x, j, 0))
        acc = acc + shifted * cw_ref[CONV_W - 1 - j:CONV_W - j, :]
    return acc


def _head_rms_mxu(x, g, ones):
    ss = jnp.dot((x * x).astype(BF16), ones, preferred_element_type=F32)
    return x * lax.rsqrt(ss * (1.0 / HEAD_DIM) + LN_EPS) * g


def _mix_prompt_tile(zt, gt, ws_ref, bs_ref, cw_ref, cb_ref, gmg_ref, mlg_ref,
                     mix_ref, cn_ref, m_ref, gmv_ref, ctail_ref, tail_ref):
    row = lax.broadcasted_iota(jnp.int32, (CHUNK, CHUNK), 0)
    col = lax.broadcasted_iota(jnp.int32, (CHUNK, CHUNK), 1)
    causal = col <= row
    causal_f = causal.astype(F32)
    qk_cols = slice(2 * GROUP_W, 4 * GROUP_W)
    scale = HEAD_DIM ** -0.5

    heads = range(N_HEADS)
    chunks = range(TOKEN_TILE // CHUNK)
    rows = [slice(PAD_ROWS + c * CHUNK, PAD_ROWS + (c + 1) * CHUNK) for c in chunks]
    out_rows = [slice(c * CHUNK, (c + 1) * CHUNK) for c in chunks]

    zt[0:PAD_ROWS, qk_cols] = tail_ref[...]
    ones = jnp.ones((HEAD_DIM, HEAD_DIM), BF16)
    ones_col = jnp.ones((CHUNK, HEAD_DIM), BF16)

    w_masked = [jnp.where(causal, ws_ref[h], 0.0).astype(BF16) for h in heads]
    bias = [jnp.broadcast_to(bs_ref[:, h:h + 1], (CHUNK, HEAD_DIM)) for h in heads]
    mixed = {}
    for c in chunks:
        for h in heads:
            mixed[c, h] = jnp.dot(w_masked[h], zt[rows[c], _head_cols(h, GROUP_W)].astype(BF16),
                                  preferred_element_type=F32)
        yield
    for c in chunks:
        for h in heads:
            gm = zt[rows[c], _head_cols(h)] * (mixed[c, h] + bias[h])
            mix_ref[out_rows[c], _head_cols(h)] = _head_rms_mxu(gm, gmg_ref[:, _head_cols(h)], ones)
        yield

    qk = []
    for c in chunks:
        r0 = rows[c].start
        acc = zt[rows[c], qk_cols] * cw_ref[CONV_W - 1:CONV_W, :] + cb_ref[...]
        for j in range(1, CONV_W):
            acc = acc + zt[r0 - j:r0 - j + CHUNK, qk_cols] * cw_ref[CONV_W - 1 - j:CONV_W - j, :]
        qk.append(_silu(acc))
        yield

    gates, bcum, gates_t, bcum_t = [], [], [], []
    for c in chunks:
        g = gt[out_rows[c], :]
        b = jnp.dot(causal_f, _log_sigmoid(g), precision=lax.Precision.HIGHEST, preferred_element_type=F32)
        gates.append(g)
        bcum.append(b)
        gates_t.append(g.T)
        bcum_t.append(b.T)
    yield
    icol = {(c, h): gates[c][:, h:h + 1] for c in chunks for h in heads}
    bcol = {(c, h): bcum[c][:, FG_LANE + h:FG_LANE + h + 1] for c in chunks for h in heads}

    dlog, mx = {}, {}
    for c in chunks:
        for h in heads:
            irow = gates_t[c][h:h + 1, :]
            brow = bcum_t[c][FG_LANE + h:FG_LANE + h + 1, :]
            dlog[c, h] = jnp.where(causal, bcol[c, h] - brow + irow, -jnp.inf)
            mx[c, h] = jnp.max(dlog[c, h], axis=-1, keepdims=True)
        yield

    m_row = m_ref[0]
    lane = lax.broadcasted_iota(jnp.int32, m_row.shape, 1)
    m_t, w_inter, dec, wk = {}, {}, {}, {}
    for h in heads:
        m_prev = m_row[:, h:h + 1]
        for c in chunks:
            inter = bcol[c, h] + m_prev
            m_t[c, h] = jnp.maximum(inter, mx[c, h])
            w_inter[c, h] = jnp.exp(inter - m_t[c, h])
            m_new = m_t[c, h][CHUNK - 1:CHUNK, :]
            b_last = bcol[c, h][CHUNK - 1:CHUNK, :]
            dec[c, h] = jnp.exp(b_last + m_prev - m_new)
            wk[c, h] = jnp.exp(b_last - bcol[c, h] + icol[c, h] - m_new)
            m_prev = m_new
        m_row = jnp.where(lane == h, m_prev, m_row)
    m_ref[0] = m_row
    yield

    sv, upd = {}, {}
    for c in chunks:
        for h in heads:
            q = qk[c][:, _head_cols(h)]
            k = qk[c][:, _head_cols(h, GROUP_W)] * scale
            v1 = jnp.concatenate([zt[rows[c], _head_cols(h, 4 * GROUP_W)].astype(BF16), ones_col], axis=1)
            s = lax.dot_general(q.astype(BF16), k.astype(BF16), _NT, preferred_element_type=F32) * jnp.exp(
                dlog[c, h] - m_t[c, h])
            sv[c, h] = jnp.dot(s.astype(BF16), v1, preferred_element_type=F32)
            kw = k * wk[c, h]
            upd[c, h] = lax.dot_general(kw.astype(BF16), v1, _TN, preferred_element_type=F32)
        yield

    for h in heads:
        cn_state = cn_ref[0, h]
        for c in chunks:
            q = qk[c][:, _head_cols(h)]
            qcn = jnp.dot(q.astype(BF16), cn_state.astype(BF16), preferred_element_type=F32)
            nd = sv[c, h] + jnp.broadcast_to(w_inter[c, h], (CHUNK, 2 * HEAD_DIM)) * qcn
            num, den = nd[:, :HEAD_DIM], nd[:, HEAD_DIM:]
            hid = num / jnp.maximum(jnp.abs(den), jnp.exp(-m_t[c, h]))
            o = zt[rows[c], _head_cols(h, 5 * GROUP_W)]
            mix_ref[out_rows[c], _head_cols(h, GROUP_W)] = _head_rms_mxu(jax.nn.sigmoid(o) * hid,
                                                                        mlg_ref[:, _head_cols(h)], ones)
            cn_state = dec[c, h] * cn_state + upd[c, h]
        cn_ref[0, h] = cn_state
        yield

    last8 = slice(PAD_ROWS + TOKEN_TILE - 8, PAD_ROWS + TOKEN_TILE)
    tail_ref[...] = zt[last8, qk_cols]
    ctail_ref[0] = zt[last8, qk_cols]
    gmv_ref[0] = zt[PAD_ROWS + TOKEN_TILE - CHUNK:PAD_ROWS + TOKEN_TILE, GROUP_W:2 * GROUP_W]


def _fused_prompt_kernel(tiles_per_seq, n_tiles, *refs):
    (x_ref, wg_ref, wu_ref, wd_ref, lng_ref, lnb_ref, wmain_ref, bmain_ref, wgate_ref, bgate_ref, vlng_ref, vlnb_ref,
     ws_ref, bs_ref, cw_ref, cb_ref, gmg_ref, mlg_ref,
     x1_ref, mix_ref, cn_ref, m_ref, gmv_ref, ctail_ref,
     xb_ref, h_ref, zbuf_ref, gbuf_ref, tail_ref) = refs
    i = pl.program_id(0)
    slot_w = i % 2
    slot_r = 1 - slot_w

    @pl.when(i % tiles_per_seq == 1)
    def _():
        cn_ref[...] = jnp.zeros_like(cn_ref)
        m_ref[...] = jnp.zeros_like(m_ref)
        tail_ref[...] = jnp.zeros_like(tail_ref)

    def store_z(cols, val):
        zbuf_ref[slot_w, PAD_ROWS:PAD_ROWS + TOKEN_TILE, cols] = val

    def store_gate(val):
        gbuf_ref[slot_w] = val

    def stage_a(between_chunks=None):
        _ffn1_and_project(x_ref, wg_ref, wu_ref, wd_ref, lng_ref, lnb_ref, wmain_ref, bmain_ref, wgate_ref,
                          bgate_ref, vlng_ref, vlnb_ref, x1_ref, xb_ref, h_ref, store_z, store_gate, between_chunks)

    def mixer_pieces():
        return _mix_prompt_tile(zbuf_ref.at[slot_r], gbuf_ref.at[slot_r], ws_ref, bs_ref, cw_ref, cb_ref, gmg_ref,
                                mlg_ref, mix_ref, cn_ref, m_ref, gmv_ref, ctail_ref, tail_ref)

    @pl.when(i == 0)
    def _():
        stage_a()

    @pl.when(i == n_tiles)
    def _():
        for _ in mixer_pieces():
            pass

    @pl.when((i > 0) & (i < n_tiles))
    def _():
        mixer = mixer_pieces()
        traced = [0]

        def mixer_units(c):
            target = (MIX_SHARE_IN_FFN[0] * (c + 1) * MIX_UNITS_PER_TILE) // (MIX_SHARE_IN_FFN[1] * N_FF_CHUNKS)
            while traced[0] < target:
                next(mixer)
                traced[0] += 1

        stage_a(mixer_units)
        for _ in mixer:
            traced[0] += 1
        assert traced[0] == MIX_UNITS_PER_TILE - 1


def _stage_a_mix_prompt(x, a_params, ws, bs_cols, cw, cb, gmg, mlg, batch, seq):
    tm = TOKEN_TILE
    n_tiles = (batch * seq) // tm
    tiles_per_seq = seq // tm
    mix_params = (ws, bs_cols, cw, cb, gmg, mlg)
    tile_in = lambda i: (jnp.minimum(i, n_tiles - 1), 0)
    tile_out = lambda i: (jnp.maximum(i - 1, 0), 0)
    seq_of = lambda i: jnp.maximum(i - 1, 0) // tiles_per_seq
    return pl.pallas_call(
        functools.partial(_fused_prompt_kernel, tiles_per_seq, n_tiles),
        grid=(n_tiles + 1,),
        in_specs=[pl.BlockSpec((tm, D_MODEL), tile_in)] + [_const_spec(p.shape) for p in (*a_params, *mix_params)],
        out_specs=[pl.BlockSpec((tm, D_MODEL), tile_in),
                   pl.BlockSpec((tm, D_MODEL), tile_out),
                   pl.BlockSpec((1, N_HEADS, HEAD_DIM, 2 * HEAD_DIM), lambda i: (seq_of(i), 0, 0, 0)),
                   pl.BlockSpec((1, 1, GATE_W), lambda i: (seq_of(i), 0, 0)),
                   pl.BlockSpec((1, CHUNK, GROUP_W), lambda i: (seq_of(i), 0, 0)),
                   pl.BlockSpec((1, 8, 2 * GROUP_W), lambda i: (seq_of(i), 0, 0))],
        out_shape=[jax.ShapeDtypeStruct((batch * seq, D_MODEL), F32),
                   jax.ShapeDtypeStruct((batch * seq, D_MODEL), F32),
                   jax.ShapeDtypeStruct((batch, N_HEADS, HEAD_DIM, 2 * HEAD_DIM), F32),
                   jax.ShapeDtypeStruct((batch, 1, GATE_W), F32),
                   jax.ShapeDtypeStruct((batch, CHUNK, GROUP_W), F32),
                   jax.ShapeDtypeStruct((batch, 8, 2 * GROUP_W), F32)],
        scratch_shapes=[pltpu.VMEM((tm, D_MODEL), BF16),
                        pltpu.VMEM((tm, D_FF), BF16),
                        pltpu.VMEM((2, PAD_ROWS + tm, N_MAIN), F32),
                        pltpu.VMEM((2, tm, GATE_W), F32),
                        pltpu.VMEM((8, 2 * GROUP_W), F32)],
        compiler_params=pltpu.CompilerParams(dimension_semantics=("arbitrary",),
                                             vmem_limit_bytes=VMEM_LIMIT_BYTES),
        name="stage_a_mix_prompt",
    )(x, *a_params, *mix_params)


def _mix_sample_kernel(zz_ref, gate_ref, ws_ref, bs_ref, cw_ref, cb_ref, gmg_ref, mlg_ref, prev_ref,
                       c0_ref, n0_ref, m0_ref, mix_ref, c_ref, n_ref, mt_ref):
    row = lax.broadcasted_iota(jnp.int32, (CHUNK, CHUNK), 0)
    col = lax.broadcasted_iota(jnp.int32, (CHUNK, CHUNK), 1)
    seq_shift = DEC_SEQ.bit_length() - 1
    mask = (col <= row) & ((col >> seq_shift) == (row >> seq_shift))
    last_sel = (col == (row | (DEC_SEQ - 1))).astype(F32)

    _gmlp_heads(zz_ref, ws_ref, bs_ref, gmg_ref, mix_ref, mask)

    x = zz_ref[:, 2 * GROUP_W:4 * GROUP_W]
    row_w = lax.broadcasted_iota(jnp.int32, (CHUNK, 2 * GROUP_W), 0) & (DEC_SEQ - 1)
    prev = prev_ref[...]
    qk = _silu(_conv_taps(x, lambda j: pltpu.roll(prev, CHUNK - DEC_SEQ + j, 0), cw_ref, cb_ref, row_w))

    gates = gate_ref[...]
    bcum = jnp.dot(mask.astype(F32), _log_sigmoid(gates), precision=lax.Precision.HIGHEST,
                   preferred_element_type=F32)
    gates_t = gates.T
    bcum_t = bcum.T
    lane = lax.broadcasted_iota(jnp.int32, (CHUNK, GATE_W), 1)
    mt_all = jnp.zeros((CHUNK, GATE_W), F32)
    scale = HEAD_DIM ** -0.5

    for h in range(N_HEADS):
        q = qk[:, _head_cols(h)]
        k = qk[:, _head_cols(h, GROUP_W)] * scale
        v = zz_ref[:, _head_cols(h, 4 * GROUP_W)]
        o = zz_ref[:, _head_cols(h, 5 * GROUP_W)]
        icol = gates[:, h:h + 1]
        bcol = bcum[:, FG_LANE + h:FG_LANE + h + 1]
        irow = gates_t[h:h + 1, :]
        brow = bcum_t[FG_LANE + h:FG_LANE + h + 1, :]
        m_prev = m0_ref[:, h:h + 1]

        m_t, w_inter, sv, ssum = _intra_chunk(q, k, v, bcol, brow, irow, m_prev, mask)
        qb = q.astype(BF16)
        qc_rows, qn_rows = [], []
        for s in range(SEQ_PER_TILE):
            rows = slice(s * DEC_SEQ, (s + 1) * DEC_SEQ)
            qc_rows.append(jnp.dot(qb[rows], c0_ref[s, h].astype(BF16), preferred_element_type=F32))
            qn_rows.append(jnp.sum(q[rows] * n0_ref[s, h:h + 1, :], axis=-1, keepdims=True))
        qc = jnp.concatenate(qc_rows, axis=0)
        qn = jnp.concatenate(qn_rows, axis=0)
        num = sv + w_inter * qc
        den = ssum + w_inter * qn
        hid = num / jnp.maximum(jnp.abs(den), jnp.exp(-m_t))
        mix_ref[:, _head_cols(h, GROUP_W)] = _head_rms(jax.nn.sigmoid(o) * hid, mlg_ref[:, _head_cols(h)])

        packed = jnp.where(lane == 0, m_t, jnp.where(lane == 1, bcol, 0.0))
        lastv = jnp.dot(last_sel, packed, precision=lax.Precision.HIGHEST, preferred_element_type=F32)
        m_new = lastv[:, 0:1]
        b_last = lastv[:, 1:2]
        dec = jnp.exp(b_last + m_prev - m_new)
        kw = k * jnp.exp(b_last - bcol + icol - m_new)
        kwb = kw.astype(BF16)
        vb = v.astype(BF16)
        for s in range(SEQ_PER_TILE):
            rows = slice(s * DEC_SEQ, (s + 1) * DEC_SEQ)
            dec_s = dec[s * DEC_SEQ:s * DEC_SEQ + 1, :]
            c_ref[s, h] = dec_s * c0_ref[s, h] + lax.dot_general(kwb[rows], vb[rows], _TN,
                                                                preferred_element_type=F32)
            n_ref[s, h:h + 1, :] = dec_s * n0_ref[s, h:h + 1, :] + jnp.sum(kw[rows], axis=0, keepdims=True)
        mt_all = jnp.where(lane == h, m_t, mt_all)

    mt_ref[...] = mt_all


def _mix_sample(zz, gates, ws_t, bs_cols, cw, cb, gmg, mlg, prev, c0, n0, m0_tok):
    n = zz.shape[0]
    n_tiles = n // CHUNK
    row = lambda w: pl.BlockSpec((CHUNK, w), lambda i: (i, 0))
    const = lambda shape: pl.BlockSpec(shape, lambda i, _nd=len(shape): (0,) * _nd)
    c_spec = pl.BlockSpec((SEQ_PER_TILE, N_HEADS, HEAD_DIM, HEAD_DIM), lambda i: (i, 0, 0, 0))
    n_spec = pl.BlockSpec((SEQ_PER_TILE, N_HEADS, HEAD_DIM), lambda i: (i, 0, 0))
    return pl.pallas_call(
        _mix_sample_kernel,
        grid=(n_tiles,),
        in_specs=[row(N_MAIN), row(GATE_W), const(ws_t.shape), const(bs_cols.shape), const(cw.shape),
                  const(cb.shape), const(gmg.shape), const(mlg.shape), row(2 * GROUP_W),
                  c_spec, n_spec, row(N_HEADS)],
        out_specs=[row(D_MODEL), c_spec, n_spec, row(GATE_W)],
        out_shape=[jax.ShapeDtypeStruct((n, D_MODEL), F32),
                   jax.ShapeDtypeStruct(c0.shape, F32),
                   jax.ShapeDtypeStruct(n0.shape, F32),
                   jax.ShapeDtypeStruct((n, GATE_W), F32)],
        compiler_params=pltpu.CompilerParams(dimension_semantics=("arbitrary",),
                                             vmem_limit_bytes=VMEM_LIMIT_BYTES),
        name="mix_sample",
    )(zz, gates, ws_t, bs_cols, cw, cb, gmg, mlg, prev, c0, n0, m0_tok)


def _ffn_weights(wg, wu, wd):
    return wg.astype(BF16), wu.astype(BF16), wd.astype(BF16)


def _gate_columns(w):
    out = jnp.zeros(w.shape[:-1] + (GATE_W,), w.dtype)
    out = out.at[..., 0:N_HEADS].set(w[..., N_MAIN:N_MAIN + N_HEADS])
    return out.at[..., FG_LANE:FG_LANE + N_HEADS].set(w[..., N_MAIN + N_HEADS:N_MAIN + 2 * N_HEADS])


def kernel(x_prompt, x_sample, state_conv, state_C, state_n, state_m, ffn1_wg, ffn1_wu, ffn1_wd, ln1_g, ln1_b, w_in, b_in, gm_ln_g, gm_ln_b, gm_ws, gm_bs, conv_w, conv_b, gm_out_g, ml_out_g, w_out, ln2_g, ln2_b, ffn2_wg, ffn2_wu, ffn2_wd, ln3_g, ln3_b):
    depth = ffn1_wg.shape[0]
    bp, seq, _ = x_prompt.shape
    bs, dec_seq, _ = x_sample.shape
    assert dec_seq == DEC_SEQ and seq % TOKEN_TILE == 0
    assert (bs * dec_seq) % (TOKEN_TILE * STAGE_C_SUBTILES) == 0 and (bp * seq) % (TOKEN_TILE * STAGE_C_SUBTILES) == 0
    y_p = x_prompt.reshape(bp * seq, D_MODEL)
    y_s = x_sample.reshape(bs * dec_seq, D_MODEL)
    outs = []
    for l in range(depth):
        f1 = _ffn_weights(ffn1_wg[l], ffn1_wu[l], ffn1_wd[l])
        f2 = _ffn_weights(ffn2_wg[l], ffn2_wu[l], ffn2_wd[l])
        row = lambda a: a.reshape(1, -1)
        a_params = (*f1, row(ln1_g[l]), row(ln1_b[l]),
                    w_in[l].astype(BF16), row(b_in[l]),
                    _gate_columns(w_in[l]).astype(BF16), row(_gate_columns(b_in[l])),
                    row(gm_ln_g[l]), row(gm_ln_b[l]))
        c_params = (w_out[l].astype(BF16), row(ln2_g[l]), row(ln2_b[l]), *f2, row(ln3_g[l]), row(ln3_b[l]))
        mix_params = (conv_w[l], row(conv_b[l]), row(gm_out_g[l]), row(ml_out_g[l]))

        x1_p, mix_p, cn_p, m_p, gmv_p, ctail_p = _stage_a_mix_prompt(
            y_p, a_params, gm_ws[l], jnp.transpose(gm_bs[l]), *mix_params, bp, seq)
        x1_s, zz_s, g_s = _stage_a(y_s, *a_params)

        ws_t = jnp.tile(gm_ws[l][:, :DEC_SEQ, :DEC_SEQ], (1, SEQ_PER_TILE, SEQ_PER_TILE))
        bs_t = jnp.tile(jnp.transpose(gm_bs[l][:, :DEC_SEQ]), (SEQ_PER_TILE, 1))
        prev = jnp.pad(state_conv[l], ((0, 0), (DEC_SEQ - (CONV_W - 1), 0), (0, 0))).reshape(bs * DEC_SEQ, 2 * GROUP_W)
        m0_tok = jnp.repeat(state_m[l], DEC_SEQ, axis=0)
        mix_s, c_s, n_s, mt_s = _mix_sample(zz_s, g_s, ws_t, bs_t, *mix_params, prev, state_C[l], state_n[l], m0_tok)

        y_p = _stage_c(mix_p, x1_p, *c_params)
        y_s = _stage_c(mix_s, x1_s, *c_params)

        zz_s3 = zz_s.reshape(bs, dec_seq, N_MAIN)
        outs.append((
            gmv_p.reshape(bp, CHUNK, N_HEADS, HEAD_DIM),
            zz_s3[:, :, GROUP_W:2 * GROUP_W].reshape(bs, dec_seq, N_HEADS, HEAD_DIM),
            ctail_p[:, 8 - (CONV_W - 1):, :],
            zz_s3[:, dec_seq - (CONV_W - 1):, 2 * GROUP_W:4 * GROUP_W],
            cn_p[..., :HEAD_DIM], c_s, cn_p[..., HEAD_DIM], n_s,
            m_p[:, 0, :N_HEADS],
            mt_s.reshape(bs, dec_seq, GATE_W)[:, dec_seq - 1, :N_HEADS],
        ))
    stacked = [jnp.stack(a) for a in zip(*outs)]
    return (y_p.reshape(bp, seq, D_MODEL), y_s.reshape(bs, dec_seq, D_MODEL), *stacked)
```

```python
import functools

import jax
import jax.numpy as jnp
from jax import lax
from jax.experimental import pallas as pl
from jax.experimental.pallas import tpu as pltpu

F32 = jnp.float32
BF16 = jnp.bfloat16

D_MODEL = 1024
D_FF = 2816
HEAD_DIM = 128
N_HEADS = 4
GROUP_W = N_HEADS * HEAD_DIM
CHUNK = 128
CONV_W = 4
DEC_SEQ = 8
SEQ_PER_TILE = CHUNK // DEC_SEQ
N_MAIN = 6 * GROUP_W
GATE_W = 128
FG_LANE = 64
FF_CHUNK = 256
N_FF_CHUNKS = D_FF // FF_CHUNK
FF_GROUP = 4
ALPHA = 2.0 ** 0.25
LN_EPS = 1e-5
TOKEN_TILE = 512
STAGE_C_SUBTILES = 2
PAD_ROWS = 8
MIX_UNITS_PER_TILE = 6 * (TOKEN_TILE // CHUNK) + 3
MIX_SHARE_IN_FFN = (4, 5)
VMEM_LIMIT_BYTES = 60 * 1024 * 1024

_NT = (((1,), (1,)), ((), ()))
_TN = (((0,), (0,)), ((), ()))


def _layer_norm(x, g, b):
    mu = jnp.mean(x, axis=-1, keepdims=True)
    xc = x - mu
    var = jnp.mean(xc * xc, axis=-1, keepdims=True)
    return xc * lax.rsqrt(var + LN_EPS) * g + b


def _head_rms(x, g):
    return x * lax.rsqrt(jnp.mean(x * x, axis=-1, keepdims=True) + LN_EPS) * g


def _silu(x):
    return x * jax.nn.sigmoid(x)


def _log_sigmoid(x):
    return jnp.minimum(x, 0.0) - jnp.log1p(jnp.exp(-jnp.abs(x)))


def _swiglu(xb_ref, wg_ref, wu_ref, wd_ref, h_ref, between_chunks=None):
    acc = None
    for c in range(N_FF_CHUNKS):
        cols = slice(c * FF_CHUNK, (c + 1) * FF_CHUNK)
        xb = xb_ref[...]
        g = jnp.dot(xb, wg_ref[:, cols], preferred_element_type=F32)
        u = jnp.dot(xb, wu_ref[:, cols], preferred_element_type=F32)
        h_ref[:, cols] = (_silu(g) * u).astype(BF16)
        if (c + 1) % FF_GROUP == 0 or c == N_FF_CHUNKS - 1:
            grp = slice((c // FF_GROUP) * FF_GROUP * FF_CHUNK, (c + 1) * FF_CHUNK)
            d = jnp.dot(h_ref[:, grp], wd_ref[grp, :], preferred_element_type=F32)
            acc = d if acc is None else acc + d
        if between_chunks is not None:
            between_chunks(c)
    return acc


def _ffn1_and_project(x_ref, wg_ref, wu_ref, wd_ref, lng_ref, lnb_ref, wmain_ref, bmain_ref, wgate_ref, bgate_ref,
                      vlng_ref, vlnb_ref, x1_ref, xb_ref, h_ref, store_z, store_gate, between_chunks=None):
    x = x_ref[...]
    xb_ref[...] = x.astype(BF16)
    ffn = _swiglu(xb_ref, wg_ref, wu_ref, wd_ref, h_ref, between_chunks)
    x1 = _layer_norm(ALPHA * x + 0.5 * ffn, lng_ref[...], lnb_ref[...])
    x1_ref[...] = x1
    xb_ref[...] = x1.astype(BF16)
    for grp in range(N_MAIN // GROUP_W):
        cols = slice(grp * GROUP_W, (grp + 1) * GROUP_W)
        z = jnp.dot(xb_ref[...], wmain_ref[:, cols], preferred_element_type=F32) + bmain_ref[:, cols]
        if grp == 0:
            store_z(cols, jax.nn.gelu(z))
        elif grp == 1:
            z = jax.nn.gelu(z)
            for h in range(N_HEADS):
                hc = slice(h * HEAD_DIM, (h + 1) * HEAD_DIM)
                store_z(slice(GROUP_W + h * HEAD_DIM, GROUP_W + (h + 1) * HEAD_DIM),
                        _layer_norm(z[:, hc], vlng_ref[:, hc], vlnb_ref[:, hc]))
        else:
            store_z(cols, z)
    store_gate(jnp.dot(xb_ref[...], wgate_ref[...], preferred_element_type=F32) + bgate_ref[...])


def _stage_a_kernel(*refs):
    *in_refs, x1_ref, zz_ref, gate_ref, xb_ref, h_ref = refs

    def store_z(cols, val):
        zz_ref[:, cols] = val

    def store_gate(val):
        gate_ref[...] = val

    _ffn1_and_project(*in_refs, x1_ref, xb_ref, h_ref, store_z, store_gate)


def _const_spec(shape):
    nd = len(shape)
    return pl.BlockSpec(shape, lambda i, _nd=nd: (0,) * _nd, pipeline_mode=pl.Buffered(1))


def _stage_a(x, wg, wu, wd, lng, lnb, wmain, bmain, wgate, bgate, vlng, vlnb):
    n = x.shape[0]
    tm = TOKEN_TILE
    row = lambda w: pl.BlockSpec((tm, w), lambda i: (i, 0))
    return pl.pallas_call(
        _stage_a_kernel,
        grid=(n // tm,),
        in_specs=[row(D_MODEL),
                  _const_spec(wg.shape), _const_spec(wu.shape), _const_spec(wd.shape),
                  _const_spec(lng.shape), _const_spec(lnb.shape),
                  _const_spec(wmain.shape), _const_spec(bmain.shape),
                  _const_spec(wgate.shape), _const_spec(bgate.shape),
                  _const_spec(vlng.shape), _const_spec(vlnb.shape)],
        out_specs=[row(D_MODEL), row(N_MAIN), row(GATE_W)],
        out_shape=[jax.ShapeDtypeStruct((n, D_MODEL), F32),
                   jax.ShapeDtypeStruct((n, N_MAIN), F32),
                   jax.ShapeDtypeStruct((n, GATE_W), F32)],
        scratch_shapes=[pltpu.VMEM((tm, D_MODEL), BF16), pltpu.VMEM((tm, D_FF), BF16)],
        compiler_params=pltpu.CompilerParams(dimension_semantics=("arbitrary",),
                                             vmem_limit_bytes=VMEM_LIMIT_BYTES),
        name="stage_a",
    )(x, wg, wu, wd, lng, lnb, wmain, bmain, wgate, bgate, vlng, vlnb)


def _stage_c_kernel(mix_ref, x1_ref, wout_ref, ln2g_ref, ln2b_ref, wg_ref, wu_ref, wd_ref, ln3g_ref, ln3b_ref,
                    y_ref, xb_ref, h_ref):
    tiles = [slice(s * TOKEN_TILE, (s + 1) * TOKEN_TILE) for s in range(STAGE_C_SUBTILES)]
    for rows in tiles:
        proj = jnp.dot(mix_ref[rows, :].astype(BF16), wout_ref[...], preferred_element_type=F32)
        y = _layer_norm(ALPHA * x1_ref[rows, :] + proj, ln2g_ref[...], ln2b_ref[...])
        y_ref[rows, :] = y
        xb_ref[rows, :] = y.astype(BF16)
    for rows in tiles:
        ffn = _swiglu(xb_ref.at[rows, :], wg_ref, wu_ref, wd_ref, h_ref.at[rows, :])
        y_ref[rows, :] = _layer_norm(ALPHA * y_ref[rows, :] + 0.5 * ffn, ln3g_ref[...], ln3b_ref[...])


def _stage_c(mix, x1, wout, ln2g, ln2b, wg, wu, wd, ln3g, ln3b):
    n = x1.shape[0]
    tm = TOKEN_TILE * STAGE_C_SUBTILES
    row = lambda w: pl.BlockSpec((tm, w), lambda i: (i, 0))
    return pl.pallas_call(
        _stage_c_kernel,
        grid=(n // tm,),
        in_specs=[row(D_MODEL), row(D_MODEL),
                  _const_spec(wout.shape), _const_spec(ln2g.shape), _const_spec(ln2b.shape),
                  _const_spec(wg.shape), _const_spec(wu.shape), _const_spec(wd.shape),
                  _const_spec(ln3g.shape), _const_spec(ln3b.shape)],
        out_specs=row(D_MODEL),
        out_shape=jax.ShapeDtypeStruct((n, D_MODEL), F32),
        scratch_shapes=[pltpu.VMEM((tm, D_MODEL), BF16), pltpu.VMEM((tm, D_FF), BF16)],
        compiler_params=pltpu.CompilerParams(dimension_semantics=("arbitrary",),
                                             vmem_limit_bytes=VMEM_LIMIT_BYTES),
        name="stage_c",
    )(mix, x1, wout, ln2g, ln2b, wg, wu, wd, ln3g, ln3b)


def _head_cols(h, base=0):
    return slice(base + h * HEAD_DIM, base + (h + 1) * HEAD_DIM)


def _gmlp_heads(zz_ref, ws_ref, bs_ref, gmg_ref, mix_ref, mask):
    for h in range(N_HEADS):
        u = zz_ref[:, _head_cols(h)]
        vn = zz_ref[:, _head_cols(h, GROUP_W)]
        w = jnp.where(mask, ws_ref[h], 0.0).astype(BF16)
        mixed = jnp.dot(w, vn.astype(BF16), preferred_element_type=F32) + bs_ref[:, h:h + 1]
        mix_ref[:, _head_cols(h)] = _head_rms(u * mixed, gmg_ref[:, _head_cols(h)])


def _intra_chunk(q, k, v, bcol, brow, irow, mcol, mask):
    dlog = jnp.where(mask, bcol - brow + irow, -jnp.inf)
    inter = bcol + mcol
    m_t = jnp.maximum(inter, jnp.max(dlog, axis=-1, keepdims=True))
    w_intra = jnp.exp(dlog - m_t)
    w_inter = jnp.exp(inter - m_t)
    s = lax.dot_general(q.astype(BF16), k.astype(BF16), _NT, preferred_element_type=F32) * w_intra
    sv = jnp.dot(s.astype(BF16), v.astype(BF16), preferred_element_type=F32)
    ssum = jnp.sum(s, axis=-1, keepdims=True)
    return m_t, w_inter, sv, ssum


def _conv_taps(x, prev_fn, cw_ref, cb_ref, row_in_seq):
    acc = x * cw_ref[CONV_W - 1:CONV_W, :] + cb_ref[...]
    for j in range(1, CONV_W):
        shifted = jnp.where(row_in_seq < j, prev_fn(j), pltpu.roll(x, j, 0))
        acc = acc + shifted * cw_ref[CONV_W - 1 - j:CONV_W - j, :]
    return acc


def _head_rms_mxu(x, g, ones):
    ss = jnp.dot((x * x).astype(BF16), ones, preferred_element_type=F32)
    return x * lax.rsqrt(ss * (1.0 / HEAD_DIM) + LN_EPS) * g


def _mix_prompt_tile(zt, gt, ws_ref, bs_ref, cw_ref, cb_ref, gmg_ref, mlg_ref,
                     mix_ref, cn_ref, m_ref, gmv_ref, ctail_ref, tail_ref):
    row = lax.broadcasted_iota(jnp.int32, (CHUNK, CHUNK), 0)
    col = lax.broadcasted_iota(jnp.int32, (CHUNK, CHUNK), 1)
    causal = col <= row
    causal_f = causal.astype(F32)
    qk_cols = slice(2 * GROUP_W, 4 * GROUP_W)
    scale = HEAD_DIM ** -0.5

    heads = range(N_HEADS)
    chunks = range(TOKEN_TILE // CHUNK)
    rows = [slice(PAD_ROWS + c * CHUNK, PAD_ROWS + (c + 1) * CHUNK) for c in chunks]
    out_rows = [slice(c * CHUNK, (c + 1) * CHUNK) for c in chunks]

    zt[0:PAD_ROWS, qk_cols] = tail_ref[...]
    ones = jnp.ones((HEAD_DIM, HEAD_DIM), BF16)
    ones_col = jnp.ones((CHUNK, HEAD_DIM), BF16)

    w_masked = [jnp.where(causal, ws_ref[h], 0.0).astype(BF16) for h in heads]
    bias = [jnp.broadcast_to(bs_ref[:, h:h + 1], (CHUNK, HEAD_DIM)) for h in heads]
    mixed = {}
    for c in chunks:
        for h in heads:
            mixed[c, h] = jnp.dot(w_masked[h], zt[rows[c], _head_cols(h, GROUP_W)].astype(BF16),
                                  preferred_element_type=F32)
        yield
    for c in chunks:
        for h in heads:
            gm = zt[rows[c], _head_cols(h)] * (mixed[c, h] + bias[h])
            mix_ref[out_rows[c], _head_cols(h)] = _head_rms_mxu(gm, gmg_ref[:, _head_cols(h)], ones)
        yield

    qk = []
    for c in chunks:
        r0 = rows[c].start
        acc = zt[rows[c], qk_cols] * cw_ref[CONV_W - 1:CONV_W, :] + cb_ref[...]
        for j in range(1, CONV_W):
            acc = acc + zt[r0 - j:r0 - j + CHUNK, qk_cols] * cw_ref[CONV_W - 1 - j:CONV_W - j, :]
        qk.append(_silu(acc))
        yield

    gates, bcum, gates_t, bcum_t = [], [], [], []
    for c in chunks:
        g = gt[out_rows[c], :]
        b = jnp.dot(causal_f, _log_sigmoid(g), precision=lax.Precision.HIGHEST, preferred_element_type=F32)
        gates.append(g)
        bcum.append(b)
        gates_t.append(g.T)
        bcum_t.append(b.T)
    yield
    icol = {(c, h): gates[c][:, h:h + 1] for c in chunks for h in heads}
    bcol = {(c, h): bcum[c][:, FG_LANE + h:FG_LANE + h + 1] for c in chunks for h in heads}

    dlog, mx = {}, {}
    for c in chunks:
        for h in heads:
            irow = gates_t[c][h:h + 1, :]
            brow = bcum_t[c][FG_LANE + h:FG_LANE + h + 1, :]
            dlog[c, h] = jnp.where(causal, bcol[c, h] - brow + irow, -jnp.inf)
            mx[c, h] = jnp.max(dlog[c, h], axis=-1, keepdims=True)
        yield

    m_row = m_ref[0]
    lane = lax.broadcasted_iota(jnp.int32, m_row.shape, 1)
    m_t, w_inter, dec, wk = {}, {}, {}, {}
    for h in heads:
        m_prev = m_row[:, h:h + 1]
        for c in chunks:
            inter = bcol[c, h] + m_prev
            m_t[c, h] = jnp.maximum(inter, mx[c, h])
            w_inter[c, h] = jnp.exp(inter - m_t[c, h])
            m_new = m_t[c, h][CHUNK - 1:CHUNK, :]
            b_last = bcol[c, h][CHUNK - 1:CHUNK, :]
            dec[c, h] = jnp.exp(b_last + m_prev - m_new)
            wk[c, h] = jnp.exp(b_last - bcol[c, h] + icol[c, h] - m_new)
            m_prev = m_new
        m_row = jnp.where(lane == h, m_prev, m_row)
    m_ref[0] = m_row
    yield

    sv, upd = {}, {}
    for c in chunks:
        for h in heads:
            q = qk[c][:, _head_cols(h)]
            k = qk[c][:, _head_cols(h, GROUP_W)] * scale
            v1 = jnp.concatenate([zt[rows[c], _head_cols(h, 4 * GROUP_W)].astype(BF16), ones_col], axis=1)
            s = lax.dot_general(q.astype(BF16), k.astype(BF16), _NT, preferred_element_type=F32) * jnp.exp(
                dlog[c, h] - m_t[c, h])
            sv[c, h] = jnp.dot(s.astype(BF16), v1, preferred_element_type=F32)
            kw = k * wk[c, h]
            upd[c, h] = lax.dot_general(kw.astype(BF16), v1, _TN, preferred_element_type=F32)
        yield

    for h in heads:
        cn_state = cn_ref[0, h]
        for c in chunks:
            q = qk[c][:, _head_cols(h)]
            qcn = jnp.dot(q.astype(BF16), cn_state.astype(BF16), preferred_element_type=F32)
            nd = sv[c, h] + jnp.broadcast_to(w_inter[c, h], (CHUNK, 2 * HEAD_DIM)) * qcn
            num, den = nd[:, :HEAD_DIM], nd[:, HEAD_DIM:]
            hid = num / jnp.maximum(jnp.abs(den), jnp.exp(-m_t[c, h]))
            o = zt[rows[c], _head_cols(h, 5 * GROUP_W)]
            mix_ref[out_rows[c], _head_cols(h, GROUP_W)] = _head_rms_mxu(jax.nn.sigmoid(o) * hid,
                                                                        mlg_ref[:, _head_cols(h)], ones)
            cn_state = dec[c, h] * cn_state + upd[c, h]
        cn_ref[0, h] = cn_state
        yield

    last8 = slice(PAD_ROWS + TOKEN_TILE - 8, PAD_ROWS + TOKEN_TILE)
    tail_ref[...] = zt[last8, qk_cols]
    ctail_ref[0] = zt[last8, qk_cols]
    gmv_ref[0] = zt[PAD_ROWS + TOKEN_TILE - CHUNK:PAD_ROWS + TOKEN_TILE, GROUP_W:2 * GROUP_W]


def _fused_prompt_kernel(tiles_per_seq, n_tiles, *refs):
    (x_ref, wg_ref, wu_ref, wd_ref, lng_ref, lnb_ref, wmain_ref, bmain_ref, wgate_ref, bgate_ref, vlng_ref, vlnb_ref,
     ws_ref, bs_ref, cw_ref, cb_ref, gmg_ref, mlg_ref,
     x1_ref, mix_ref, cn_ref, m_ref, gmv_ref, ctail_ref,
     xb_ref, h_ref, zbuf_ref, gbuf_ref, tail_ref) = refs
    i = pl.program_id(0)
    slot_w = i % 2
    slot_r = 1 - slot_w

    @pl.when(i == 0)
    def _():
        zbuf_ref[1] = jnp.zeros(zbuf_ref.shape[1:], F32)
        gbuf_ref[1] = jnp.zeros(gbuf_ref.shape[1:], F32)

    @pl.when((i == 0) | (i % tiles_per_seq == 1))
    def _():
        cn_ref[...] = jnp.zeros_like(cn_ref)
        m_ref[...] = jnp.zeros_like(m_ref)
        tail_ref[...] = jnp.zeros_like(tail_ref)

    def store_z(cols, val):
        zbuf_ref[slot_w, PAD_ROWS:PAD_ROWS + TOKEN_TILE, cols] = val

    def store_gate(val):
        gbuf_ref[slot_w] = val

    def mixer_pieces():
        return _mix_prompt_tile(zbuf_ref.at[slot_r], gbuf_ref.at[slot_r], ws_ref, bs_ref, cw_ref, cb_ref, gmg_ref,
                                mlg_ref, mix_ref, cn_ref, m_ref, gmv_ref, ctail_ref, tail_ref)

    @pl.when(i == n_tiles)
    def _():
        for _ in mixer_pieces():
            pass

    @pl.when(i < n_tiles)
    def _():
        mixer = mixer_pieces()
        traced = [0]

        def mixer_units(c):
            target = (MIX_SHARE_IN_FFN[0] * (c + 1) * MIX_UNITS_PER_TILE) // (MIX_SHARE_IN_FFN[1] * N_FF_CHUNKS)
            while traced[0] < target:
                next(mixer)
                traced[0] += 1

        _ffn1_and_project(x_ref, wg_ref, wu_ref, wd_ref, lng_ref, lnb_ref, wmain_ref, bmain_ref, wgate_ref,
                          bgate_ref, vlng_ref, vlnb_ref, x1_ref, xb_ref, h_ref, store_z, store_gate, mixer_units)
        for _ in mixer:
            traced[0] += 1
        assert traced[0] == MIX_UNITS_PER_TILE - 1


def _stage_a_mix_prompt(x, a_params, ws, bs_cols, cw, cb, gmg, mlg, batch, seq):
    tm = TOKEN_TILE
    n_tiles = (batch * seq) // tm
    tiles_per_seq = seq // tm
    mix_params = (ws, bs_cols, cw, cb, gmg, mlg)
    tile_in = lambda i: (jnp.minimum(i, n_tiles - 1), 0)
    tile_out = lambda i: (jnp.maximum(i - 1, 0), 0)
    seq_of = lambda i: jnp.maximum(i - 1, 0) // tiles_per_seq
    return pl.pallas_call(
        functools.partial(_fused_prompt_kernel, tiles_per_seq, n_tiles),
        grid=(n_tiles + 1,),
        in_specs=[pl.BlockSpec((tm, D_MODEL), tile_in)] + [_const_spec(p.shape) for p in (*a_params, *mix_params)],
        out_specs=[pl.BlockSpec((tm, D_MODEL), tile_in),
                   pl.BlockSpec((tm, D_MODEL), tile_out),
                   pl.BlockSpec((1, N_HEADS, HEAD_DIM, 2 * HEAD_DIM), lambda i: (seq_of(i), 0, 0, 0)),
                   pl.BlockSpec((1, 1, GATE_W), lambda i: (seq_of(i), 0, 0)),
                   pl.BlockSpec((1, CHUNK, GROUP_W), lambda i: (seq_of(i), 0, 0)),
                   pl.BlockSpec((1, 8, 2 * GROUP_W), lambda i: (seq_of(i), 0, 0))],
        out_shape=[jax.ShapeDtypeStruct((batch * seq, D_MODEL), F32),
                   jax.ShapeDtypeStruct((batch * seq, D_MODEL), F32),
                   jax.ShapeDtypeStruct((batch, N_HEADS, HEAD_DIM, 2 * HEAD_DIM), F32),
                   jax.ShapeDtypeStruct((batch, 1, GATE_W), F32),
                   jax.ShapeDtypeStruct((batch, CHUNK, GROUP_W), F32),
                   jax.ShapeDtypeStruct((batch, 8, 2 * GROUP_W), F32)],
        scratch_shapes=[pltpu.VMEM((tm, D_MODEL), BF16),
                        pltpu.VMEM((tm, D_FF), BF16),
                        pltpu.VMEM((2, PAD_ROWS + tm, N_MAIN), F32),
                        pltpu.VMEM((2, tm, GATE_W), F32),
                        pltpu.VMEM((8, 2 * GROUP_W), F32)],
        compiler_params=pltpu.CompilerParams(dimension_semantics=("arbitrary",),
                                             vmem_limit_bytes=VMEM_LIMIT_BYTES),
        name="stage_a_mix_prompt",
    )(x, *a_params, *mix_params)


def _mix_sample_kernel(zz_ref, gate_ref, ws_ref, bs_ref, cw_ref, cb_ref, gmg_ref, mlg_ref, prev_ref,
                       c0_ref, n0_ref, m0_ref, mix_ref, c_ref, n_ref, mt_ref):
    row = lax.broadcasted_iota(jnp.int32, (CHUNK, CHUNK), 0)
    col = lax.broadcasted_iota(jnp.int32, (CHUNK, CHUNK), 1)
    seq_shift = DEC_SEQ.bit_length() - 1
    mask = (col <= row) & ((col >> seq_shift) == (row >> seq_shift))
    last_sel = (col == (row | (DEC_SEQ - 1))).astype(F32)

    _gmlp_heads(zz_ref, ws_ref, bs_ref, gmg_ref, mix_ref, mask)

    x = zz_ref[:, 2 * GROUP_W:4 * GROUP_W]
    row_w = lax.broadcasted_iota(jnp.int32, (CHUNK, 2 * GROUP_W), 0) & (DEC_SEQ - 1)
    prev = prev_ref[...]
    qk = _silu(_conv_taps(x, lambda j: pltpu.roll(prev, CHUNK - DEC_SEQ + j, 0), cw_ref, cb_ref, row_w))

    gates = gate_ref[...]
    bcum = jnp.dot(mask.astype(F32), _log_sigmoid(gates), precision=lax.Precision.HIGHEST,
                   preferred_element_type=F32)
    gates_t = gates.T
    bcum_t = bcum.T
    lane = lax.broadcasted_iota(jnp.int32, (CHUNK, GATE_W), 1)
    mt_all = jnp.zeros((CHUNK, GATE_W), F32)
    scale = HEAD_DIM ** -0.5

    for h in range(N_HEADS):
        q = qk[:, _head_cols(h)]
        k = qk[:, _head_cols(h, GROUP_W)] * scale
        v = zz_ref[:, _head_cols(h, 4 * GROUP_W)]
        o = zz_ref[:, _head_cols(h, 5 * GROUP_W)]
        icol = gates[:, h:h + 1]
        bcol = bcum[:, FG_LANE + h:FG_LANE + h + 1]
        irow = gates_t[h:h + 1, :]
        brow = bcum_t[FG_LANE + h:FG_LANE + h + 1, :]
        m_prev = m0_ref[:, h:h + 1]

        m_t, w_inter, sv, ssum = _intra_chunk(q, k, v, bcol, brow, irow, m_prev, mask)
        qb = q.astype(BF16)
        qc_rows, qn_rows = [], []
        for s in range(SEQ_PER_TILE):
            rows = slice(s * DEC_SEQ, (s + 1) * DEC_SEQ)
            qc_rows.append(jnp.dot(qb[rows], c0_ref[s, h].astype(BF16), preferred_element_type=F32))
            qn_rows.append(jnp.sum(q[rows] * n0_ref[s, h:h + 1, :], axis=-1, keepdims=True))
        qc = jnp.concatenate(qc_rows, axis=0)
        qn = jnp.concatenate(qn_rows, axis=0)
        num = sv + w_inter * qc
        den = ssum + w_inter * qn
        hid = num / jnp.maximum(jnp.abs(den), jnp.exp(-m_t))
        mix_ref[:, _head_cols(h, GROUP_W)] = _head_rms(jax.nn.sigmoid(o) * hid, mlg_ref[:, _head_cols(h)])

        packed = jnp.where(lane == 0, m_t, jnp.where(lane == 1, bcol, 0.0))
        lastv = jnp.dot(last_sel, packed, precision=lax.Precision.HIGHEST, preferred_element_type=F32)
        m_new = lastv[:, 0:1]
        b_last = lastv[:, 1:2]
        dec = jnp.exp(b_last + m_prev - m_new)
        kw = k * jnp.exp(b_last - bcol + icol - m_new)
        kwb = kw.astype(BF16)
        vb = v.astype(BF16)
        for s in range(SEQ_PER_TILE):
            rows = slice(s * DEC_SEQ, (s + 1) * DEC_SEQ)
            dec_s = dec[s * DEC_SEQ:s * DEC_SEQ + 1, :]
            c_ref[s, h] = dec_s * c0_ref[s, h] + lax.dot_general(kwb[rows], vb[rows], _TN,
                                                                preferred_element_type=F32)
            n_ref[s, h:h + 1, :] = dec_s * n0_ref[s, h:h + 1, :] + jnp.sum(kw[rows], axis=0, keepdims=True)
        mt_all = jnp.where(lane == h, m_t, mt_all)

    mt_ref[...] = mt_all


def _mix_sample(zz, gates, ws_t, bs_cols, cw, cb, gmg, mlg, prev, c0, n0, m0_tok):
    n = zz.shape[0]
    n_tiles = n // CHUNK
    row = lambda w: pl.BlockSpec((CHUNK, w), lambda i: (i, 0))
    const = lambda shape: pl.BlockSpec(shape, lambda i, _nd=len(shape): (0,) * _nd)
    c_spec = pl.BlockSpec((SEQ_PER_TILE, N_HEADS, HEAD_DIM, HEAD_DIM), lambda i: (i, 0, 0, 0))
    n_spec = pl.BlockSpec((SEQ_PER_TILE, N_HEADS, HEAD_DIM), lambda i: (i, 0, 0))
    return pl.pallas_call(
        _mix_sample_kernel,
        grid=(n_tiles,),
        in_specs=[row(N_MAIN), row(GATE_W), const(ws_t.shape), const(bs_cols.shape), const(cw.shape),
                  const(cb.shape), const(gmg.shape), const(mlg.shape), row(2 * GROUP_W),
                  c_spec, n_spec, row(N_HEADS)],
        out_specs=[row(D_MODEL), c_spec, n_spec, row(GATE_W)],
        out_shape=[jax.ShapeDtypeStruct((n, D_MODEL), F32),
                   jax.ShapeDtypeStruct(c0.shape, F32),
                   jax.ShapeDtypeStruct(n0.shape, F32),
                   jax.ShapeDtypeStruct((n, GATE_W), F32)],
        compiler_params=pltpu.CompilerParams(dimension_semantics=("arbitrary",),
                                             vmem_limit_bytes=VMEM_LIMIT_BYTES),
        name="mix_sample",
    )(zz, gates, ws_t, bs_cols, cw, cb, gmg, mlg, prev, c0, n0, m0_tok)


def _ffn_weights(wg, wu, wd):
    return wg.astype(BF16), wu.astype(BF16), wd.astype(BF16)


def _gate_columns(w):
    out = jnp.zeros(w.shape[:-1] + (GATE_W,), w.dtype)
    out = out.at[..., 0:N_HEADS].set(w[..., N_MAIN:N_MAIN + N_HEADS])
    return out.at[..., FG_LANE:FG_LANE + N_HEADS].set(w[..., N_MAIN + N_HEADS:N_MAIN + 2 * N_HEADS])


def kernel(x_prompt, x_sample, state_conv, state_C, state_n, state_m, ffn1_wg, ffn1_wu, ffn1_wd, ln1_g, ln1_b, w_in, b_in, gm_ln_g, gm_ln_b, gm_ws, gm_bs, conv_w, conv_b, gm_out_g, ml_out_g, w_out, ln2_g, ln2_b, ffn2_wg, ffn2_wu, ffn2_wd, ln3_g, ln3_b):
    depth = ffn1_wg.shape[0]
    bp, seq, _ = x_prompt.shape
    bs, dec_seq, _ = x_sample.shape
    assert dec_seq == DEC_SEQ and seq % TOKEN_TILE == 0
    assert (bs * dec_seq) % (TOKEN_TILE * STAGE_C_SUBTILES) == 0 and (bp * seq) % (TOKEN_TILE * STAGE_C_SUBTILES) == 0
    y_p = x_prompt.reshape(bp * seq, D_MODEL)
    y_s = x_sample.reshape(bs * dec_seq, D_MODEL)
    outs = []
    for l in range(depth):
        f1 = _ffn_weights(ffn1_wg[l], ffn1_wu[l], ffn1_wd[l])
        f2 = _ffn_weights(ffn2_wg[l], ffn2_wu[l], ffn2_wd[l])
        row = lambda a: a.reshape(1, -1)
        a_params = (*f1, row(ln1_g[l]), row(ln1_b[l]),
                    w_in[l].astype(BF16), row(b_in[l]),
                    _gate_columns(w_in[l]).astype(BF16), row(_gate_columns(b_in[l])),
                    row(gm_ln_g[l]), row(gm_ln_b[l]))
        c_params = (w_out[l].astype(BF16), row(ln2_g[l]), row(ln2_b[l]), *f2, row(ln3_g[l]), row(ln3_b[l]))
        mix_params = (conv_w[l], row(conv_b[l]), row(gm_out_g[l]), row(ml_out_g[l]))

        x1_p, mix_p, cn_p, m_p, gmv_p, ctail_p = _stage_a_mix_prompt(
            y_p, a_params, gm_ws[l], jnp.transpose(gm_bs[l]), *mix_params, bp, seq)
        x1_s, zz_s, g_s = _stage_a(y_s, *a_params)

        ws_t = jnp.tile(gm_ws[l][:, :DEC_SEQ, :DEC_SEQ], (1, SEQ_PER_TILE, SEQ_PER_TILE))
        bs_t = jnp.tile(jnp.transpose(gm_bs[l][:, :DEC_SEQ]), (SEQ_PER_TILE, 1))
        prev = jnp.pad(state_conv[l], ((0, 0), (DEC_SEQ - (CONV_W - 1), 0), (0, 0))).reshape(bs * DEC_SEQ, 2 * GROUP_W)
        m0_tok = jnp.repeat(state_m[l], DEC_SEQ, axis=0)
        mix_s, c_s, n_s, mt_s = _mix_sample(zz_s, g_s, ws_t, bs_t, *mix_params, prev, state_C[l], state_n[l], m0_tok)

        y_p = _stage_c(mix_p, x1_p, *c_params)
        y_s = _stage_c(mix_s, x1_s, *c_params)

        zz_s3 = zz_s.reshape(bs, dec_seq, N_MAIN)
        outs.append((
            gmv_p.reshape(bp, CHUNK, N_HEADS, HEAD_DIM),
            zz_s3[:, :, GROUP_W:2 * GROUP_W].reshape(bs, dec_seq, N_HEADS, HEAD_DIM),
            ctail_p[:, 8 - (CONV_W - 1):, :],
            zz_s3[:, dec_seq - (CONV_W - 1):, 2 * GROUP_W:4 * GROUP_W],
            cn_p[..., :HEAD_DIM], c_s, cn_p[..., HEAD_DIM], n_s,
            m_p[:, 0, :N_HEADS],
            mt_s.reshape(bs, dec_seq, GATE_W)[:, dec_seq - 1, :N_HEADS],
        ))
    stacked = [jnp.stack(a) for a in zip(*outs)]
    return (y_p.reshape(bp, seq, D_MODEL), y_s.reshape(bs, dec_seq, D_MODEL), *stacked)
```

```python
import functools

import jax
import jax.numpy as jnp
from jax import lax
from jax.experimental import pallas as pl
from jax.experimental.pallas import tpu as pltpu

F32 = jnp.float32
BF16 = jnp.bfloat16

D_MODEL = 1024
D_FF = 2816
HEAD_DIM = 128
N_HEADS = 4
GROUP_W = N_HEADS * HEAD_DIM
CHUNK = 128
CONV_W = 4
DEC_SEQ = 8
SEQ_PER_TILE = CHUNK // DEC_SEQ
N_MAIN = 6 * GROUP_W
GATE_W = 128
FG_LANE = 64
FF_CHUNK = 256
N_FF_CHUNKS = D_FF // FF_CHUNK
FF_GROUP = 4
ALPHA = 2.0 ** 0.25
LN_EPS = 1e-5
TOKEN_TILE = 512
STAGE_C_SUBTILES = 2
PAD_ROWS = 8
MIX_UNITS_PER_TILE = 6 * (TOKEN_TILE // CHUNK) + 3
MIX_SHARE_IN_FFN = (4, 5)
VMEM_LIMIT_BYTES = 60 * 1024 * 1024

_NT = (((1,), (1,)), ((), ()))
_TN = (((0,), (0,)), ((), ()))


def _layer_norm(x, g, b):
    mu = jnp.mean(x, axis=-1, keepdims=True)
    xc = x - mu
    var = jnp.mean(xc * xc, axis=-1, keepdims=True)
    return xc * lax.rsqrt(var + LN_EPS) * g + b


def _head_rms_mxu(x, g, ones):
    ss = jnp.dot((x * x).astype(BF16), ones, preferred_element_type=F32)
    return x * lax.rsqrt(ss * (1.0 / HEAD_DIM) + LN_EPS) * g


def _silu(x):
    return x * jax.nn.sigmoid(x)


def _log_sigmoid(x):
    return jnp.minimum(x, 0.0) - jnp.log1p(jnp.exp(-jnp.abs(x)))


def _swiglu(xb_ref, wg_ref, wu_ref, wd_ref, h_ref, between_chunks=None):
    acc = None
    for c in range(N_FF_CHUNKS):
        cols = slice(c * FF_CHUNK, (c + 1) * FF_CHUNK)
        xb = xb_ref[...]
        g = jnp.dot(xb, wg_ref[:, cols], preferred_element_type=F32)
        u = jnp.dot(xb, wu_ref[:, cols], preferred_element_type=F32)
        h_ref[:, cols] = (_silu(g) * u).astype(BF16)
        if (c + 1) % FF_GROUP == 0 or c == N_FF_CHUNKS - 1:
            grp = slice((c // FF_GROUP) * FF_GROUP * FF_CHUNK, (c + 1) * FF_CHUNK)
            d = jnp.dot(h_ref[:, grp], wd_ref[grp, :], preferred_element_type=F32)
            acc = d if acc is None else acc + d
        if between_chunks is not None:
            between_chunks(c)
    return acc


def _ffn1_and_project(x_ref, wg_ref, wu_ref, wd_ref, lng_ref, lnb_ref, wmain_ref, bmain_ref, wgate_ref, bgate_ref,
                      vlng_ref, vlnb_ref, x1_ref, xb_ref, h_ref, store_z, store_gate, between_chunks=None):
    x = x_ref[...]
    xb_ref[...] = x.astype(BF16)
    ffn = _swiglu(xb_ref, wg_ref, wu_ref, wd_ref, h_ref, between_chunks)
    x1 = _layer_norm(ALPHA * x + 0.5 * ffn, lng_ref[...], lnb_ref[...])
    x1_ref[...] = x1
    xb_ref[...] = x1.astype(BF16)
    for grp in range(N_MAIN // GROUP_W):
        cols = slice(grp * GROUP_W, (grp + 1) * GROUP_W)
        z = jnp.dot(xb_ref[...], wmain_ref[:, cols], preferred_element_type=F32) + bmain_ref[:, cols]
        if grp == 0:
            store_z(cols, jax.nn.gelu(z))
        elif grp == 1:
            z = jax.nn.gelu(z)
            for h in range(N_HEADS):
                hc = slice(h * HEAD_DIM, (h + 1) * HEAD_DIM)
                store_z(slice(GROUP_W + h * HEAD_DIM, GROUP_W + (h + 1) * HEAD_DIM),
                        _layer_norm(z[:, hc], vlng_ref[:, hc], vlnb_ref[:, hc]))
        else:
            store_z(cols, z)
    store_gate(jnp.dot(xb_ref[...], wgate_ref[...], preferred_element_type=F32) + bgate_ref[...])


def _stage_a_kernel(*refs):
    *in_refs, x1_ref, zz_ref, gate_ref, xb_ref, h_ref = refs

    def store_z(cols, val):
        zz_ref[:, cols] = val

    def store_gate(val):
        gate_ref[...] = val

    _ffn1_and_project(*in_refs, x1_ref, xb_ref, h_ref, store_z, store_gate)


def _const_spec(shape):
    nd = len(shape)
    return pl.BlockSpec(shape, lambda i, _nd=nd: (0,) * _nd, pipeline_mode=pl.Buffered(1))


def _stage_a(x, wg, wu, wd, lng, lnb, wmain, bmain, wgate, bgate, vlng, vlnb):
    n = x.shape[0]
    tm = TOKEN_TILE
    row = lambda w: pl.BlockSpec((tm, w), lambda i: (i, 0))
    return pl.pallas_call(
        _stage_a_kernel,
        grid=(n // tm,),
        in_specs=[row(D_MODEL),
                  _const_spec(wg.shape), _const_spec(wu.shape), _const_spec(wd.shape),
                  _const_spec(lng.shape), _const_spec(lnb.shape),
                  _const_spec(wmain.shape), _const_spec(bmain.shape),
                  _const_spec(wgate.shape), _const_spec(bgate.shape),
                  _const_spec(vlng.shape), _const_spec(vlnb.shape)],
        out_specs=[row(D_MODEL), row(N_MAIN), row(GATE_W)],
        out_shape=[jax.ShapeDtypeStruct((n, D_MODEL), F32),
                   jax.ShapeDtypeStruct((n, N_MAIN), F32),
                   jax.ShapeDtypeStruct((n, GATE_W), F32)],
        scratch_shapes=[pltpu.VMEM((tm, D_MODEL), BF16), pltpu.VMEM((tm, D_FF), BF16)],
        compiler_params=pltpu.CompilerParams(dimension_semantics=("arbitrary",),
                                             vmem_limit_bytes=VMEM_LIMIT_BYTES),
        name="stage_a",
    )(x, wg, wu, wd, lng, lnb, wmain, bmain, wgate, bgate, vlng, vlnb)


def _stage_c_kernel(mix_ref, x1_ref, wout_ref, ln2g_ref, ln2b_ref, wg_ref, wu_ref, wd_ref, ln3g_ref, ln3b_ref,
                    y_ref, xb_ref, h_ref):
    tiles = [slice(s * TOKEN_TILE, (s + 1) * TOKEN_TILE) for s in range(STAGE_C_SUBTILES)]
    for rows in tiles:
        proj = jnp.dot(mix_ref[rows, :].astype(BF16), wout_ref[...], preferred_element_type=F32)
        y = _layer_norm(ALPHA * x1_ref[rows, :] + proj, ln2g_ref[...], ln2b_ref[...])
        y_ref[rows, :] = y
        xb_ref[rows, :] = y.astype(BF16)
    for rows in tiles:
        ffn = _swiglu(xb_ref.at[rows, :], wg_ref, wu_ref, wd_ref, h_ref.at[rows, :])
        y_ref[rows, :] = _layer_norm(ALPHA * y_ref[rows, :] + 0.5 * ffn, ln3g_ref[...], ln3b_ref[...])


def _stage_c(mix, x1, wout, ln2g, ln2b, wg, wu, wd, ln3g, ln3b):
    n = x1.shape[0]
    tm = TOKEN_TILE * STAGE_C_SUBTILES
    row = lambda w: pl.BlockSpec((tm, w), lambda i: (i, 0))
    return pl.pallas_call(
        _stage_c_kernel,
        grid=(n // tm,),
        in_specs=[row(D_MODEL), row(D_MODEL),
                  _const_spec(wout.shape), _const_spec(ln2g.shape), _const_spec(ln2b.shape),
                  _const_spec(wg.shape), _const_spec(wu.shape), _const_spec(wd.shape),
                  _const_spec(ln3g.shape), _const_spec(ln3b.shape)],
        out_specs=row(D_MODEL),
        out_shape=jax.ShapeDtypeStruct((n, D_MODEL), F32),
        scratch_shapes=[pltpu.VMEM((tm, D_MODEL), BF16), pltpu.VMEM((tm, D_FF), BF16)],
        compiler_params=pltpu.CompilerParams(dimension_semantics=("arbitrary",),
                                             vmem_limit_bytes=VMEM_LIMIT_BYTES),
        name="stage_c",
    )(mix, x1, wout, ln2g, ln2b, wg, wu, wd, ln3g, ln3b)


def _head_cols(h, base=0):
    return slice(base + h * HEAD_DIM, base + (h + 1) * HEAD_DIM)


def _gmlp_heads(zz_ref, ws_ref, bs_ref, gmg_ref, mix_ref, mask, ones):
    for h in range(N_HEADS):
        u = zz_ref[:, _head_cols(h)]
        vn = zz_ref[:, _head_cols(h, GROUP_W)]
        w = jnp.where(mask, ws_ref[h], 0.0).astype(BF16)
        mixed = jnp.dot(w, vn.astype(BF16), preferred_element_type=F32) + bs_ref[:, h:h + 1]
        mix_ref[:, _head_cols(h)] = _head_rms_mxu(u * mixed, gmg_ref[:, _head_cols(h)], ones)


def _intra_chunk(q, k, v, bcol, brow, irow, mcol, mask):
    dlog = jnp.where(mask, bcol - brow + irow, -jnp.inf)
    inter = bcol + mcol
    m_t = jnp.maximum(inter, jnp.max(dlog, axis=-1, keepdims=True))
    w_intra = jnp.exp(dlog - m_t)
    w_inter = jnp.exp(inter - m_t)
    s = lax.dot_general(q.astype(BF16), k.astype(BF16), _NT, preferred_element_type=F32) * w_intra
    sv = jnp.dot(s.astype(BF16), v.astype(BF16), preferred_element_type=F32)
    ssum = jnp.sum(s, axis=-1, keepdims=True)
    return m_t, w_inter, sv, ssum


def _conv_taps(x, prev_fn, cw_ref, cb_ref, row_in_seq):
    acc = x * cw_ref[CONV_W - 1:CONV_W, :] + cb_ref[...]
    for j in range(1, CONV_W):
        shifted = jnp.where(row_in_seq < j, prev_fn(j), pltpu.roll(x, j, 0))
        acc = acc + shifted * cw_ref[CONV_W - 1 - j:CONV_W - j, :]
    return acc


def _mix_prompt_tile(zt, gt, ws_ref, bs_ref, cw_ref, cb_ref, gmg_ref, mlg_ref,
                     mix_ref, cn_ref, m_ref, gmv_ref, ctail_ref, tail_ref):
    row = lax.broadcasted_iota(jnp.int32, (CHUNK, CHUNK), 0)
    col = lax.broadcasted_iota(jnp.int32, (CHUNK, CHUNK), 1)
    causal = col <= row
    causal_f = causal.astype(F32)
    qk_cols = slice(2 * GROUP_W, 4 * GROUP_W)
    scale = HEAD_DIM ** -0.5

    heads = range(N_HEADS)
    chunks = range(TOKEN_TILE // CHUNK)
    rows = [slice(PAD_ROWS + c * CHUNK, PAD_ROWS + (c + 1) * CHUNK) for c in chunks]
    out_rows = [slice(c * CHUNK, (c + 1) * CHUNK) for c in chunks]

    zt[0:PAD_ROWS, qk_cols] = tail_ref[...]
    ones = jnp.ones((HEAD_DIM, HEAD_DIM), BF16)
    ones_col = jnp.ones((CHUNK, HEAD_DIM), BF16)

    w_masked = [jnp.where(causal, ws_ref[h], 0.0).astype(BF16) for h in heads]
    bias = [jnp.broadcast_to(bs_ref[:, h:h + 1], (CHUNK, HEAD_DIM)) for h in heads]
    mixed = {}
    for c in chunks:
        for h in heads:
            mixed[c, h] = jnp.dot(w_masked[h], zt[rows[c], _head_cols(h, GROUP_W)].astype(BF16),
                                  preferred_element_type=F32)
        yield
    for c in chunks:
        for h in heads:
            gm = zt[rows[c], _head_cols(h)] * (mixed[c, h] + bias[h])
            mix_ref[out_rows[c], _head_cols(h)] = _head_rms_mxu(gm, gmg_ref[:, _head_cols(h)], ones)
        yield

    qk = []
    for c in chunks:
        r0 = rows[c].start
        acc = zt[rows[c], qk_cols] * cw_ref[CONV_W - 1:CONV_W, :] + cb_ref[...]
        for j in range(1, CONV_W):
            acc = acc + zt[r0 - j:r0 - j + CHUNK, qk_cols] * cw_ref[CONV_W - 1 - j:CONV_W - j, :]
        qk.append(_silu(acc))
        yield

    gates, bcum, gates_t, bcum_t = [], [], [], []
    for c in chunks:
        g = gt[out_rows[c], :]
        b = jnp.dot(causal_f, _log_sigmoid(g), precision=lax.Precision.HIGHEST, preferred_element_type=F32)
        gates.append(g)
        bcum.append(b)
        gates_t.append(g.T)
        bcum_t.append(b.T)
    yield
    icol = {(c, h): gates[c][:, h:h + 1] for c in chunks for h in heads}
    bcol = {(c, h): bcum[c][:, FG_LANE + h:FG_LANE + h + 1] for c in chunks for h in heads}

    dlog, mx = {}, {}
    for c in chunks:
        for h in heads:
            irow = gates_t[c][h:h + 1, :]
            brow = bcum_t[c][FG_LANE + h:FG_LANE + h + 1, :]
            dlog[c, h] = jnp.where(causal, bcol[c, h] - brow + irow, -jnp.inf)
            mx[c, h] = jnp.max(dlog[c, h], axis=-1, keepdims=True)
        yield

    m_row = m_ref[0]
    lane = lax.broadcasted_iota(jnp.int32, m_row.shape, 1)
    m_t, w_inter, dec, wk = {}, {}, {}, {}
    for h in heads:
        m_prev = m_row[:, h:h + 1]
        for c in chunks:
            inter = bcol[c, h] + m_prev
            m_t[c, h] = jnp.maximum(inter, mx[c, h])
            w_inter[c, h] = jnp.exp(inter - m_t[c, h])
            m_new = m_t[c, h][CHUNK - 1:CHUNK, :]
            b_last = bcol[c, h][CHUNK - 1:CHUNK, :]
            dec[c, h] = jnp.exp(b_last + m_prev - m_new)
            wk[c, h] = jnp.exp(b_last - bcol[c, h] + icol[c, h] - m_new)
            m_prev = m_new
        m_row = jnp.where(lane == h, m_prev, m_row)
    m_ref[0] = m_row
    yield

    sv, upd = {}, {}
    for c in chunks:
        for h in heads:
            q = qk[c][:, _head_cols(h)]
            k = qk[c][:, _head_cols(h, GROUP_W)] * scale
            v1 = jnp.concatenate([zt[rows[c], _head_cols(h, 4 * GROUP_W)].astype(BF16), ones_col], axis=1)
            s = lax.dot_general(q.astype(BF16), k.astype(BF16), _NT, preferred_element_type=F32) * jnp.exp(
                dlog[c, h] - m_t[c, h])
            sv[c, h] = jnp.dot(s.astype(BF16), v1, preferred_element_type=F32)
            kw = k * wk[c, h]
            upd[c, h] = lax.dot_general(kw.astype(BF16), v1, _TN, preferred_element_type=F32)
        yield

    for h in heads:
        cn_state = cn_ref[0, h]
        for c in chunks:
            q = qk[c][:, _head_cols(h)]
            qcn = jnp.dot(q.astype(BF16), cn_state.astype(BF16), preferred_element_type=F32)
            nd = sv[c, h] + jnp.broadcast_to(w_inter[c, h], (CHUNK, 2 * HEAD_DIM)) * qcn
            num, den = nd[:, :HEAD_DIM], nd[:, HEAD_DIM:]
            hid = num / jnp.maximum(jnp.abs(den), jnp.exp(-m_t[c, h]))
            o = zt[rows[c], _head_cols(h, 5 * GROUP_W)]
            mix_ref[out_rows[c], _head_cols(h, GROUP_W)] = _head_rms_mxu(jax.nn.sigmoid(o) * hid,
                                                                        mlg_ref[:, _head_cols(h)], ones)
            cn_state = dec[c, h] * cn_state + upd[c, h]
        cn_ref[0, h] = cn_state
        yield

    last8 = slice(PAD_ROWS + TOKEN_TILE - 8, PAD_ROWS + TOKEN_TILE)
    tail_ref[...] = zt[last8, qk_cols]
    ctail_ref[0] = zt[last8, qk_cols]
    gmv_ref[0] = zt[PAD_ROWS + TOKEN_TILE - CHUNK:PAD_ROWS + TOKEN_TILE, GROUP_W:2 * GROUP_W]


def _fused_prompt_kernel(tiles_per_seq, *refs):
    (x_ref, wg_ref, wu_ref, wd_ref, lng_ref, lnb_ref, wmain_ref, bmain_ref, wgate_ref, bgate_ref, vlng_ref, vlnb_ref,
     ws_ref, bs_ref, cw_ref, cb_ref, gmg_ref, mlg_ref,
     x1_ref, mix_ref, cn_ref, m_ref, gmv_ref, ctail_ref,
     xb_ref, h_ref, zbuf_ref, gbuf_ref, tail_ref) = refs
    i = pl.program_id(0)
    slot_w = i % 2
    slot_r = 1 - slot_w

    @pl.when(i == 0)
    def _():
        zbuf_ref[1] = jnp.zeros(zbuf_ref.shape[1:], F32)
        gbuf_ref[1] = jnp.zeros(gbuf_ref.shape[1:], F32)

    @pl.when((i == 0) | (i % tiles_per_seq == 1))
    def _():
        cn_ref[...] = jnp.zeros_like(cn_ref)
        m_ref[...] = jnp.zeros_like(m_ref)
        tail_ref[...] = jnp.zeros_like(tail_ref)

    def store_z(cols, val):
        zbuf_ref[slot_w, PAD_ROWS:PAD_ROWS + TOKEN_TILE, cols] = val

    def store_gate(val):
        gbuf_ref[slot_w] = val

    mixer = _mix_prompt_tile(zbuf_ref.at[slot_r], gbuf_ref.at[slot_r], ws_ref, bs_ref, cw_ref, cb_ref, gmg_ref,
                             mlg_ref, mix_ref, cn_ref, m_ref, gmv_ref, ctail_ref, tail_ref)

    traced = [0]

    def mixer_units(c):
        target = (MIX_SHARE_IN_FFN[0] * (c + 1) * MIX_UNITS_PER_TILE) // (MIX_SHARE_IN_FFN[1] * N_FF_CHUNKS)
        while traced[0] < target:
            next(mixer)
            traced[0] += 1

    _ffn1_and_project(x_ref, wg_ref, wu_ref, wd_ref, lng_ref, lnb_ref, wmain_ref, bmain_ref, wgate_ref, bgate_ref,
                      vlng_ref, vlnb_ref, x1_ref, xb_ref, h_ref, store_z, store_gate, mixer_units)
    for _ in mixer:
        traced[0] += 1
    assert traced[0] == MIX_UNITS_PER_TILE - 1


def _stage_a_mix_prompt(x, a_params, ws, bs_cols, cw, cb, gmg, mlg, batch, seq):
    tm = TOKEN_TILE
    n_tiles = (batch * seq) // tm
    tiles_per_seq = seq // tm
    mix_params = (ws, bs_cols, cw, cb, gmg, mlg)
    tile_in = lambda i: (jnp.minimum(i, n_tiles - 1), 0)
    tile_out = lambda i: (jnp.maximum(i - 1, 0), 0)
    seq_of = lambda i: jnp.maximum(i - 1, 0) // tiles_per_seq
    return pl.pallas_call(
        functools.partial(_fused_prompt_kernel, tiles_per_seq),
        grid=(n_tiles + 1,),
        in_specs=[pl.BlockSpec((tm, D_MODEL), tile_in)] + [_const_spec(p.shape) for p in (*a_params, *mix_params)],
        out_specs=[pl.BlockSpec((tm, D_MODEL), tile_in),
                   pl.BlockSpec((tm, D_MODEL), tile_out),
                   pl.BlockSpec((1, N_HEADS, HEAD_DIM, 2 * HEAD_DIM), lambda i: (seq_of(i), 0, 0, 0)),
                   pl.BlockSpec((1, 1, GATE_W), lambda i: (seq_of(i), 0, 0)),
                   pl.BlockSpec((1, CHUNK, GROUP_W), lambda i: (seq_of(i), 0, 0)),
                   pl.BlockSpec((1, 8, 2 * GROUP_W), lambda i: (seq_of(i), 0, 0))],
        out_shape=[jax.ShapeDtypeStruct((batch * seq, D_MODEL), F32),
                   jax.ShapeDtypeStruct((batch * seq, D_MODEL), F32),
                   jax.ShapeDtypeStruct((batch, N_HEADS, HEAD_DIM, 2 * HEAD_DIM), F32),
                   jax.ShapeDtypeStruct((batch, 1, GATE_W), F32),
                   jax.ShapeDtypeStruct((batch, CHUNK, GROUP_W), F32),
                   jax.ShapeDtypeStruct((batch, 8, 2 * GROUP_W), F32)],
        scratch_shapes=[pltpu.VMEM((tm, D_MODEL), BF16),
                        pltpu.VMEM((tm, D_FF), BF16),
                        pltpu.VMEM((2, PAD_ROWS + tm, N_MAIN), F32),
                        pltpu.VMEM((2, tm, GATE_W), F32),
                        pltpu.VMEM((8, 2 * GROUP_W), F32)],
        compiler_params=pltpu.CompilerParams(dimension_semantics=("arbitrary",),
                                             vmem_limit_bytes=VMEM_LIMIT_BYTES),
        name="stage_a_mix_prompt",
    )(x, *a_params, *mix_params)


def _mix_sample_kernel(zz_ref, gate_ref, ws_ref, bs_ref, cw_ref, cb_ref, gmg_ref, mlg_ref, prev_ref,
                       c0_ref, n0_ref, m0_ref, mix_ref, c_ref, n_ref, mt_ref):
    row = lax.broadcasted_iota(jnp.int32, (CHUNK, CHUNK), 0)
    col = lax.broadcasted_iota(jnp.int32, (CHUNK, CHUNK), 1)
    seq_shift = DEC_SEQ.bit_length() - 1
    mask = (col <= row) & ((col >> seq_shift) == (row >> seq_shift))
    last_sel = (col == (row | (DEC_SEQ - 1))).astype(F32)

    ones = jnp.ones((HEAD_DIM, HEAD_DIM), BF16)
    _gmlp_heads(zz_ref, ws_ref, bs_ref, gmg_ref, mix_ref, mask, ones)

    x = zz_ref[:, 2 * GROUP_W:4 * GROUP_W]
    row_w = lax.broadcasted_iota(jnp.int32, (CHUNK, 2 * GROUP_W), 0) & (DEC_SEQ - 1)
    prev = prev_ref[...]
    qk = _silu(_conv_taps(x, lambda j: pltpu.roll(prev, CHUNK - DEC_SEQ + j, 0), cw_ref, cb_ref, row_w))

    gates = gate_ref[...]
    bcum = jnp.dot(mask.astype(F32), _log_sigmoid(gates), precision=lax.Precision.HIGHEST,
                   preferred_element_type=F32)
    gates_t = gates.T
    bcum_t = bcum.T
    lane = lax.broadcasted_iota(jnp.int32, (CHUNK, GATE_W), 1)
    mt_all = jnp.zeros((CHUNK, GATE_W), F32)
    scale = HEAD_DIM ** -0.5

    for h in range(N_HEADS):
        q = qk[:, _head_cols(h)]
        k = qk[:, _head_cols(h, GROUP_W)] * scale
        v = zz_ref[:, _head_cols(h, 4 * GROUP_W)]
        o = zz_ref[:, _head_cols(h, 5 * GROUP_W)]
        icol = gates[:, h:h + 1]
        bcol = bcum[:, FG_LANE + h:FG_LANE + h + 1]
        irow = gates_t[h:h + 1, :]
        brow = bcum_t[FG_LANE + h:FG_LANE + h + 1, :]
        m_prev = m0_ref[:, h:h + 1]

        m_t, w_inter, sv, ssum = _intra_chunk(q, k, v, bcol, brow, irow, m_prev, mask)
        qb = q.astype(BF16)
        qc_rows, qn_rows = [], []
        for s in range(SEQ_PER_TILE):
            rows = slice(s * DEC_SEQ, (s + 1) * DEC_SEQ)
            qc_rows.append(jnp.dot(qb[rows], c0_ref[s, h].astype(BF16), preferred_element_type=F32))
            qn_rows.append(jnp.sum(q[rows] * n0_ref[s, h:h + 1, :], axis=-1, keepdims=True))
        qc = jnp.concatenate(qc_rows, axis=0)
        qn = jnp.concatenate(qn_rows, axis=0)
        num = sv + w_inter * qc
        den = ssum + w_inter * qn
        hid = num / jnp.maximum(jnp.abs(den), jnp.exp(-m_t))
        mix_ref[:, _head_cols(h, GROUP_W)] = _head_rms_mxu(jax.nn.sigmoid(o) * hid, mlg_ref[:, _head_cols(h)], ones)

        packed = jnp.where(lane == 0, m_t, jnp.where(lane == 1, bcol, 0.0))
        lastv = jnp.dot(last_sel, packed, precision=lax.Precision.HIGHEST, preferred_element_type=F32)
        m_new = lastv[:, 0:1]
        b_last = lastv[:, 1:2]
        dec = jnp.exp(b_last + m_prev - m_new)
        kw = k * jnp.exp(b_last - bcol + icol - m_new)
        kwb = kw.astype(BF16)
        vb = v.astype(BF16)
        for s in range(SEQ_PER_TILE):
            rows = slice(s * DEC_SEQ, (s + 1) * DEC_SEQ)
            dec_s = dec[s * DEC_SEQ:s * DEC_SEQ + 1, :]
            c_ref[s, h] = dec_s * c0_ref[s, h] + lax.dot_general(kwb[rows], vb[rows], _TN,
                                                                preferred_element_type=F32)
            n_ref[s, h:h + 1, :] = dec_s * n0_ref[s, h:h + 1, :] + jnp.sum(kw[rows], axis=0, keepdims=True)
        mt_all = jnp.where(lane == h, m_t, mt_all)

    mt_ref[...] = mt_all


def _mix_sample(zz, gates, ws_t, bs_cols, cw, cb, gmg, mlg, prev, c0, n0, m0_tok):
    n = zz.shape[0]
    n_tiles = n // CHUNK
    row = lambda w: pl.BlockSpec((CHUNK, w), lambda i: (i, 0))
    const = lambda shape: pl.BlockSpec(shape, lambda i, _nd=len(shape): (0,) * _nd)
    c_spec = pl.BlockSpec((SEQ_PER_TILE, N_HEADS, HEAD_DIM, HEAD_DIM), lambda i: (i, 0, 0, 0))
    n_spec = pl.BlockSpec((SEQ_PER_TILE, N_HEADS, HEAD_DIM), lambda i: (i, 0, 0))
    return pl.pallas_call(
        _mix_sample_kernel,
        grid=(n_tiles,),
        in_specs=[row(N_MAIN), row(GATE_W), const(ws_t.shape), const(bs_cols.shape), const(cw.shape),
                  const(cb.shape), const(gmg.shape), const(mlg.shape), row(2 * GROUP_W),
                  c_spec, n_spec, row(N_HEADS)],
        out_specs=[row(D_MODEL), c_spec, n_spec, row(GATE_W)],
        out_shape=[jax.ShapeDtypeStruct((n, D_MODEL), F32),
                   jax.ShapeDtypeStruct(c0.shape, F32),
                   jax.ShapeDtypeStruct(n0.shape, F32),
                   jax.ShapeDtypeStruct((n, GATE_W), F32)],
        compiler_params=pltpu.CompilerParams(dimension_semantics=("arbitrary",),
                                             vmem_limit_bytes=VMEM_LIMIT_BYTES),
        name="mix_sample",
    )(zz, gates, ws_t, bs_cols, cw, cb, gmg, mlg, prev, c0, n0, m0_tok)


def _ffn_weights(wg, wu, wd):
    return wg.astype(BF16), wu.astype(BF16), wd.astype(BF16)


def _gate_columns(w):
    out = jnp.zeros(w.shape[:-1] + (GATE_W,), w.dtype)
    out = out.at[..., 0:N_HEADS].set(w[..., N_MAIN:N_MAIN + N_HEADS])
    return out.at[..., FG_LANE:FG_LANE + N_HEADS].set(w[..., N_MAIN + N_HEADS:N_MAIN + 2 * N_HEADS])


def kernel(x_prompt, x_sample, state_conv, state_C, state_n, state_m, ffn1_wg, ffn1_wu, ffn1_wd, ln1_g, ln1_b, w_in, b_in, gm_ln_g, gm_ln_b, gm_ws, gm_bs, conv_w, conv_b, gm_out_g, ml_out_g, w_out, ln2_g, ln2_b, ffn2_wg, ffn2_wu, ffn2_wd, ln3_g, ln3_b):
    depth = ffn1_wg.shape[0]
    bp, seq, _ = x_prompt.shape
    bs, dec_seq, _ = x_sample.shape
    assert dec_seq == DEC_SEQ and seq % TOKEN_TILE == 0
    assert (bs * dec_seq) % (TOKEN_TILE * STAGE_C_SUBTILES) == 0 and (bp * seq) % (TOKEN_TILE * STAGE_C_SUBTILES) == 0
    y_p = x_prompt.reshape(bp * seq, D_MODEL)
    y_s = x_sample.reshape(bs * dec_seq, D_MODEL)
    outs = []
    for l in range(depth):
        f1 = _ffn_weights(ffn1_wg[l], ffn1_wu[l], ffn1_wd[l])
        f2 = _ffn_weights(ffn2_wg[l], ffn2_wu[l], ffn2_wd[l])
        row = lambda a: a.reshape(1, -1)
        a_params = (*f1, row(ln1_g[l]), row(ln1_b[l]),
                    w_in[l].astype(BF16), row(b_in[l]),
                    _gate_columns(w_in[l]).astype(BF16), row(_gate_columns(b_in[l])),
                    row(gm_ln_g[l]), row(gm_ln_b[l]))
        c_params = (w_out[l].astype(BF16), row(ln2_g[l]), row(ln2_b[l]), *f2, row(ln3_g[l]), row(ln3_b[l]))
        mix_params = (conv_w[l], row(conv_b[l]), row(gm_out_g[l]), row(ml_out_g[l]))

        x1_p, mix_p, cn_p, m_p, gmv_p, ctail_p = _stage_a_mix_prompt(
            y_p, a_params, gm_ws[l], jnp.transpose(gm_bs[l]), *mix_params, bp, seq)
        x1_s, zz_s, g_s = _stage_a(y_s, *a_params)

        ws_t = jnp.tile(gm_ws[l][:, :DEC_SEQ, :DEC_SEQ], (1, SEQ_PER_TILE, SEQ_PER_TILE))
        bs_t = jnp.tile(jnp.transpose(gm_bs[l][:, :DEC_SEQ]), (SEQ_PER_TILE, 1))
        prev = jnp.pad(state_conv[l], ((0, 0), (DEC_SEQ - (CONV_W - 1), 0), (0, 0))).reshape(bs * DEC_SEQ, 2 * GROUP_W)
        m0_tok = jnp.repeat(state_m[l], DEC_SEQ, axis=0)
        mix_s, c_s, n_s, mt_s = _mix_sample(zz_s, g_s, ws_t, bs_t, *mix_params, prev, state_C[l], state_n[l], m0_tok)

        y_p = _stage_c(mix_p, x1_p, *c_params)
        y_s = _stage_c(mix_s, x1_s, *c_params)

        zz_s3 = zz_s.reshape(bs, dec_seq, N_MAIN)
        outs.append((
            gmv_p.reshape(bp, CHUNK, N_HEADS, HEAD_DIM),
            zz_s3[:, :, GROUP_W:2 * GROUP_W].reshape(bs, dec_seq, N_HEADS, HEAD_DIM),
            ctail_p[:, 8 - (CONV_W - 1):, :],
            zz_s3[:, dec_seq - (CONV_W - 1):, 2 * GROUP_W:4 * GROUP_W],
            cn_p[..., :HEAD_DIM], c_s, cn_p[..., HEAD_DIM], n_s,
            m_p[:, 0, :N_HEADS],
            mt_s.reshape(bs, dec_seq, GATE_W)[:, dec_seq - 1, :N_HEADS],
        ))
    stacked = [jnp.stack(a) for a in zip(*outs)]
    return (y_p.reshape(bp, seq, D_MODEL), y_s.reshape(bs, dec_seq, D_MODEL), *stacked)
```

```python
import functools

import jax
import jax.numpy as jnp
from jax import lax
from jax.experimental import pallas as pl
from jax.experimental.pallas import tpu as pltpu

F32 = jnp.float32
BF16 = jnp.bfloat16

D_MODEL = 1024
D_FF = 2816
HEAD_DIM = 128
N_HEADS = 4
GROUP_W = N_HEADS * HEAD_DIM
CHUNK = 128
CONV_W = 4
DEC_SEQ = 8
SEQ_PER_TILE = CHUNK // DEC_SEQ
N_MAIN = 6 * GROUP_W
GATE_W = 128
FG_LANE = 64
FF_CHUNK = 256
N_FF_CHUNKS = D_FF // FF_CHUNK
FF_GROUP = 4
ALPHA = 2.0 ** 0.25
LN_EPS = 1e-5
TOKEN_TILE = 512
STAGE_C_SUBTILES = 2
PAD_ROWS = 8
MIX_UNITS_PER_TILE = 6 * (TOKEN_TILE // CHUNK) + 3
MIX_SHARE_IN_FFN = (9, 10)
VMEM_LIMIT_BYTES = 60 * 1024 * 1024

_NT = (((1,), (1,)), ((), ()))
_TN = (((0,), (0,)), ((), ()))


def _layer_norm(x, g, b):
    mu = jnp.mean(x, axis=-1, keepdims=True)
    xc = x - mu
    var = jnp.mean(xc * xc, axis=-1, keepdims=True)
    return xc * lax.rsqrt(var + LN_EPS) * g + b


def _head_rms_mxu(x, g, ones):
    ss = jnp.dot((x * x).astype(BF16), ones, preferred_element_type=F32)
    return x * lax.rsqrt(ss * (1.0 / HEAD_DIM) + LN_EPS) * g


def _silu(x):
    return x * jax.nn.sigmoid(x)


def _log_sigmoid(x):
    return jnp.minimum(x, 0.0) - jnp.log1p(jnp.exp(-jnp.abs(x)))


def _swiglu(xb_ref, wg_ref, wu_ref, wd_ref, h_ref, between_chunks=None):
    acc = None
    for c in range(N_FF_CHUNKS):
        cols = slice(c * FF_CHUNK, (c + 1) * FF_CHUNK)
        xb = xb_ref[...]
        g = jnp.dot(xb, wg_ref[:, cols], preferred_element_type=F32)
        u = jnp.dot(xb, wu_ref[:, cols], preferred_element_type=F32)
        h_ref[:, cols] = (_silu(g) * u).astype(BF16)
        if (c + 1) % FF_GROUP == 0 or c == N_FF_CHUNKS - 1:
            grp = slice((c // FF_GROUP) * FF_GROUP * FF_CHUNK, (c + 1) * FF_CHUNK)
            d = jnp.dot(h_ref[:, grp], wd_ref[grp, :], preferred_element_type=F32)
            acc = d if acc is None else acc + d
        if between_chunks is not None:
            between_chunks(c)
    return acc


def _ffn1_and_project(x_ref, wg_ref, wu_ref, wd_ref, lng_ref, lnb_ref, wmain_ref, bmain_ref,
                      vlng_ref, vlnb_ref, x1_ref, xb_ref, h_ref, store_z, store_gate, between_chunks=None):
    x = x_ref[...]
    xb_ref[...] = x.astype(BF16)
    ffn = _swiglu(xb_ref, wg_ref, wu_ref, wd_ref, h_ref, between_chunks)
    x1 = _layer_norm(ALPHA * x + 0.5 * ffn, lng_ref[...], lnb_ref[...])
    x1_ref[...] = x1
    xb_ref[...] = x1.astype(BF16)
    for grp in range(N_MAIN // GROUP_W):
        cols = slice(grp * GROUP_W, (grp + 1) * GROUP_W)
        z = jnp.dot(xb_ref[...], wmain_ref[:, cols], preferred_element_type=F32) + bmain_ref[:, cols]
        if grp == 0:
            store_z(cols, jax.nn.gelu(z))
        elif grp == 1:
            z = jax.nn.gelu(z)
            for h in range(N_HEADS):
                hc = slice(h * HEAD_DIM, (h + 1) * HEAD_DIM)
                store_z(slice(GROUP_W + h * HEAD_DIM, GROUP_W + (h + 1) * HEAD_DIM),
                        _layer_norm(z[:, hc], vlng_ref[:, hc], vlnb_ref[:, hc]))
        else:
            store_z(cols, z)
    gate_cols = slice(N_MAIN, N_MAIN + GATE_W)
    store_gate(jnp.dot(xb_ref[...], wmain_ref[:, gate_cols], preferred_element_type=F32) + bmain_ref[:, gate_cols])


def _stage_a_kernel(*refs):
    *in_refs, x1_ref, zz_ref, gate_ref, xb_ref, h_ref = refs

    def store_z(cols, val):
        zz_ref[:, cols] = val

    def store_gate(val):
        gate_ref[...] = val

    _ffn1_and_project(*in_refs, x1_ref, xb_ref, h_ref, store_z, store_gate)


def _const_spec(shape):
    nd = len(shape)
    return pl.BlockSpec(shape, lambda i, _nd=nd: (0,) * _nd, pipeline_mode=pl.Buffered(1))


def _stage_a(x, wg, wu, wd, lng, lnb, wmain, bmain, vlng, vlnb):
    n = x.shape[0]
    tm = TOKEN_TILE
    row = lambda w: pl.BlockSpec((tm, w), lambda i: (i, 0))
    return pl.pallas_call(
        _stage_a_kernel,
        grid=(n // tm,),
        in_specs=[row(D_MODEL),
                  _const_spec(wg.shape), _const_spec(wu.shape), _const_spec(wd.shape),
                  _const_spec(lng.shape), _const_spec(lnb.shape),
                  _const_spec(wmain.shape), _const_spec(bmain.shape),
                  _const_spec(vlng.shape), _const_spec(vlnb.shape)],
        out_specs=[row(D_MODEL), row(N_MAIN), row(GATE_W)],
        out_shape=[jax.ShapeDtypeStruct((n, D_MODEL), F32),
                   jax.ShapeDtypeStruct((n, N_MAIN), F32),
                   jax.ShapeDtypeStruct((n, GATE_W), F32)],
        scratch_shapes=[pltpu.VMEM((tm, D_MODEL), BF16), pltpu.VMEM((tm, D_FF), BF16)],
        compiler_params=pltpu.CompilerParams(dimension_semantics=("arbitrary",),
                                             vmem_limit_bytes=VMEM_LIMIT_BYTES),
        name="stage_a",
    )(x, wg, wu, wd, lng, lnb, wmain, bmain, vlng, vlnb)


def _stage_c_kernel(mix_ref, x1_ref, wout_ref, ln2g_ref, ln2b_ref, wg_ref, wu_ref, wd_ref, ln3g_ref, ln3b_ref,
                    y_ref, xb_ref, h_ref):
    tiles = [slice(s * TOKEN_TILE, (s + 1) * TOKEN_TILE) for s in range(STAGE_C_SUBTILES)]
    for rows in tiles:
        proj = jnp.dot(mix_ref[rows, :].astype(BF16), wout_ref[...], preferred_element_type=F32)
        y = _layer_norm(ALPHA * x1_ref[rows, :] + proj, ln2g_ref[...], ln2b_ref[...])
        y_ref[rows, :] = y
        xb_ref[rows, :] = y.astype(BF16)
    for rows in tiles:
        ffn = _swiglu(xb_ref.at[rows, :], wg_ref, wu_ref, wd_ref, h_ref.at[rows, :])
        y_ref[rows, :] = _layer_norm(ALPHA * y_ref[rows, :] + 0.5 * ffn, ln3g_ref[...], ln3b_ref[...])


def _stage_c(mix, x1, wout, ln2g, ln2b, wg, wu, wd, ln3g, ln3b):
    n = x1.shape[0]
    tm = TOKEN_TILE * STAGE_C_SUBTILES
    row = lambda w: pl.BlockSpec((tm, w), lambda i: (i, 0))
    return pl.pallas_call(
        _stage_c_kernel,
        grid=(n // tm,),
        in_specs=[row(D_MODEL), row(D_MODEL),
                  _const_spec(wout.shape), _const_spec(ln2g.shape), _const_spec(ln2b.shape),
                  _const_spec(wg.shape), _const_spec(wu.shape), _const_spec(wd.shape),
                  _const_spec(ln3g.shape), _const_spec(ln3b.shape)],
        out_specs=row(D_MODEL),
        out_shape=jax.ShapeDtypeStruct((n, D_MODEL), F32),
        scratch_shapes=[pltpu.VMEM((tm, D_MODEL), BF16), pltpu.VMEM((tm, D_FF), BF16)],
        compiler_params=pltpu.CompilerParams(dimension_semantics=("arbitrary",),
                                             vmem_limit_bytes=VMEM_LIMIT_BYTES),
        name="stage_c",
    )(mix, x1, wout, ln2g, ln2b, wg, wu, wd, ln3g, ln3b)


def _head_cols(h, base=0):
    return slice(base + h * HEAD_DIM, base + (h + 1) * HEAD_DIM)


def _gmlp_heads(zz_ref, ws_ref, bs_ref, gmg_ref, mix_ref, mask, ones):
    for h in range(N_HEADS):
        u = zz_ref[:, _head_cols(h)]
        vn = zz_ref[:, _head_cols(h, GROUP_W)]
        w = jnp.where(mask, ws_ref[h], 0.0).astype(BF16)
        mixed = jnp.dot(w, vn.astype(BF16), preferred_element_type=F32) + bs_ref[:, h:h + 1]
        mix_ref[:, _head_cols(h)] = _head_rms_mxu(u * mixed, gmg_ref[:, _head_cols(h)], ones)


def _intra_chunk(q, k, v, bcol, brow, irow, mcol, mask):
    dlog = jnp.where(mask, bcol - brow + irow, -jnp.inf)
    inter = bcol + mcol
    m_t = jnp.maximum(inter, jnp.max(dlog, axis=-1, keepdims=True))
    w_intra = jnp.exp(dlog - m_t)
    w_inter = jnp.exp(inter - m_t)
    s = lax.dot_general(q.astype(BF16), k.astype(BF16), _NT, preferred_element_type=F32) * w_intra
    sv = jnp.dot(s.astype(BF16), v.astype(BF16), preferred_element_type=F32)
    ssum = jnp.sum(s, axis=-1, keepdims=True)
    return m_t, w_inter, sv, ssum


def _conv_taps(x, prev_fn, cw_ref, cb_ref, row_in_seq):
    acc = x * cw_ref[CONV_W - 1:CONV_W, :] + cb_ref[...]
    for j in range(1, CONV_W):
        shifted = jnp.where(row_in_seq < j, prev_fn(j), pltpu.roll(x, j, 0))
        acc = acc + shifted * cw_ref[CONV_W - 1 - j:CONV_W - j, :]
    return acc


def _mix_prompt_tile(zt, gt, ws_ref, bs_ref, cw_ref, cb_ref, gmg_ref, mlg_ref,
                     mix_ref, cn_ref, m_ref, gmv_ref, ctail_ref, tail_ref):
    row = lax.broadcasted_iota(jnp.int32, (CHUNK, CHUNK), 0)
    col = lax.broadcasted_iota(jnp.int32, (CHUNK, CHUNK), 1)
    causal = col <= row
    causal_f = causal.astype(F32)
    qk_cols = slice(2 * GROUP_W, 4 * GROUP_W)
    scale = HEAD_DIM ** -0.5

    heads = range(N_HEADS)
    chunks = range(TOKEN_TILE // CHUNK)
    rows = [slice(PAD_ROWS + c * CHUNK, PAD_ROWS + (c + 1) * CHUNK) for c in chunks]
    out_rows = [slice(c * CHUNK, (c + 1) * CHUNK) for c in chunks]

    zt[0:PAD_ROWS, qk_cols] = tail_ref[...]
    ones = jnp.ones((HEAD_DIM, HEAD_DIM), BF16)
    ones_col = jnp.ones((CHUNK, HEAD_DIM), BF16)

    w_masked = [jnp.where(causal, ws_ref[h], 0.0).astype(BF16) for h in heads]
    bias = [jnp.broadcast_to(bs_ref[:, h:h + 1], (CHUNK, HEAD_DIM)) for h in heads]
    mixed = {}
    for c in chunks:
        for h in heads:
            mixed[c, h] = jnp.dot(w_masked[h], zt[rows[c], _head_cols(h, GROUP_W)].astype(BF16),
                                  preferred_element_type=F32)
        yield
    for c in chunks:
        for h in heads:
            gm = zt[rows[c], _head_cols(h)] * (mixed[c, h] + bias[h])
            mix_ref[out_rows[c], _head_cols(h)] = _head_rms_mxu(gm, gmg_ref[:, _head_cols(h)], ones)
        yield

    qk = []
    for c in chunks:
        r0 = rows[c].start
        acc = zt[rows[c], qk_cols] * cw_ref[CONV_W - 1:CONV_W, :] + cb_ref[...]
        for j in range(1, CONV_W):
            acc = acc + zt[r0 - j:r0 - j + CHUNK, qk_cols] * cw_ref[CONV_W - 1 - j:CONV_W - j, :]
        qk.append(_silu(acc))
        yield

    gates, bcum, gates_t, bcum_t = [], [], [], []
    for c in chunks:
        g = gt[out_rows[c], :]
        b = jnp.dot(causal_f, _log_sigmoid(g), precision=lax.Precision.HIGHEST, preferred_element_type=F32)
        gates.append(g)
        bcum.append(b)
        gates_t.append(g.T)
        bcum_t.append(b.T)
    yield
    icol = {(c, h): gates[c][:, h:h + 1] for c in chunks for h in heads}
    bcol = {(c, h): bcum[c][:, FG_LANE + h:FG_LANE + h + 1] for c in chunks for h in heads}

    dlog, mx = {}, {}
    for c in chunks:
        for h in heads:
            irow = gates_t[c][h:h + 1, :]
            brow = bcum_t[c][FG_LANE + h:FG_LANE + h + 1, :]
            dlog[c, h] = jnp.where(causal, bcol[c, h] - brow + irow, -jnp.inf)
            mx[c, h] = jnp.max(dlog[c, h], axis=-1, keepdims=True)
        yield

    m_row = m_ref[0]
    lane = lax.broadcasted_iota(jnp.int32, m_row.shape, 1)
    m_t, w_inter, dec, wk = {}, {}, {}, {}
    for h in heads:
        m_prev = m_row[:, h:h + 1]
        for c in chunks:
            inter = bcol[c, h] + m_prev
            m_t[c, h] = jnp.maximum(inter, mx[c, h])
            w_inter[c, h] = jnp.exp(inter - m_t[c, h])
            m_new = m_t[c, h][CHUNK - 1:CHUNK, :]
            b_last = bcol[c, h][CHUNK - 1:CHUNK, :]
            dec[c, h] = jnp.exp(b_last + m_prev - m_new)
            wk[c, h] = jnp.exp(b_last - bcol[c, h] + icol[c, h] - m_new)
            m_prev = m_new
        m_row = jnp.where(lane == h, m_prev, m_row)
    m_ref[0] = m_row
    yield

    sv, upd = {}, {}
    for c in chunks:
        for h in heads:
            q = qk[c][:, _head_cols(h)]
            k = qk[c][:, _head_cols(h, GROUP_W)] * scale
            v1 = jnp.concatenate([zt[rows[c], _head_cols(h, 4 * GROUP_W)].astype(BF16), ones_col], axis=1)
            s = lax.dot_general(q.astype(BF16), k.astype(BF16), _NT, preferred_element_type=F32) * jnp.exp(
                dlog[c, h] - m_t[c, h])
            sv[c, h] = jnp.dot(s.astype(BF16), v1, preferred_element_type=F32)
            kw = k * wk[c, h]
            upd[c, h] = lax.dot_general(kw.astype(BF16), v1, _TN, preferred_element_type=F32)
        yield

    for h in heads:
        cn_state = cn_ref[0, h]
        for c in chunks:
            q = qk[c][:, _head_cols(h)]
            qcn = jnp.dot(q.astype(BF16), cn_state.astype(BF16), preferred_element_type=F32)
            nd = sv[c, h] + jnp.broadcast_to(w_inter[c, h], (CHUNK, 2 * HEAD_DIM)) * qcn
            num, den = nd[:, :HEAD_DIM], nd[:, HEAD_DIM:]
            hid = num / jnp.maximum(jnp.abs(den), jnp.exp(-m_t[c, h]))
            o = zt[rows[c], _head_cols(h, 5 * GROUP_W)]
            mix_ref[out_rows[c], _head_cols(h, GROUP_W)] = _head_rms_mxu(jax.nn.sigmoid(o) * hid,
                                                                        mlg_ref[:, _head_cols(h)], ones)
            cn_state = dec[c, h] * cn_state + upd[c, h]
        cn_ref[0, h] = cn_state
        yield

    last8 = slice(PAD_ROWS + TOKEN_TILE - 8, PAD_ROWS + TOKEN_TILE)
    tail_ref[...] = zt[last8, qk_cols]
    ctail_ref[0] = zt[last8, qk_cols]
    gmv_ref[0] = zt[PAD_ROWS + TOKEN_TILE - CHUNK:PAD_ROWS + TOKEN_TILE, GROUP_W:2 * GROUP_W]


def _fused_prompt_kernel(tiles_per_seq, *refs):
    (x_ref, wg_ref, wu_ref, wd_ref, lng_ref, lnb_ref, wmain_ref, bmain_ref, vlng_ref, vlnb_ref,
     ws_ref, bs_ref, cw_ref, cb_ref, gmg_ref, mlg_ref,
     x1_ref, mix_ref, cn_ref, m_ref, gmv_ref, ctail_ref,
     xb_ref, h_ref, zbuf_ref, gbuf_ref, tail_ref) = refs
    i = pl.program_id(0)
    slot_w = i % 2
    slot_r = 1 - slot_w

    @pl.when(i == 0)
    def _():
        zbuf_ref[1] = jnp.zeros(zbuf_ref.shape[1:], F32)
        gbuf_ref[1] = jnp.zeros(gbuf_ref.shape[1:], F32)

    @pl.when((i == 0) | (i % tiles_per_seq == 1))
    def _():
        cn_ref[...] = jnp.zeros_like(cn_ref)
        m_ref[...] = jnp.zeros_like(m_ref)
        tail_ref[...] = jnp.zeros_like(tail_ref)

    def store_z(cols, val):
        zbuf_ref[slot_w, PAD_ROWS:PAD_ROWS + TOKEN_TILE, cols] = val

    def store_gate(val):
        gbuf_ref[slot_w] = val

    mixer = _mix_prompt_tile(zbuf_ref.at[slot_r], gbuf_ref.at[slot_r], ws_ref, bs_ref, cw_ref, cb_ref, gmg_ref,
                             mlg_ref, mix_ref, cn_ref, m_ref, gmv_ref, ctail_ref, tail_ref)

    traced = [0]

    def mixer_units(c):
        target = (MIX_SHARE_IN_FFN[0] * (c + 1) * MIX_UNITS_PER_TILE) // (MIX_SHARE_IN_FFN[1] * N_FF_CHUNKS)
        while traced[0] < target:
            next(mixer)
            traced[0] += 1

    _ffn1_and_project(x_ref, wg_ref, wu_ref, wd_ref, lng_ref, lnb_ref, wmain_ref, bmain_ref,
                      vlng_ref, vlnb_ref, x1_ref, xb_ref, h_ref, store_z, store_gate, mixer_units)
    for _ in mixer:
        traced[0] += 1
    assert traced[0] == MIX_UNITS_PER_TILE - 1


def _stage_a_mix_prompt(x, a_params, ws, bs_cols, cw, cb, gmg, mlg, batch, seq):
    tm = TOKEN_TILE
    n_tiles = (batch * seq) // tm
    tiles_per_seq = seq // tm
    mix_params = (ws, bs_cols, cw, cb, gmg, mlg)
    tile_in = lambda i: (jnp.minimum(i, n_tiles - 1), 0)
    tile_out = lambda i: (jnp.maximum(i - 1, 0), 0)
    seq_of = lambda i: jnp.maximum(i - 1, 0) // tiles_per_seq
    return pl.pallas_call(
        functools.partial(_fused_prompt_kernel, tiles_per_seq),
        grid=(n_tiles + 1,),
        in_specs=[pl.BlockSpec((tm, D_MODEL), tile_in)] + [_const_spec(p.shape) for p in (*a_params, *mix_params)],
        out_specs=[pl.BlockSpec((tm, D_MODEL), tile_in),
                   pl.BlockSpec((tm, D_MODEL), tile_out),
                   pl.BlockSpec((1, N_HEADS, HEAD_DIM, 2 * HEAD_DIM), lambda i: (seq_of(i), 0, 0, 0)),
                   pl.BlockSpec((1, 1, GATE_W), lambda i: (seq_of(i), 0, 0)),
                   pl.BlockSpec((1, CHUNK, GROUP_W), lambda i: (seq_of(i), 0, 0)),
                   pl.BlockSpec((1, 8, 2 * GROUP_W), lambda i: (seq_of(i), 0, 0))],
        out_shape=[jax.ShapeDtypeStruct((batch * seq, D_MODEL), F32),
                   jax.ShapeDtypeStruct((batch * seq, D_MODEL), F32),
                   jax.ShapeDtypeStruct((batch, N_HEADS, HEAD_DIM, 2 * HEAD_DIM), F32),
                   jax.ShapeDtypeStruct((batch, 1, GATE_W), F32),
                   jax.ShapeDtypeStruct((batch, CHUNK, GROUP_W), F32),
                   jax.ShapeDtypeStruct((batch, 8, 2 * GROUP_W), F32)],
        scratch_shapes=[pltpu.VMEM((tm, D_MODEL), BF16),
                        pltpu.VMEM((tm, D_FF), BF16),
                        pltpu.VMEM((2, PAD_ROWS + tm, N_MAIN), F32),
                        pltpu.VMEM((2, tm, GATE_W), F32),
                        pltpu.VMEM((8, 2 * GROUP_W), F32)],
        compiler_params=pltpu.CompilerParams(dimension_semantics=("arbitrary",),
                                             vmem_limit_bytes=VMEM_LIMIT_BYTES),
        name="stage_a_mix_prompt",
    )(x, *a_params, *mix_params)


def _mix_sample_kernel(zz_ref, gate_ref, ws_ref, bs_ref, cw_ref, cb_ref, gmg_ref, mlg_ref, prev_ref,
                       c0_ref, n0_ref, m0_ref, mix_ref, c_ref, n_ref, mt_ref):
    row = lax.broadcasted_iota(jnp.int32, (CHUNK, CHUNK), 0)
    col = lax.broadcasted_iota(jnp.int32, (CHUNK, CHUNK), 1)
    seq_shift = DEC_SEQ.bit_length() - 1
    mask = (col <= row) & ((col >> seq_shift) == (row >> seq_shift))
    last_sel = (col == (row | (DEC_SEQ - 1))).astype(F32)

    ones = jnp.ones((HEAD_DIM, HEAD_DIM), BF16)
    _gmlp_heads(zz_ref, ws_ref, bs_ref, gmg_ref, mix_ref, mask, ones)

    x = zz_ref[:, 2 * GROUP_W:4 * GROUP_W]
    row_w = lax.broadcasted_iota(jnp.int32, (CHUNK, 2 * GROUP_W), 0) & (DEC_SEQ - 1)
    prev = prev_ref[...]
    qk = _silu(_conv_taps(x, lambda j: pltpu.roll(prev, CHUNK - DEC_SEQ + j, 0), cw_ref, cb_ref, row_w))

    gates = gate_ref[...]
    bcum = jnp.dot(mask.astype(F32), _log_sigmoid(gates), precision=lax.Precision.HIGHEST,
                   preferred_element_type=F32)
    gates_t = gates.T
    bcum_t = bcum.T
    lane = lax.broadcasted_iota(jnp.int32, (CHUNK, GATE_W), 1)
    mt_all = jnp.zeros((CHUNK, GATE_W), F32)
    scale = HEAD_DIM ** -0.5

    for h in range(N_HEADS):
        q = qk[:, _head_cols(h)]
        k = qk[:, _head_cols(h, GROUP_W)] * scale
        v = zz_ref[:, _head_cols(h, 4 * GROUP_W)]
        o = zz_ref[:, _head_cols(h, 5 * GROUP_W)]
        icol = gates[:, h:h + 1]
        bcol = bcum[:, FG_LANE + h:FG_LANE + h + 1]
        irow = gates_t[h:h + 1, :]
        brow = bcum_t[FG_LANE + h:FG_LANE + h + 1, :]
        m_prev = m0_ref[:, h:h + 1]

        m_t, w_inter, sv, ssum = _intra_chunk(q, k, v, bcol, brow, irow, m_prev, mask)
        qb = q.astype(BF16)
        qc_rows, qn_rows = [], []
        for s in range(SEQ_PER_TILE):
            rows = slice(s * DEC_SEQ, (s + 1) * DEC_SEQ)
            qc_rows.append(jnp.dot(qb[rows], c0_ref[s, h].astype(BF16), preferred_element_type=F32))
            qn_rows.append(jnp.sum(q[rows] * n0_ref[s, h:h + 1, :], axis=-1, keepdims=True))
        qc = jnp.concatenate(qc_rows, axis=0)
        qn = jnp.concatenate(qn_rows, axis=0)
        num = sv + w_inter * qc
        den = ssum + w_inter * qn
        hid = num / jnp.maximum(jnp.abs(den), jnp.exp(-m_t))
        mix_ref[:, _head_cols(h, GROUP_W)] = _head_rms_mxu(jax.nn.sigmoid(o) * hid, mlg_ref[:, _head_cols(h)], ones)

        packed = jnp.where(lane == 0, m_t, jnp.where(lane == 1, bcol, 0.0))
        lastv = jnp.dot(last_sel, packed, precision=lax.Precision.HIGHEST, preferred_element_type=F32)
        m_new = lastv[:, 0:1]
        b_last = lastv[:, 1:2]
        dec = jnp.exp(b_last + m_prev - m_new)
        kw = k * jnp.exp(b_last - bcol + icol - m_new)
        kwb = kw.astype(BF16)
        vb = v.astype(BF16)
        for s in range(SEQ_PER_TILE):
            rows = slice(s * DEC_SEQ, (s + 1) * DEC_SEQ)
            dec_s = dec[s * DEC_SEQ:s * DEC_SEQ + 1, :]
            c_ref[s, h] = dec_s * c0_ref[s, h] + lax.dot_general(kwb[rows], vb[rows], _TN,
                                                                preferred_element_type=F32)
            n_ref[s, h:h + 1, :] = dec_s * n0_ref[s, h:h + 1, :] + jnp.sum(kw[rows], axis=0, keepdims=True)
        mt_all = jnp.where(lane == h, m_t, mt_all)

    mt_ref[...] = mt_all


def _mix_sample(zz, gates, ws_t, bs_cols, cw, cb, gmg, mlg, prev, c0, n0, m0_tok):
    n = zz.shape[0]
    n_tiles = n // CHUNK
    row = lambda w: pl.BlockSpec((CHUNK, w), lambda i: (i, 0))
    const = lambda shape: pl.BlockSpec(shape, lambda i, _nd=len(shape): (0,) * _nd)
    c_spec = pl.BlockSpec((SEQ_PER_TILE, N_HEADS, HEAD_DIM, HEAD_DIM), lambda i: (i, 0, 0, 0))
    n_spec = pl.BlockSpec((SEQ_PER_TILE, N_HEADS, HEAD_DIM), lambda i: (i, 0, 0))
    return pl.pallas_call(
        _mix_sample_kernel,
        grid=(n_tiles,),
        in_specs=[row(N_MAIN), row(GATE_W), const(ws_t.shape), const(bs_cols.shape), const(cw.shape),
                  const(cb.shape), const(gmg.shape), const(mlg.shape), row(2 * GROUP_W),
                  c_spec, n_spec, row(N_HEADS)],
        out_specs=[row(D_MODEL), c_spec, n_spec, row(GATE_W)],
        out_shape=[jax.ShapeDtypeStruct((n, D_MODEL), F32),
                   jax.ShapeDtypeStruct(c0.shape, F32),
                   jax.ShapeDtypeStruct(n0.shape, F32),
                   jax.ShapeDtypeStruct((n, GATE_W), F32)],
        compiler_params=pltpu.CompilerParams(dimension_semantics=("arbitrary",),
                                             vmem_limit_bytes=VMEM_LIMIT_BYTES),
        name="mix_sample",
    )(zz, gates, ws_t, bs_cols, cw, cb, gmg, mlg, prev, c0, n0, m0_tok)


def _ffn_weights(wg, wu, wd):
    return wg.astype(BF16), wu.astype(BF16), wd.astype(BF16)


def _projection_columns(w):
    zeros = jnp.zeros(w.shape[:-1] + (FG_LANE - N_HEADS,), w.dtype)
    return jnp.concatenate([w[..., :N_MAIN], w[..., N_MAIN:N_MAIN + N_HEADS], zeros,
                            w[..., N_MAIN + N_HEADS:N_MAIN + 2 * N_HEADS], zeros], axis=-1)


def kernel(x_prompt, x_sample, state_conv, state_C, state_n, state_m, ffn1_wg, ffn1_wu, ffn1_wd, ln1_g, ln1_b, w_in, b_in, gm_ln_g, gm_ln_b, gm_ws, gm_bs, conv_w, conv_b, gm_out_g, ml_out_g, w_out, ln2_g, ln2_b, ffn2_wg, ffn2_wu, ffn2_wd, ln3_g, ln3_b):
    depth = ffn1_wg.shape[0]
    bp, seq, _ = x_prompt.shape
    bs, dec_seq, _ = x_sample.shape
    assert dec_seq == DEC_SEQ and seq % TOKEN_TILE == 0
    assert (bs * dec_seq) % (TOKEN_TILE * STAGE_C_SUBTILES) == 0 and (bp * seq) % (TOKEN_TILE * STAGE_C_SUBTILES) == 0
    y_p = x_prompt.reshape(bp * seq, D_MODEL)
    y_s = x_sample.reshape(bs * dec_seq, D_MODEL)
    outs = []
    for l in range(depth):
        f1 = _ffn_weights(ffn1_wg[l], ffn1_wu[l], ffn1_wd[l])
        f2 = _ffn_weights(ffn2_wg[l], ffn2_wu[l], ffn2_wd[l])
        row = lambda a: a.reshape(1, -1)
        a_params = (*f1, row(ln1_g[l]), row(ln1_b[l]),
                    _projection_columns(w_in[l]).astype(BF16), row(_projection_columns(b_in[l])),
                    row(gm_ln_g[l]), row(gm_ln_b[l]))
        c_params = (w_out[l].astype(BF16), row(ln2_g[l]), row(ln2_b[l]), *f2, row(ln3_g[l]), row(ln3_b[l]))
        mix_params = (conv_w[l], row(conv_b[l]), row(gm_out_g[l]), row(ml_out_g[l]))

        x1_p, mix_p, cn_p, m_p, gmv_p, ctail_p = _stage_a_mix_prompt(
            y_p, a_params, gm_ws[l], jnp.transpose(gm_bs[l]), *mix_params, bp, seq)
        x1_s, zz_s, g_s = _stage_a(y_s, *a_params)

        ws_t = jnp.tile(gm_ws[l][:, :DEC_SEQ, :DEC_SEQ], (1, SEQ_PER_TILE, SEQ_PER_TILE))
        bs_t = jnp.tile(jnp.transpose(gm_bs[l][:, :DEC_SEQ]), (SEQ_PER_TILE, 1))
        prev = jnp.pad(state_conv[l], ((0, 0), (DEC_SEQ - (CONV_W - 1), 0), (0, 0))).reshape(bs * DEC_SEQ, 2 * GROUP_W)
        m0_tok = jnp.repeat(state_m[l], DEC_SEQ, axis=0)
        mix_s, c_s, n_s, mt_s = _mix_sample(zz_s, g_s, ws_t, bs_t, *mix_params, prev, state_C[l], state_n[l], m0_tok)

        y_p = _stage_c(mix_p, x1_p, *c_params)
        y_s = _stage_c(mix_s, x1_s, *c_params)

        zz_s3 = zz_s.reshape(bs, dec_seq, N_MAIN)
        outs.append((
            gmv_p.reshape(bp, CHUNK, N_HEADS, HEAD_DIM),
            zz_s3[:, :, GROUP_W:2 * GROUP_W].reshape(bs, dec_seq, N_HEADS, HEAD_DIM),
            ctail_p[:, 8 - (CONV_W - 1):, :],
            zz_s3[:, dec_seq - (CONV_W - 1):, 2 * GROUP_W:4 * GROUP_W],
            cn_p[..., :HEAD_DIM], c_s, cn_p[..., HEAD_DIM], n_s,
            m_p[:, 0, :N_HEADS],
            mt_s.reshape(bs, dec_seq, GATE_W)[:, dec_seq - 1, :N_HEADS],
        ))
    stacked = [jnp.stack(a) for a in zip(*outs)]
    return (y_p.reshape(bp, seq, D_MODEL), y_s.reshape(bs, dec_seq, D_MODEL), *stacked)
```

```python
import functools

import jax
import jax.numpy as jnp
from jax import lax
from jax.experimental import pallas as pl
from jax.experimental.pallas import tpu as pltpu

F32 = jnp.float32
BF16 = jnp.bfloat16

D_MODEL = 1024
D_FF = 2816
HEAD_DIM = 128
N_HEADS = 4
GROUP_W = N_HEADS * HEAD_DIM
CHUNK = 128
CONV_W = 4
DEC_SEQ = 8
SEQ_PER_TILE = CHUNK // DEC_SEQ
N_MAIN = 6 * GROUP_W
GATE_W = 128
FG_LANE = 64
FF_CHUNK = 256
N_FF_CHUNKS = D_FF // FF_CHUNK
FF_GROUP = 4
ALPHA = 2.0 ** 0.25
LN_EPS = 1e-5
TOKEN_TILE = 512
STAGE_C_SUBTILES = 2
PAD_ROWS = 8
MIX_UNITS_PER_TILE = 6 * (TOKEN_TILE // CHUNK) + 3
MIX_SHARE_IN_FFN = (9, 10)
VMEM_LIMIT_BYTES = 60 * 1024 * 1024

_NT = (((1,), (1,)), ((), ()))
_TN = (((0,), (0,)), ((), ()))


def _layer_norm(x, g, b):
    mu = jnp.mean(x, axis=-1, keepdims=True)
    xc = x - mu
    var = jnp.mean(xc * xc, axis=-1, keepdims=True)
    return xc * lax.rsqrt(var + LN_EPS) * g + b


def _head_rms_mxu(x, g, ones):
    ss = jnp.dot((x * x).astype(BF16), ones, preferred_element_type=F32)
    return x * lax.rsqrt(ss * (1.0 / HEAD_DIM) + LN_EPS) * g


def _silu(x):
    return x * jax.nn.sigmoid(x)


def _log_sigmoid(x):
    return jnp.minimum(x, 0.0) - jnp.log1p(jnp.exp(-jnp.abs(x)))


def _swiglu(xb_ref, wg_ref, wu_ref, wd_ref, h_ref, between_chunks=None):
    acc = None
    for c in range(N_FF_CHUNKS):
        cols = slice(c * FF_CHUNK, (c + 1) * FF_CHUNK)
        xb = xb_ref[...]
        g = jnp.dot(xb, wg_ref[:, cols], preferred_element_type=F32)
        u = jnp.dot(xb, wu_ref[:, cols], preferred_element_type=F32)
        h_ref[:, cols] = (_silu(g) * u).astype(BF16)
        if (c + 1) % FF_GROUP == 0 or c == N_FF_CHUNKS - 1:
            grp = slice((c // FF_GROUP) * FF_GROUP * FF_CHUNK, (c + 1) * FF_CHUNK)
            d = jnp.dot(h_ref[:, grp], wd_ref[grp, :], preferred_element_type=F32)
            acc = d if acc is None else acc + d
        if between_chunks is not None:
            between_chunks(c)
    return acc


def _ffn1_and_project(x_ref, wg_ref, wu_ref, wd_ref, lng_ref, lnb_ref, wmain_ref, bmain_ref, wgate_ref, bgate_ref,
                      vlng_ref, vlnb_ref, x1_ref, xb_ref, h_ref, store_z, store_gate, between_chunks=None):
    x = x_ref[...]
    xb_ref[...] = x.astype(BF16)
    ffn = _swiglu(xb_ref, wg_ref, wu_ref, wd_ref, h_ref, between_chunks)
    x1 = _layer_norm(ALPHA * x + 0.5 * ffn, lng_ref[...], lnb_ref[...])
    x1_ref[...] = x1
    xb_ref[...] = x1.astype(BF16)
    for grp in range(N_MAIN // GROUP_W):
        cols = slice(grp * GROUP_W, (grp + 1) * GROUP_W)
        z = jnp.dot(xb_ref[...], wmain_ref[:, cols], preferred_element_type=F32) + bmain_ref[:, cols]
        if grp == 0:
            store_z(cols, jax.nn.gelu(z))
        elif grp == 1:
            z = jax.nn.gelu(z)
            for h in range(N_HEADS):
                hc = slice(h * HEAD_DIM, (h + 1) * HEAD_DIM)
                store_z(slice(GROUP_W + h * HEAD_DIM, GROUP_W + (h + 1) * HEAD_DIM),
                        _layer_norm(z[:, hc], vlng_ref[:, hc], vlnb_ref[:, hc]))
        else:
            store_z(cols, z)
    store_gate(jnp.dot(xb_ref[...], wgate_ref[...], preferred_element_type=F32) + bgate_ref[...])


def _stage_a_kernel(*refs):
    *in_refs, x1_ref, zz_ref, gate_ref, xb_ref, h_ref = refs

    def store_z(cols, val):
        zz_ref[:, cols] = val

    def store_gate(val):
        gate_ref[...] = val

    _ffn1_and_project(*in_refs, x1_ref, xb_ref, h_ref, store_z, store_gate)


def _const_spec(shape):
    nd = len(shape)
    return pl.BlockSpec(shape, lambda i, _nd=nd: (0,) * _nd, pipeline_mode=pl.Buffered(1))


def _stage_a(x, wg, wu, wd, lng, lnb, wmain, bmain, wgate, bgate, vlng, vlnb):
    n = x.shape[0]
    tm = TOKEN_TILE
    row = lambda w: pl.BlockSpec((tm, w), lambda i: (i, 0))
    return pl.pallas_call(
        _stage_a_kernel,
        grid=(n // tm,),
        in_specs=[row(D_MODEL),
                  _const_spec(wg.shape), _const_spec(wu.shape), _const_spec(wd.shape),
                  _const_spec(lng.shape), _const_spec(lnb.shape),
                  _const_spec(wmain.shape), _const_spec(bmain.shape),
                  _const_spec(wgate.shape), _const_spec(bgate.shape),
                  _const_spec(vlng.shape), _const_spec(vlnb.shape)],
        out_specs=[row(D_MODEL), row(N_MAIN), row(GATE_W)],
        out_shape=[jax.ShapeDtypeStruct((n, D_MODEL), F32),
                   jax.ShapeDtypeStruct((n, N_MAIN), F32),
                   jax.ShapeDtypeStruct((n, GATE_W), F32)],
        scratch_shapes=[pltpu.VMEM((tm, D_MODEL), BF16), pltpu.VMEM((tm, D_FF), BF16)],
        compiler_params=pltpu.CompilerParams(dimension_semantics=("arbitrary",),
                                             vmem_limit_bytes=VMEM_LIMIT_BYTES),
        name="stage_a",
    )(x, wg, wu, wd, lng, lnb, wmain, bmain, wgate, bgate, vlng, vlnb)


def _stage_c_kernel(mix_ref, x1_ref, wout_ref, ln2g_ref, ln2b_ref, wg_ref, wu_ref, wd_ref, ln3g_ref, ln3b_ref,
                    y_ref, xb_ref, h_ref):
    tiles = [slice(s * TOKEN_TILE, (s + 1) * TOKEN_TILE) for s in range(STAGE_C_SUBTILES)]
    for rows in tiles:
        proj = jnp.dot(mix_ref[rows, :].astype(BF16), wout_ref[...], preferred_element_type=F32)
        y = _layer_norm(ALPHA * x1_ref[rows, :] + proj, ln2g_ref[...], ln2b_ref[...])
        y_ref[rows, :] = y
        xb_ref[rows, :] = y.astype(BF16)
    for rows in tiles:
        ffn = _swiglu(xb_ref.at[rows, :], wg_ref, wu_ref, wd_ref, h_ref.at[rows, :])
        y_ref[rows, :] = _layer_norm(ALPHA * y_ref[rows, :] + 0.5 * ffn, ln3g_ref[...], ln3b_ref[...])


def _stage_c(mix, x1, wout, ln2g, ln2b, wg, wu, wd, ln3g, ln3b):
    n = x1.shape[0]
    tm = TOKEN_TILE * STAGE_C_SUBTILES
    row = lambda w: pl.BlockSpec((tm, w), lambda i: (i, 0))
    return pl.pallas_call(
        _stage_c_kernel,
        grid=(n // tm,),
        in_specs=[row(D_MODEL), row(D_MODEL),
                  _const_spec(wout.shape), _const_spec(ln2g.shape), _const_spec(ln2b.shape),
                  _const_spec(wg.shape), _const_spec(wu.shape), _const_spec(wd.shape),
                  _const_spec(ln3g.shape), _const_spec(ln3b.shape)],
        out_specs=row(D_MODEL),
        out_shape=jax.ShapeDtypeStruct((n, D_MODEL), F32),
        scratch_shapes=[pltpu.VMEM((tm, D_MODEL), BF16), pltpu.VMEM((tm, D_FF), BF16)],
        compiler_params=pltpu.CompilerParams(dimension_semantics=("arbitrary",),
                                             vmem_limit_bytes=VMEM_LIMIT_BYTES),
        name="stage_c",
    )(mix, x1, wout, ln2g, ln2b, wg, wu, wd, ln3g, ln3b)


def _head_cols(h, base=0):
    return slice(base + h * HEAD_DIM, base + (h + 1) * HEAD_DIM)


def _gmlp_heads(zz_ref, ws_ref, bs_ref, gmg_ref, mix_ref, mask, ones):
    for h in range(N_HEADS):
        u = zz_ref[:, _head_cols(h)]
        vn = zz_ref[:, _head_cols(h, GROUP_W)]
        w = jnp.where(mask, ws_ref[h], 0.0).astype(BF16)
        mixed = jnp.dot(w, vn.astype(BF16), preferred_element_type=F32) + bs_ref[:, h:h + 1]
        mix_ref[:, _head_cols(h)] = _head_rms_mxu(u * mixed, gmg_ref[:, _head_cols(h)], ones)


def _intra_chunk(q, k, v, bcol, brow, irow, mcol, mask):
    dlog = jnp.where(mask, bcol - brow + irow, -jnp.inf)
    inter = bcol + mcol
    m_t = jnp.maximum(inter, jnp.max(dlog, axis=-1, keepdims=True))
    w_intra = jnp.exp(dlog - m_t)
    w_inter = jnp.exp(inter - m_t)
    s = lax.dot_general(q.astype(BF16), k.astype(BF16), _NT, preferred_element_type=F32) * w_intra
    sv = jnp.dot(s.astype(BF16), v.astype(BF16), preferred_element_type=F32)
    ssum = jnp.sum(s, axis=-1, keepdims=True)
    return m_t, w_inter, sv, ssum


def _conv_taps(x, prev_fn, cw_ref, cb_ref, row_in_seq):
    acc = x * cw_ref[CONV_W - 1:CONV_W, :] + cb_ref[...]
    for j in range(1, CONV_W):
        shifted = jnp.where(row_in_seq < j, prev_fn(j), pltpu.roll(x, j, 0))
        acc = acc + shifted * cw_ref[CONV_W - 1 - j:CONV_W - j, :]
    return acc


def _mix_prompt_tile(zt, gt, ws_ref, bs_ref, cw_ref, cb_ref, gmg_ref, mlg_ref,
                     mix_ref, cn_ref, m_ref, gmv_ref, ctail_ref, tail_ref):
    row = lax.broadcasted_iota(jnp.int32, (CHUNK, CHUNK), 0)
    col = lax.broadcasted_iota(jnp.int32, (CHUNK, CHUNK), 1)
    causal = col <= row
    causal_f = causal.astype(F32)
    qk_cols = slice(2 * GROUP_W, 4 * GROUP_W)
    scale = HEAD_DIM ** -0.5

    heads = range(N_HEADS)
    chunks = range(TOKEN_TILE // CHUNK)
    rows = [slice(PAD_ROWS + c * CHUNK, PAD_ROWS + (c + 1) * CHUNK) for c in chunks]
    out_rows = [slice(c * CHUNK, (c + 1) * CHUNK) for c in chunks]

    zt[0:PAD_ROWS, qk_cols] = tail_ref[...]
    ones = jnp.ones((HEAD_DIM, HEAD_DIM), BF16)
    ones_col = jnp.ones((CHUNK, HEAD_DIM), BF16)

    w_masked = [jnp.where(causal, ws_ref[h], 0.0).astype(BF16) for h in heads]
    bias = [jnp.broadcast_to(bs_ref[:, h:h + 1], (CHUNK, HEAD_DIM)) for h in heads]
    mixed = {}
    for c in chunks:
        for h in heads:
            mixed[c, h] = jnp.dot(w_masked[h], zt[rows[c], _head_cols(h, GROUP_W)].astype(BF16),
                                  preferred_element_type=F32)
        yield
    for c in chunks:
        for h in heads:
            gm = zt[rows[c], _head_cols(h)] * (mixed[c, h] + bias[h])
            mix_ref[out_rows[c], _head_cols(h)] = _head_rms_mxu(gm, gmg_ref[:, _head_cols(h)], ones)
        yield

    qk = []
    for c in chunks:
        r0 = rows[c].start
        acc = zt[rows[c], qk_cols] * cw_ref[CONV_W - 1:CONV_W, :] + cb_ref[...]
        for j in range(1, CONV_W):
            acc = acc + zt[r0 - j:r0 - j + CHUNK, qk_cols] * cw_ref[CONV_W - 1 - j:CONV_W - j, :]
        qk.append(_silu(acc))
        yield

    gates, bcum, gates_t, bcum_t = [], [], [], []
    for c in chunks:
        g = gt[out_rows[c], :]
        b = jnp.dot(causal_f, _log_sigmoid(g), precision=lax.Precision.HIGHEST, preferred_element_type=F32)
        gates.append(g)
        bcum.append(b)
        gates_t.append(g.T)
        bcum_t.append(b.T)
    yield
    icol = {(c, h): gates[c][:, h:h + 1] for c in chunks for h in heads}
    bcol = {(c, h): bcum[c][:, FG_LANE + h:FG_LANE + h + 1] for c in chunks for h in heads}

    dlog, mx = {}, {}
    for c in chunks:
        for h in heads:
            irow = gates_t[c][h:h + 1, :]
            brow = bcum_t[c][FG_LANE + h:FG_LANE + h + 1, :]
            dlog[c, h] = jnp.where(causal, bcol[c, h] - brow + irow, -jnp.inf)
            mx[c, h] = jnp.max(dlog[c, h], axis=-1, keepdims=True)
        yield

    m_row = m_ref[0]
    lane = lax.broadcasted_iota(jnp.int32, m_row.shape, 1)
    m_t, w_inter, dec, wk = {}, {}, {}, {}
    for h in heads:
        m_prev = m_row[:, h:h + 1]
        for c in chunks:
            inter = bcol[c, h] + m_prev
            m_t[c, h] = jnp.maximum(inter, mx[c, h])
            w_inter[c, h] = jnp.exp(inter - m_t[c, h])
            m_new = m_t[c, h][CHUNK - 1:CHUNK, :]
            b_last = bcol[c, h][CHUNK - 1:CHUNK, :]
            dec[c, h] = jnp.exp(b_last + m_prev - m_new)
            wk[c, h] = jnp.exp(b_last - bcol[c, h] + icol[c, h] - m_new)
            m_prev = m_new
        m_row = jnp.where(lane == h, m_prev, m_row)
    m_ref[0] = m_row
    yield

    sv, upd = {}, {}
    for c in chunks:
        for h in heads:
            q = qk[c][:, _head_cols(h)]
            k = qk[c][:, _head_cols(h, GROUP_W)] * scale
            v1 = jnp.concatenate([zt[rows[c], _head_cols(h, 4 * GROUP_W)].astype(BF16), ones_col], axis=1)
            s = lax.dot_general(q.astype(BF16), k.astype(BF16), _NT, preferred_element_type=F32) * jnp.exp(
                dlog[c, h] - m_t[c, h])
            sv[c, h] = jnp.dot(s.astype(BF16), v1, preferred_element_type=F32)
            kw = k * wk[c, h]
            upd[c, h] = lax.dot_general(kw.astype(BF16), v1, _TN, preferred_element_type=F32)
        yield

    for h in heads:
        cn_state = cn_ref[0, h]
        for c in chunks:
            q = qk[c][:, _head_cols(h)]
            qcn = jnp.dot(q.astype(BF16), cn_state.astype(BF16), preferred_element_type=F32)
            nd = sv[c, h] + jnp.broadcast_to(w_inter[c, h], (CHUNK, 2 * HEAD_DIM)) * qcn
            num, den = nd[:, :HEAD_DIM], nd[:, HEAD_DIM:]
            hid = num / jnp.maximum(jnp.abs(den), jnp.exp(-m_t[c, h]))
            o = zt[rows[c], _head_cols(h, 5 * GROUP_W)]
            mix_ref[out_rows[c], _head_cols(h, GROUP_W)] = _head_rms_mxu(jax.nn.sigmoid(o) * hid,
                                                                        mlg_ref[:, _head_cols(h)], ones)
            cn_state = dec[c, h] * cn_state + upd[c, h]
        cn_ref[0, h] = cn_state
        yield

    last8 = slice(PAD_ROWS + TOKEN_TILE - 8, PAD_ROWS + TOKEN_TILE)
    tail_ref[...] = zt[last8, qk_cols]
    ctail_ref[0] = zt[last8, qk_cols]
    gmv_ref[0] = zt[PAD_ROWS + TOKEN_TILE - CHUNK:PAD_ROWS + TOKEN_TILE, GROUP_W:2 * GROUP_W]


def _fused_prompt_kernel(tiles_per_seq, *refs):
    (x_ref, wg_ref, wu_ref, wd_ref, lng_ref, lnb_ref, wmain_ref, bmain_ref, wgate_ref, bgate_ref, vlng_ref, vlnb_ref,
     ws_ref, bs_ref, cw_ref, cb_ref, gmg_ref, mlg_ref,
     x1_ref, mix_ref, cn_ref, m_ref, gmv_ref, ctail_ref,
     xb_ref, h_ref, zbuf_ref, gbuf_ref, tail_ref) = refs
    i = pl.program_id(0)
    slot_w = i % 2
    slot_r = 1 - slot_w

    @pl.when(i == 0)
    def _():
        zbuf_ref[1] = jnp.zeros(zbuf_ref.shape[1:], F32)
        gbuf_ref[1] = jnp.zeros(gbuf_ref.shape[1:], F32)

    @pl.when((i == 0) | (i % tiles_per_seq == 1))
    def _():
        cn_ref[...] = jnp.zeros_like(cn_ref)
        m_ref[...] = jnp.zeros_like(m_ref)
        tail_ref[...] = jnp.zeros_like(tail_ref)

    def store_z(cols, val):
        zbuf_ref[slot_w, PAD_ROWS:PAD_ROWS + TOKEN_TILE, cols] = val

    def store_gate(val):
        gbuf_ref[slot_w] = val

    mixer = _mix_prompt_tile(zbuf_ref.at[slot_r], gbuf_ref.at[slot_r], ws_ref, bs_ref, cw_ref, cb_ref, gmg_ref,
                             mlg_ref, mix_ref, cn_ref, m_ref, gmv_ref, ctail_ref, tail_ref)

    traced = [0]

    def mixer_units(c):
        target = (MIX_SHARE_IN_FFN[0] * (c + 1) * MIX_UNITS_PER_TILE) // (MIX_SHARE_IN_FFN[1] * N_FF_CHUNKS)
        while traced[0] < target:
            next(mixer)
            traced[0] += 1

    _ffn1_and_project(x_ref, wg_ref, wu_ref, wd_ref, lng_ref, lnb_ref, wmain_ref, bmain_ref, wgate_ref, bgate_ref,
                      vlng_ref, vlnb_ref, x1_ref, xb_ref, h_ref, store_z, store_gate, mixer_units)
    for _ in mixer:
        traced[0] += 1
    assert traced[0] == MIX_UNITS_PER_TILE - 1


def _stage_a_mix_prompt(x, a_params, ws, bs_cols, cw, cb, gmg, mlg, batch, seq):
    tm = TOKEN_TILE
    n_tiles = (batch * seq) // tm
    tiles_per_seq = seq // tm
    mix_params = (ws, bs_cols, cw, cb, gmg, mlg)
    tile_in = lambda i: (jnp.minimum(i, n_tiles - 1), 0)
    tile_out = lambda i: (jnp.maximum(i - 1, 0), 0)
    seq_of = lambda i: jnp.maximum(i - 1, 0) // tiles_per_seq
    return pl.pallas_call(
        functools.partial(_fused_prompt_kernel, tiles_per_seq),
        grid=(n_tiles + 1,),
        in_specs=[pl.BlockSpec((tm, D_MODEL), tile_in)] + [_const_spec(p.shape) for p in (*a_params, *mix_params)],
        out_specs=[pl.BlockSpec((tm, D_MODEL), tile_in),
                   pl.BlockSpec((tm, D_MODEL), tile_out),
                   pl.BlockSpec((1, N_HEADS, HEAD_DIM, 2 * HEAD_DIM), lambda i: (seq_of(i), 0, 0, 0)),
                   pl.BlockSpec((1, 1, GATE_W), lambda i: (seq_of(i), 0, 0)),
                   pl.BlockSpec((1, CHUNK, GROUP_W), lambda i: (seq_of(i), 0, 0)),
                   pl.BlockSpec((1, 8, 2 * GROUP_W), lambda i: (seq_of(i), 0, 0))],
        out_shape=[jax.ShapeDtypeStruct((batch * seq, D_MODEL), F32),
                   jax.ShapeDtypeStruct((batch * seq, D_MODEL), F32),
                   jax.ShapeDtypeStruct((batch, N_HEADS, HEAD_DIM, 2 * HEAD_DIM), F32),
                   jax.ShapeDtypeStruct((batch, 1, GATE_W), F32),
                   jax.ShapeDtypeStruct((batch, CHUNK, GROUP_W), F32),
                   jax.ShapeDtypeStruct((batch, 8, 2 * GROUP_W), F32)],
        scratch_shapes=[pltpu.VMEM((tm, D_MODEL), BF16),
                        pltpu.VMEM((tm, D_FF), BF16),
                        pltpu.VMEM((2, PAD_ROWS + tm, N_MAIN), F32),
                        pltpu.VMEM((2, tm, GATE_W), F32),
                        pltpu.VMEM((8, 2 * GROUP_W), F32)],
        compiler_params=pltpu.CompilerParams(dimension_semantics=("arbitrary",),
                                             vmem_limit_bytes=VMEM_LIMIT_BYTES),
        name="stage_a_mix_prompt",
    )(x, *a_params, *mix_params)


def _mix_sample_kernel(zz_ref, gate_ref, ws_ref, bs_ref, cw_ref, cb_ref, gmg_ref, mlg_ref, prev_ref,
                       c0_ref, n0_ref, m0_ref, mix_ref, c_ref, n_ref, mt_ref):
    row = lax.broadcasted_iota(jnp.int32, (CHUNK, CHUNK), 0)
    col = lax.broadcasted_iota(jnp.int32, (CHUNK, CHUNK), 1)
    seq_shift = DEC_SEQ.bit_length() - 1
    mask = (col <= row) & ((col >> seq_shift) == (row >> seq_shift))
    last_sel = (col == (row | (DEC_SEQ - 1))).astype(F32)

    ones = jnp.ones((HEAD_DIM, HEAD_DIM), BF16)
    _gmlp_heads(zz_ref, ws_ref, bs_ref, gmg_ref, mix_ref, mask, ones)

    x = zz_ref[:, 2 * GROUP_W:4 * GROUP_W]
    row_w = lax.broadcasted_iota(jnp.int32, (CHUNK, 2 * GROUP_W), 0) & (DEC_SEQ - 1)
    prev = prev_ref[...]
    qk = _silu(_conv_taps(x, lambda j: pltpu.roll(prev, CHUNK - DEC_SEQ + j, 0), cw_ref, cb_ref, row_w))

    gates = gate_ref[...]
    bcum = jnp.dot(mask.astype(F32), _log_sigmoid(gates), precision=lax.Precision.HIGHEST,
                   preferred_element_type=F32)
    gates_t = gates.T
    bcum_t = bcum.T
    lane = lax.broadcasted_iota(jnp.int32, (CHUNK, GATE_W), 1)
    mt_all = jnp.zeros((CHUNK, GATE_W), F32)
    scale = HEAD_DIM ** -0.5

    for h in range(N_HEADS):
        q = qk[:, _head_cols(h)]
        k = qk[:, _head_cols(h, GROUP_W)] * scale
        v = zz_ref[:, _head_cols(h, 4 * GROUP_W)]
        o = zz_ref[:, _head_cols(h, 5 * GROUP_W)]
        icol = gates[:, h:h + 1]
        bcol = bcum[:, FG_LANE + h:FG_LANE + h + 1]
        irow = gates_t[h:h + 1, :]
        brow = bcum_t[FG_LANE + h:FG_LANE + h + 1, :]
        m_prev = m0_ref[:, h:h + 1]

        m_t, w_inter, sv, ssum = _intra_chunk(q, k, v, bcol, brow, irow, m_prev, mask)
        qb = q.astype(BF16)
        qc_rows, qn_rows = [], []
        for s in range(SEQ_PER_TILE):
            rows = slice(s * DEC_SEQ, (s + 1) * DEC_SEQ)
            qc_rows.append(jnp.dot(qb[rows], c0_ref[s, h].astype(BF16), preferred_element_type=F32))
            qn_rows.append(jnp.sum(q[rows] * n0_ref[s, h:h + 1, :], axis=-1, keepdims=True))
        qc = jnp.concatenate(qc_rows, axis=0)
        qn = jnp.concatenate(qn_rows, axis=0)
        num = sv + w_inter * qc
        den = ssum + w_inter * qn
        hid = num / jnp.maximum(jnp.abs(den), jnp.exp(-m_t))
        mix_ref[:, _head_cols(h, GROUP_W)] = _head_rms_mxu(jax.nn.sigmoid(o) * hid, mlg_ref[:, _head_cols(h)], ones)

        packed = jnp.where(lane == 0, m_t, jnp.where(lane == 1, bcol, 0.0))
        lastv = jnp.dot(last_sel, packed, precision=lax.Precision.HIGHEST, preferred_element_type=F32)
        m_new = lastv[:, 0:1]
        b_last = lastv[:, 1:2]
        dec = jnp.exp(b_last + m_prev - m_new)
        kw = k * jnp.exp(b_last - bcol + icol - m_new)
        kwb = kw.astype(BF16)
        vb = v.astype(BF16)
        for s in range(SEQ_PER_TILE):
            rows = slice(s * DEC_SEQ, (s + 1) * DEC_SEQ)
            dec_s = dec[s * DEC_SEQ:s * DEC_SEQ + 1, :]
            c_ref[s, h] = dec_s * c0_ref[s, h] + lax.dot_general(kwb[rows], vb[rows], _TN,
                                                                preferred_element_type=F32)
            n_ref[s, h:h + 1, :] = dec_s * n0_ref[s, h:h + 1, :] + jnp.sum(kw[rows], axis=0, keepdims=True)
        mt_all = jnp.where(lane == h, m_t, mt_all)

    mt_ref[...] = mt_all


def _mix_sample(zz, gates, ws_t, bs_cols, cw, cb, gmg, mlg, prev, c0, n0, m0_tok):
    n = zz.shape[0]
    n_tiles = n // CHUNK
    row = lambda w: pl.BlockSpec((CHUNK, w), lambda i: (i, 0))
    const = lambda shape: pl.BlockSpec(shape, lambda i, _nd=len(shape): (0,) * _nd)
    c_spec = pl.BlockSpec((SEQ_PER_TILE, N_HEADS, HEAD_DIM, HEAD_DIM), lambda i: (i, 0, 0, 0))
    n_spec = pl.BlockSpec((SEQ_PER_TILE, N_HEADS, HEAD_DIM), lambda i: (i, 0, 0))
    return pl.pallas_call(
        _mix_sample_kernel,
        grid=(n_tiles,),
        in_specs=[row(N_MAIN), row(GATE_W), const(ws_t.shape), const(bs_cols.shape), const(cw.shape),
                  const(cb.shape), const(gmg.shape), const(mlg.shape), row(2 * GROUP_W),
                  c_spec, n_spec, row(N_HEADS)],
        out_specs=[row(D_MODEL), c_spec, n_spec, row(GATE_W)],
        out_shape=[jax.ShapeDtypeStruct((n, D_MODEL), F32),
                   jax.ShapeDtypeStruct(c0.shape, F32),
                   jax.ShapeDtypeStruct(n0.shape, F32),
                   jax.ShapeDtypeStruct((n, GATE_W), F32)],
        compiler_params=pltpu.CompilerParams(dimension_semantics=("arbitrary",),
                                             vmem_limit_bytes=VMEM_LIMIT_BYTES),
        name="mix_sample",
    )(zz, gates, ws_t, bs_cols, cw, cb, gmg, mlg, prev, c0, n0, m0_tok)


def _ffn_weights(wg, wu, wd):
    return wg.astype(BF16), wu.astype(BF16), wd.astype(BF16)


def _gate_columns(w):
    tail = w[..., N_MAIN:]
    zeros = jnp.zeros(w.shape[:-1] + (FG_LANE - N_HEADS,), w.dtype)
    return jnp.concatenate([tail[..., :N_HEADS], zeros, tail[..., N_HEADS:], zeros], axis=-1)


def kernel(x_prompt, x_sample, state_conv, state_C, state_n, state_m, ffn1_wg, ffn1_wu, ffn1_wd, ln1_g, ln1_b, w_in, b_in, gm_ln_g, gm_ln_b, gm_ws, gm_bs, conv_w, conv_b, gm_out_g, ml_out_g, w_out, ln2_g, ln2_b, ffn2_wg, ffn2_wu, ffn2_wd, ln3_g, ln3_b):
    depth = ffn1_wg.shape[0]
    bp, seq, _ = x_prompt.shape
    bs, dec_seq, _ = x_sample.shape
    assert dec_seq == DEC_SEQ and seq % TOKEN_TILE == 0
    assert (bs * dec_seq) % (TOKEN_TILE * STAGE_C_SUBTILES) == 0 and (bp * seq) % (TOKEN_TILE * STAGE_C_SUBTILES) == 0
    y_p = x_prompt.reshape(bp * seq, D_MODEL)
    y_s = x_sample.reshape(bs * dec_seq, D_MODEL)
    outs = []
    for l in range(depth):
        f1 = _ffn_weights(ffn1_wg[l], ffn1_wu[l], ffn1_wd[l])
        f2 = _ffn_weights(ffn2_wg[l], ffn2_wu[l], ffn2_wd[l])
        row = lambda a: a.reshape(1, -1)
        a_params = (*f1, row(ln1_g[l]), row(ln1_b[l]),
                    w_in[l].astype(BF16), row(b_in[l]),
                    _gate_columns(w_in[l]).astype(BF16), row(_gate_columns(b_in[l])),
                    row(gm_ln_g[l]), row(gm_ln_b[l]))
        c_params = (w_out[l].astype(BF16), row(ln2_g[l]), row(ln2_b[l]), *f2, row(ln3_g[l]), row(ln3_b[l]))
        mix_params = (conv_w[l], row(conv_b[l]), row(gm_out_g[l]), row(ml_out_g[l]))

        x1_p, mix_p, cn_p, m_p, gmv_p, ctail_p = _stage_a_mix_prompt(
            y_p, a_params, gm_ws[l], jnp.transpose(gm_bs[l]), *mix_params, bp, seq)
        x1_s, zz_s, g_s = _stage_a(y_s, *a_params)

        ws_t = jnp.tile(gm_ws[l][:, :DEC_SEQ, :DEC_SEQ], (1, SEQ_PER_TILE, SEQ_PER_TILE))
        bs_t = jnp.tile(jnp.transpose(gm_bs[l][:, :DEC_SEQ]), (SEQ_PER_TILE, 1))
        prev = jnp.pad(state_conv[l], ((0, 0), (DEC_SEQ - (CONV_W - 1), 0), (0, 0))).reshape(bs * DEC_SEQ, 2 * GROUP_W)
        m0_tok = jnp.repeat(state_m[l], DEC_SEQ, axis=0)
        mix_s, c_s, n_s, mt_s = _mix_sample(zz_s, g_s, ws_t, bs_t, *mix_params, prev, state_C[l], state_n[l], m0_tok)

        y_p = _stage_c(mix_p, x1_p, *c_params)
        y_s = _stage_c(mix_s, x1_s, *c_params)

        zz_s3 = zz_s.reshape(bs, dec_seq, N_MAIN)
        outs.append((
            gmv_p.reshape(bp, CHUNK, N_HEADS, HEAD_DIM),
            zz_s3[:, :, GROUP_W:2 * GROUP_W].reshape(bs, dec_seq, N_HEADS, HEAD_DIM),
            ctail_p[:, 8 - (CONV_W - 1):, :],
            zz_s3[:, dec_seq - (CONV_W - 1):, 2 * GROUP_W:4 * GROUP_W],
            cn_p[..., :HEAD_DIM], c_s, cn_p[..., HEAD_DIM], n_s,
            m_p[:, 0, :N_HEADS],
            mt_s.reshape(bs, dec_seq, GATE_W)[:, dec_seq - 1, :N_HEADS],
        ))
    stacked = [jnp.stack(a) for a in zip(*outs)]
    return (y_p.reshape(bp, seq, D_MODEL), y_s.reshape(bs, dec_seq, D_MODEL), *stacked)
```

```python
import functools

import jax
import jax.numpy as jnp
from jax import lax
from jax.experimental import pallas as pl
from jax.experimental.pallas import tpu as pltpu

F32 = jnp.float32
BF16 = jnp.bfloat16

D_MODEL = 1024
D_FF = 2816
HEAD_DIM = 128
N_HEADS = 4
GROUP_W = N_HEADS * HEAD_DIM
CHUNK = 128
CONV_W = 4
DEC_SEQ = 8
SEQ_PER_TILE = CHUNK // DEC_SEQ
N_MAIN = 6 * GROUP_W
GATE_W = 128
FG_LANE = 64
FF_CHUNK = 256
N_FF_CHUNKS = D_FF // FF_CHUNK
FF_GROUP = 4
ALPHA = 2.0 ** 0.25
LN_EPS = 1e-5
TOKEN_TILE = 512
STAGE_C_SUBTILES = 2
PAD_ROWS = 8
MIX_UNITS_PER_TILE = 6 * (TOKEN_TILE // CHUNK) + 3
MIX_SHARE_IN_FFN = (9, 10)
VMEM_LIMIT_BYTES = 60 * 1024 * 1024

_NT = (((1,), (1,)), ((), ()))
_TN = (((0,), (0,)), ((), ()))


def _layer_norm(x, g, b):
    mu = jnp.mean(x, axis=-1, keepdims=True)
    xc = x - mu
    var = jnp.mean(xc * xc, axis=-1, keepdims=True)
    return xc * lax.rsqrt(var + LN_EPS) * g + b


def _head_rms_mxu(x, g, ones):
    ss = jnp.dot((x * x).astype(BF16), ones, preferred_element_type=F32)
    return x * lax.rsqrt(ss * (1.0 / HEAD_DIM) + LN_EPS) * g


def _silu(x):
    return x * jax.nn.sigmoid(x)


def _log_sigmoid(x):
    return jnp.minimum(x, 0.0) - jnp.log1p(jnp.exp(-jnp.abs(x)))


def _swiglu(xb_ref, wg_ref, wu_ref, wd_ref, h_ref, between_chunks=None):
    acc = None
    for c in range(N_FF_CHUNKS):
        cols = slice(c * FF_CHUNK, (c + 1) * FF_CHUNK)
        xb = xb_ref[...]
        g = jnp.dot(xb, wg_ref[:, cols], preferred_element_type=F32)
        u = jnp.dot(xb, wu_ref[:, cols], preferred_element_type=F32)
        h_ref[:, cols] = (_silu(g) * u).astype(BF16)
        if (c + 1) % FF_GROUP == 0 or c == N_FF_CHUNKS - 1:
            grp = slice((c // FF_GROUP) * FF_GROUP * FF_CHUNK, (c + 1) * FF_CHUNK)
            d = jnp.dot(h_ref[:, grp], wd_ref[grp, :], preferred_element_type=F32)
            acc = d if acc is None else acc + d
        if between_chunks is not None:
            between_chunks(c)
    return acc


def _ffn1_and_project(x_ref, wg_ref, wu_ref, wd_ref, lng_ref, lnb_ref, wmain_ref, bmain_ref, wgate_ref, bgate_ref,
                      vlng_ref, vlnb_ref, x1_ref, xb_ref, h_ref, store_z, store_gate, between_chunks=None):
    x = x_ref[...]
    xb_ref[...] = x.astype(BF16)
    ffn = _swiglu(xb_ref, wg_ref, wu_ref, wd_ref, h_ref, between_chunks)
    x1 = _layer_norm(ALPHA * x + 0.5 * ffn, lng_ref[...], lnb_ref[...])
    x1_ref[...] = x1
    xb_ref[...] = x1.astype(BF16)
    for grp in range(N_MAIN // GROUP_W):
        cols = slice(grp * GROUP_W, (grp + 1) * GROUP_W)
        z = jnp.dot(xb_ref[...], wmain_ref[:, cols], preferred_element_type=F32) + bmain_ref[:, cols]
        if grp == 0:
            store_z(cols, jax.nn.gelu(z))
        elif grp == 1:
            z = jax.nn.gelu(z)
            for h in range(N_HEADS):
                hc = slice(h * HEAD_DIM, (h + 1) * HEAD_DIM)
                store_z(slice(GROUP_W + h * HEAD_DIM, GROUP_W + (h + 1) * HEAD_DIM),
                        _layer_norm(z[:, hc], vlng_ref[:, hc], vlnb_ref[:, hc]))
        else:
            store_z(cols, z)
    store_gate(jnp.dot(xb_ref[...], wgate_ref[...], preferred_element_type=F32) + bgate_ref[...])


def _stage_a_kernel(*refs):
    *in_refs, x1_ref, zz_ref, gate_ref, xb_ref, h_ref = refs

    def store_z(cols, val):
        zz_ref[:, cols] = val

    def store_gate(val):
        gate_ref[...] = val

    _ffn1_and_project(*in_refs, x1_ref, xb_ref, h_ref, store_z, store_gate)


def _const_spec(shape):
    nd = len(shape)
    return pl.BlockSpec(shape, lambda i, _nd=nd: (0,) * _nd, pipeline_mode=pl.Buffered(1))


def _stage_a(x, wg, wu, wd, lng, lnb, wmain, bmain, wgate, bgate, vlng, vlnb):
    n = x.shape[0]
    tm = TOKEN_TILE
    row = lambda w: pl.BlockSpec((tm, w), lambda i: (i, 0))
    return pl.pallas_call(
        _stage_a_kernel,
        grid=(n // tm,),
        in_specs=[row(D_MODEL),
                  _const_spec(wg.shape), _const_spec(wu.shape), _const_spec(wd.shape),
                  _const_spec(lng.shape), _const_spec(lnb.shape),
                  _const_spec(wmain.shape), _const_spec(bmain.shape),
                  _const_spec(wgate.shape), _const_spec(bgate.shape),
                  _const_spec(vlng.shape), _const_spec(vlnb.shape)],
        out_specs=[row(D_MODEL), row(N_MAIN), row(GATE_W)],
        out_shape=[jax.ShapeDtypeStruct((n, D_MODEL), F32),
                   jax.ShapeDtypeStruct((n, N_MAIN), F32),
                   jax.ShapeDtypeStruct((n, GATE_W), F32)],
        scratch_shapes=[pltpu.VMEM((tm, D_MODEL), BF16), pltpu.VMEM((tm, D_FF), BF16)],
        compiler_params=pltpu.CompilerParams(dimension_semantics=("arbitrary",),
                                             vmem_limit_bytes=VMEM_LIMIT_BYTES),
        name="stage_a",
    )(x, wg, wu, wd, lng, lnb, wmain, bmain, wgate, bgate, vlng, vlnb)


def _stage_c_kernel(mix_ref, x1_ref, wout_ref, ln2g_ref, ln2b_ref, wg_ref, wu_ref, wd_ref, ln3g_ref, ln3b_ref,
                    y_ref, xb_ref, h_ref):
    tiles = [slice(s * TOKEN_TILE, (s + 1) * TOKEN_TILE) for s in range(STAGE_C_SUBTILES)]
    for rows in tiles:
        proj = jnp.dot(mix_ref[rows, :], wout_ref[...], preferred_element_type=F32)
        y = _layer_norm(ALPHA * x1_ref[rows, :] + proj, ln2g_ref[...], ln2b_ref[...])
        y_ref[rows, :] = y
        xb_ref[rows, :] = y.astype(BF16)
    for rows in tiles:
        ffn = _swiglu(xb_ref.at[rows, :], wg_ref, wu_ref, wd_ref, h_ref.at[rows, :])
        y_ref[rows, :] = _layer_norm(ALPHA * y_ref[rows, :] + 0.5 * ffn, ln3g_ref[...], ln3b_ref[...])


def _stage_c(mix, x1, wout, ln2g, ln2b, wg, wu, wd, ln3g, ln3b):
    n = x1.shape[0]
    tm = TOKEN_TILE * STAGE_C_SUBTILES
    row = lambda w: pl.BlockSpec((tm, w), lambda i: (i, 0))
    return pl.pallas_call(
        _stage_c_kernel,
        grid=(n // tm,),
        in_specs=[row(D_MODEL), row(D_MODEL),
                  _const_spec(wout.shape), _const_spec(ln2g.shape), _const_spec(ln2b.shape),
                  _const_spec(wg.shape), _const_spec(wu.shape), _const_spec(wd.shape),
                  _const_spec(ln3g.shape), _const_spec(ln3b.shape)],
        out_specs=row(D_MODEL),
        out_shape=jax.ShapeDtypeStruct((n, D_MODEL), F32),
        scratch_shapes=[pltpu.VMEM((tm, D_MODEL), BF16), pltpu.VMEM((tm, D_FF), BF16)],
        compiler_params=pltpu.CompilerParams(dimension_semantics=("arbitrary",),
                                             vmem_limit_bytes=VMEM_LIMIT_BYTES),
        name="stage_c",
    )(mix, x1, wout, ln2g, ln2b, wg, wu, wd, ln3g, ln3b)


def _head_cols(h, base=0):
    return slice(base + h * HEAD_DIM, base + (h + 1) * HEAD_DIM)


def _gmlp_heads(zz_ref, ws_ref, bs_ref, gmg_ref, mix_ref, mask, ones):
    for h in range(N_HEADS):
        u = zz_ref[:, _head_cols(h)]
        vn = zz_ref[:, _head_cols(h, GROUP_W)]
        w = jnp.where(mask, ws_ref[h], 0.0).astype(BF16)
        mixed = jnp.dot(w, vn.astype(BF16), preferred_element_type=F32) + bs_ref[:, h:h + 1]
        mix_ref[:, _head_cols(h)] = _head_rms_mxu(u * mixed, gmg_ref[:, _head_cols(h)], ones).astype(BF16)


def _intra_chunk(q, k, v, bcol, brow, irow, mcol, mask):
    dlog = jnp.where(mask, bcol - brow + irow, -jnp.inf)
    inter = bcol + mcol
    m_t = jnp.maximum(inter, jnp.max(dlog, axis=-1, keepdims=True))
    w_intra = jnp.exp(dlog - m_t)
    w_inter = jnp.exp(inter - m_t)
    s = lax.dot_general(q.astype(BF16), k.astype(BF16), _NT, preferred_element_type=F32) * w_intra
    sv = jnp.dot(s.astype(BF16), v.astype(BF16), preferred_element_type=F32)
    ssum = jnp.sum(s, axis=-1, keepdims=True)
    return m_t, w_inter, sv, ssum


def _conv_taps(x, prev_fn, cw_ref, cb_ref, row_in_seq):
    acc = x * cw_ref[CONV_W - 1:CONV_W, :] + cb_ref[...]
    for j in range(1, CONV_W):
        shifted = jnp.where(row_in_seq < j, prev_fn(j), pltpu.roll(x, j, 0))
        acc = acc + shifted * cw_ref[CONV_W - 1 - j:CONV_W - j, :]
    return acc


def _mix_prompt_tile(zt, gt, ws_ref, bs_ref, cw_ref, cb_ref, gmg_ref, mlg_ref,
                     mix_ref, cn_ref, m_ref, gmv_ref, ctail_ref, tail_ref):
    row = lax.broadcasted_iota(jnp.int32, (CHUNK, CHUNK), 0)
    col = lax.broadcasted_iota(jnp.int32, (CHUNK, CHUNK), 1)
    causal = col <= row
    causal_f = causal.astype(F32)
    qk_cols = slice(2 * GROUP_W, 4 * GROUP_W)
    scale = HEAD_DIM ** -0.5

    heads = range(N_HEADS)
    chunks = range(TOKEN_TILE // CHUNK)
    rows = [slice(PAD_ROWS + c * CHUNK, PAD_ROWS + (c + 1) * CHUNK) for c in chunks]
    out_rows = [slice(c * CHUNK, (c + 1) * CHUNK) for c in chunks]

    zt[0:PAD_ROWS, qk_cols] = tail_ref[...]
    ones = jnp.ones((HEAD_DIM, HEAD_DIM), BF16)
    ones_col = jnp.ones((CHUNK, HEAD_DIM), BF16)

    w_masked = [jnp.where(causal, ws_ref[h], 0.0).astype(BF16) for h in heads]
    bias = [jnp.broadcast_to(bs_ref[:, h:h + 1], (CHUNK, HEAD_DIM)) for h in heads]
    mixed = {}
    for c in chunks:
        for h in heads:
            mixed[c, h] = jnp.dot(w_masked[h], zt[rows[c], _head_cols(h, GROUP_W)].astype(BF16),
                                  preferred_element_type=F32)
        yield
    for c in chunks:
        for h in heads:
            gm = zt[rows[c], _head_cols(h)] * (mixed[c, h] + bias[h])
            mix_ref[out_rows[c], _head_cols(h)] = _head_rms_mxu(gm, gmg_ref[:, _head_cols(h)], ones).astype(BF16)
        yield

    qk = []
    for c in chunks:
        r0 = rows[c].start
        acc = zt[rows[c], qk_cols] * cw_ref[CONV_W - 1:CONV_W, :] + cb_ref[...]
        for j in range(1, CONV_W):
            acc = acc + zt[r0 - j:r0 - j + CHUNK, qk_cols] * cw_ref[CONV_W - 1 - j:CONV_W - j, :]
        qk.append(_silu(acc))
        yield

    gates, bcum, gates_t, bcum_t = [], [], [], []
    for c in chunks:
        g = gt[out_rows[c], :]
        b = jnp.dot(causal_f, _log_sigmoid(g), precision=lax.Precision.HIGHEST, preferred_element_type=F32)
        gates.append(g)
        bcum.append(b)
        gates_t.append(g.T)
        bcum_t.append(b.T)
    yield
    icol = {(c, h): gates[c][:, h:h + 1] for c in chunks for h in heads}
    bcol = {(c, h): bcum[c][:, FG_LANE + h:FG_LANE + h + 1] for c in chunks for h in heads}

    dlog, mx = {}, {}
    for c in chunks:
        for h in heads:
            irow = gates_t[c][h:h + 1, :]
            brow = bcum_t[c][FG_LANE + h:FG_LANE + h + 1, :]
            dlog[c, h] = jnp.where(causal, bcol[c, h] - brow + irow, -jnp.inf)
            mx[c, h] = jnp.max(dlog[c, h], axis=-1, keepdims=True)
        yield

    m_row = m_ref[0]
    lane = lax.broadcasted_iota(jnp.int32, m_row.shape, 1)
    m_t, w_inter, dec, wk = {}, {}, {}, {}
    for h in heads:
        m_prev = m_row[:, h:h + 1]
        for c in chunks:
            inter = bcol[c, h] + m_prev
            m_t[c, h] = jnp.maximum(inter, mx[c, h])
            w_inter[c, h] = jnp.exp(inter - m_t[c, h])
            m_new = m_t[c, h][CHUNK - 1:CHUNK, :]
            b_last = bcol[c, h][CHUNK - 1:CHUNK, :]
            dec[c, h] = jnp.exp(b_last + m_prev - m_new)
            wk[c, h] = jnp.exp(b_last - bcol[c, h] + icol[c, h] - m_new)
            m_prev = m_new
        m_row = jnp.where(lane == h, m_prev, m_row)
    m_ref[0] = m_row
    yield

    sv, upd = {}, {}
    for c in chunks:
        for h in heads:
            q = qk[c][:, _head_cols(h)]
            k = qk[c][:, _head_cols(h, GROUP_W)] * scale
            v1 = jnp.concatenate([zt[rows[c], _head_cols(h, 4 * GROUP_W)].astype(BF16), ones_col], axis=1)
            s = lax.dot_general(q.astype(BF16), k.astype(BF16), _NT, preferred_element_type=F32) * jnp.exp(
                dlog[c, h] - m_t[c, h])
            sv[c, h] = jnp.dot(s.astype(BF16), v1, preferred_element_type=F32)
            kw = k * wk[c, h]
            upd[c, h] = lax.dot_general(kw.astype(BF16), v1, _TN, preferred_element_type=F32)
        yield

    for h in heads:
        cn_state = cn_ref[0, h]
        for c in chunks:
            q = qk[c][:, _head_cols(h)]
            qcn = jnp.dot(q.astype(BF16), cn_state.astype(BF16), preferred_element_type=F32)
            nd = sv[c, h] + jnp.broadcast_to(w_inter[c, h], (CHUNK, 2 * HEAD_DIM)) * qcn
            num, den = nd[:, :HEAD_DIM], nd[:, HEAD_DIM:]
            hid = num / jnp.maximum(jnp.abs(den), jnp.exp(-m_t[c, h]))
            o = zt[rows[c], _head_cols(h, 5 * GROUP_W)]
            mix_ref[out_rows[c], _head_cols(h, GROUP_W)] = _head_rms_mxu(jax.nn.sigmoid(o) * hid,
                                                                        mlg_ref[:, _head_cols(h)], ones).astype(BF16)
            cn_state = dec[c, h] * cn_state + upd[c, h]
        cn_ref[0, h] = cn_state
        yield

    last8 = slice(PAD_ROWS + TOKEN_TILE - 8, PAD_ROWS + TOKEN_TILE)
    tail_ref[...] = zt[last8, qk_cols]
    ctail_ref[0] = zt[last8, qk_cols]
    gmv_ref[0] = zt[PAD_ROWS + TOKEN_TILE - CHUNK:PAD_ROWS + TOKEN_TILE, GROUP_W:2 * GROUP_W]


def _fused_prompt_kernel(tiles_per_seq, *refs):
    (x_ref, wg_ref, wu_ref, wd_ref, lng_ref, lnb_ref, wmain_ref, bmain_ref, wgate_ref, bgate_ref, vlng_ref, vlnb_ref,
     ws_ref, bs_ref, cw_ref, cb_ref, gmg_ref, mlg_ref,
     x1_ref, mix_ref, cn_ref, m_ref, gmv_ref, ctail_ref,
     xb_ref, h_ref, zbuf_ref, gbuf_ref, tail_ref) = refs
    i = pl.program_id(0)
    slot_w = i % 2
    slot_r = 1 - slot_w

    @pl.when(i == 0)
    def _():
        zbuf_ref[1] = jnp.zeros(zbuf_ref.shape[1:], F32)
        gbuf_ref[1] = jnp.zeros(gbuf_ref.shape[1:], F32)

    @pl.when((i == 0) | (i % tiles_per_seq == 1))
    def _():
        cn_ref[...] = jnp.zeros_like(cn_ref)
        m_ref[...] = jnp.zeros_like(m_ref)
        tail_ref[...] = jnp.zeros_like(tail_ref)

    def store_z(cols, val):
        zbuf_ref[slot_w, PAD_ROWS:PAD_ROWS + TOKEN_TILE, cols] = val

    def store_gate(val):
        gbuf_ref[slot_w] = val

    mixer = _mix_prompt_tile(zbuf_ref.at[slot_r], gbuf_ref.at[slot_r], ws_ref, bs_ref, cw_ref, cb_ref, gmg_ref,
                             mlg_ref, mix_ref, cn_ref, m_ref, gmv_ref, ctail_ref, tail_ref)

    traced = [0]

    def mixer_units(c):
        target = (MIX_SHARE_IN_FFN[0] * (c + 1) * MIX_UNITS_PER_TILE) // (MIX_SHARE_IN_FFN[1] * N_FF_CHUNKS)
        while traced[0] < target:
            next(mixer)
            traced[0] += 1

    _ffn1_and_project(x_ref, wg_ref, wu_ref, wd_ref, lng_ref, lnb_ref, wmain_ref, bmain_ref, wgate_ref, bgate_ref,
                      vlng_ref, vlnb_ref, x1_ref, xb_ref, h_ref, store_z, store_gate, mixer_units)
    for _ in mixer:
        traced[0] += 1
    assert traced[0] == MIX_UNITS_PER_TILE - 1


def _stage_a_mix_prompt(x, a_params, ws, bs_cols, cw, cb, gmg, mlg, batch, seq):
    tm = TOKEN_TILE
    n_tiles = (batch * seq) // tm
    tiles_per_seq = seq // tm
    mix_params = (ws, bs_cols, cw, cb, gmg, mlg)
    tile_in = lambda i: (jnp.minimum(i, n_tiles - 1), 0)
    tile_out = lambda i: (jnp.maximum(i - 1, 0), 0)
    seq_of = lambda i: jnp.maximum(i - 1, 0) // tiles_per_seq
    return pl.pallas_call(
        functools.partial(_fused_prompt_kernel, tiles_per_seq),
        grid=(n_tiles + 1,),
        in_specs=[pl.BlockSpec((tm, D_MODEL), tile_in)] + [_const_spec(p.shape) for p in (*a_params, *mix_params)],
        out_specs=[pl.BlockSpec((tm, D_MODEL), tile_in),
                   pl.BlockSpec((tm, D_MODEL), tile_out),
                   pl.BlockSpec((1, N_HEADS, HEAD_DIM, 2 * HEAD_DIM), lambda i: (seq_of(i), 0, 0, 0)),
                   pl.BlockSpec((1, 1, GATE_W), lambda i: (seq_of(i), 0, 0)),
                   pl.BlockSpec((1, CHUNK, GROUP_W), lambda i: (seq_of(i), 0, 0)),
                   pl.BlockSpec((1, 8, 2 * GROUP_W), lambda i: (seq_of(i), 0, 0))],
        out_shape=[jax.ShapeDtypeStruct((batch * seq, D_MODEL), F32),
                   jax.ShapeDtypeStruct((batch * seq, D_MODEL), BF16),
                   jax.ShapeDtypeStruct((batch, N_HEADS, HEAD_DIM, 2 * HEAD_DIM), F32),
                   jax.ShapeDtypeStruct((batch, 1, GATE_W), F32),
                   jax.ShapeDtypeStruct((batch, CHUNK, GROUP_W), F32),
                   jax.ShapeDtypeStruct((batch, 8, 2 * GROUP_W), F32)],
        scratch_shapes=[pltpu.VMEM((tm, D_MODEL), BF16),
                        pltpu.VMEM((tm, D_FF), BF16),
                        pltpu.VMEM((2, PAD_ROWS + tm, N_MAIN), F32),
                        pltpu.VMEM((2, tm, GATE_W), F32),
                        pltpu.VMEM((8, 2 * GROUP_W), F32)],
        compiler_params=pltpu.CompilerParams(dimension_semantics=("arbitrary",),
                                             vmem_limit_bytes=VMEM_LIMIT_BYTES),
        name="stage_a_mix_prompt",
    )(x, *a_params, *mix_params)


def _mix_sample_kernel(zz_ref, gate_ref, ws_ref, bs_ref, cw_ref, cb_ref, gmg_ref, mlg_ref, prev_ref,
                       c0_ref, n0_ref, m0_ref, mix_ref, c_ref, n_ref, mt_ref):
    row = lax.broadcasted_iota(jnp.int32, (CHUNK, CHUNK), 0)
    col = lax.broadcasted_iota(jnp.int32, (CHUNK, CHUNK), 1)
    seq_shift = DEC_SEQ.bit_length() - 1
    mask = (col <= row) & ((col >> seq_shift) == (row >> seq_shift))
    last_sel = (col == (row | (DEC_SEQ - 1))).astype(F32)

    ones = jnp.ones((HEAD_DIM, HEAD_DIM), BF16)
    _gmlp_heads(zz_ref, ws_ref, bs_ref, gmg_ref, mix_ref, mask, ones)

    x = zz_ref[:, 2 * GROUP_W:4 * GROUP_W]
    row_w = lax.broadcasted_iota(jnp.int32, (CHUNK, 2 * GROUP_W), 0) & (DEC_SEQ - 1)
    prev = prev_ref[...]
    qk = _silu(_conv_taps(x, lambda j: pltpu.roll(prev, CHUNK - DEC_SEQ + j, 0), cw_ref, cb_ref, row_w))

    gates = gate_ref[...]
    bcum = jnp.dot(mask.astype(F32), _log_sigmoid(gates), precision=lax.Precision.HIGHEST,
                   preferred_element_type=F32)
    gates_t = gates.T
    bcum_t = bcum.T
    lane = lax.broadcasted_iota(jnp.int32, (CHUNK, GATE_W), 1)
    mt_all = jnp.zeros((CHUNK, GATE_W), F32)
    scale = HEAD_DIM ** -0.5

    for h in range(N_HEADS):
        q = qk[:, _head_cols(h)]
        k = qk[:, _head_cols(h, GROUP_W)] * scale
        v = zz_ref[:, _head_cols(h, 4 * GROUP_W)]
        o = zz_ref[:, _head_cols(h, 5 * GROUP_W)]
        icol = gates[:, h:h + 1]
        bcol = bcum[:, FG_LANE + h:FG_LANE + h + 1]
        irow = gates_t[h:h + 1, :]
        brow = bcum_t[FG_LANE + h:FG_LANE + h + 1, :]
        m_prev = m0_ref[:, h:h + 1]

        m_t, w_inter, sv, ssum = _intra_chunk(q, k, v, bcol, brow, irow, m_prev, mask)
        qb = q.astype(BF16)
        qc_rows, qn_rows = [], []
        for s in range(SEQ_PER_TILE):
            rows = slice(s * DEC_SEQ, (s + 1) * DEC_SEQ)
            qc_rows.append(jnp.dot(qb[rows], c0_ref[s, h].astype(BF16), preferred_element_type=F32))
            qn_rows.append(jnp.sum(q[rows] * n0_ref[s, h:h + 1, :], axis=-1, keepdims=True))
        qc = jnp.concatenate(qc_rows, axis=0)
        qn = jnp.concatenate(qn_rows, axis=0)
        num = sv + w_inter * qc
        den = ssum + w_inter * qn
        hid = num / jnp.maximum(jnp.abs(den), jnp.exp(-m_t))
        mix_ref[:, _head_cols(h, GROUP_W)] = _head_rms_mxu(jax.nn.sigmoid(o) * hid, mlg_ref[:, _head_cols(h)],
                                                           ones).astype(BF16)

        packed = jnp.where(lane == 0, m_t, jnp.where(lane == 1, bcol, 0.0))
        lastv = jnp.dot(last_sel, packed, precision=lax.Precision.HIGHEST, preferred_element_type=F32)
        m_new = lastv[:, 0:1]
        b_last = lastv[:, 1:2]
        dec = jnp.exp(b_last + m_prev - m_new)
        kw = k * jnp.exp(b_last - bcol + icol - m_new)
        kwb = kw.astype(BF16)
        vb = v.astype(BF16)
        for s in range(SEQ_PER_TILE):
            rows = slice(s * DEC_SEQ, (s + 1) * DEC_SEQ)
            dec_s = dec[s * DEC_SEQ:s * DEC_SEQ + 1, :]
            c_ref[s, h] = dec_s * c0_ref[s, h] + lax.dot_general(kwb[rows], vb[rows], _TN,
                                                                preferred_element_type=F32)
            n_ref[s, h:h + 1, :] = dec_s * n0_ref[s, h:h + 1, :] + jnp.sum(kw[rows], axis=0, keepdims=True)
        mt_all = jnp.where(lane == h, m_t, mt_all)

    mt_ref[...] = mt_all


def _mix_sample(zz, gates, ws_t, bs_cols, cw, cb, gmg, mlg, prev, c0, n0, m0_tok):
    n = zz.shape[0]
    n_tiles = n // CHUNK
    row = lambda w: pl.BlockSpec((CHUNK, w), lambda i: (i, 0))
    const = lambda shape: pl.BlockSpec(shape, lambda i, _nd=len(shape): (0,) * _nd)
    c_spec = pl.BlockSpec((SEQ_PER_TILE, N_HEADS, HEAD_DIM, HEAD_DIM), lambda i: (i, 0, 0, 0))
    n_spec = pl.BlockSpec((SEQ_PER_TILE, N_HEADS, HEAD_DIM), lambda i: (i, 0, 0))
    return pl.pallas_call(
        _mix_sample_kernel,
        grid=(n_tiles,),
        in_specs=[row(N_MAIN), row(GATE_W), const(ws_t.shape), const(bs_cols.shape), const(cw.shape),
                  const(cb.shape), const(gmg.shape), const(mlg.shape), row(2 * GROUP_W),
                  c_spec, n_spec, row(N_HEADS)],
        out_specs=[row(D_MODEL), c_spec, n_spec, row(GATE_W)],
        out_shape=[jax.ShapeDtypeStruct((n, D_MODEL), BF16),
                   jax.ShapeDtypeStruct(c0.shape, F32),
                   jax.ShapeDtypeStruct(n0.shape, F32),
                   jax.ShapeDtypeStruct((n, GATE_W), F32)],
        compiler_params=pltpu.CompilerParams(dimension_semantics=("arbitrary",),
                                             vmem_limit_bytes=VMEM_LIMIT_BYTES),
        name="mix_sample",
    )(zz, gates, ws_t, bs_cols, cw, cb, gmg, mlg, prev, c0, n0, m0_tok)


def _ffn_weights(wg, wu, wd):
    return wg.astype(BF16), wu.astype(BF16), wd.astype(BF16)


def _gate_columns(w):
    out = jnp.zeros(w.shape[:-1] + (GATE_W,), w.dtype)
    out = out.at[..., 0:N_HEADS].set(w[..., N_MAIN:N_MAIN + N_HEADS])
    return out.at[..., FG_LANE:FG_LANE + N_HEADS].set(w[..., N_MAIN + N_HEADS:N_MAIN + 2 * N_HEADS])


def kernel(x_prompt, x_sample, state_conv, state_C, state_n, state_m, ffn1_wg, ffn1_wu, ffn1_wd, ln1_g, ln1_b, w_in, b_in, gm_ln_g, gm_ln_b, gm_ws, gm_bs, conv_w, conv_b, gm_out_g, ml_out_g, w_out, ln2_g, ln2_b, ffn2_wg, ffn2_wu, ffn2_wd, ln3_g, ln3_b):
    depth = ffn1_wg.shape[0]
    bp, seq, _ = x_prompt.shape
    bs, dec_seq, _ = x_sample.shape
    assert dec_seq == DEC_SEQ and seq % TOKEN_TILE == 0
    assert (bs * dec_seq) % (TOKEN_TILE * STAGE_C_SUBTILES) == 0 and (bp * seq) % (TOKEN_TILE * STAGE_C_SUBTILES) == 0
    y_p = x_prompt.reshape(bp * seq, D_MODEL)
    y_s = x_sample.reshape(bs * dec_seq, D_MODEL)
    outs = []
    for l in range(depth):
        f1 = _ffn_weights(ffn1_wg[l], ffn1_wu[l], ffn1_wd[l])
        f2 = _ffn_weights(ffn2_wg[l], ffn2_wu[l], ffn2_wd[l])
        row = lambda a: a.reshape(1, -1)
        a_params = (*f1, row(ln1_g[l]), row(ln1_b[l]),
                    w_in[l].astype(BF16), row(b_in[l]),
                    _gate_columns(w_in[l]).astype(BF16), row(_gate_columns(b_in[l])),
                    row(gm_ln_g[l]), row(gm_ln_b[l]))
        c_params = (w_out[l].astype(BF16), row(ln2_g[l]), row(ln2_b[l]), *f2, row(ln3_g[l]), row(ln3_b[l]))
        mix_params = (conv_w[l], row(conv_b[l]), row(gm_out_g[l]), row(ml_out_g[l]))

        x1_p, mix_p, cn_p, m_p, gmv_p, ctail_p = _stage_a_mix_prompt(
            y_p, a_params, gm_ws[l], jnp.transpose(gm_bs[l]), *mix_params, bp, seq)
        x1_s, zz_s, g_s = _stage_a(y_s, *a_params)

        ws_t = jnp.tile(gm_ws[l][:, :DEC_SEQ, :DEC_SEQ], (1, SEQ_PER_TILE, SEQ_PER_TILE))
        bs_t = jnp.tile(jnp.transpose(gm_bs[l][:, :DEC_SEQ]), (SEQ_PER_TILE, 1))
        prev = jnp.pad(state_conv[l], ((0, 0), (DEC_SEQ - (CONV_W - 1), 0), (0, 0))).reshape(bs * DEC_SEQ, 2 * GROUP_W)
        m0_tok = jnp.repeat(state_m[l], DEC_SEQ, axis=0)
        mix_s, c_s, n_s, mt_s = _mix_sample(zz_s, g_s, ws_t, bs_t, *mix_params, prev, state_C[l], state_n[l], m0_tok)

        y_p = _stage_c(mix_p, x1_p, *c_params)
        y_s = _stage_c(mix_s, x1_s, *c_params)

        zz_s3 = zz_s.reshape(bs, dec_seq, N_MAIN)
        outs.append((
            gmv_p.reshape(bp, CHUNK, N_HEADS, HEAD_DIM),
            zz_s3[:, :, GROUP_W:2 * GROUP_W].reshape(bs, dec_seq, N_HEADS, HEAD_DIM),
            ctail_p[:, 8 - (CONV_W - 1):, :],
            zz_s3[:, dec_seq - (CONV_W - 1):, 2 * GROUP_W:4 * GROUP_W],
            cn_p[..., :HEAD_DIM], c_s, cn_p[..., HEAD_DIM], n_s,
            m_p[:, 0, :N_HEADS],
            mt_s.reshape(bs, dec_seq, GATE_W)[:, dec_seq - 1, :N_HEADS],
        ))
    stacked = [jnp.stack(a) for a in zip(*outs)]
    return (y_p.reshape(bp, seq, D_MODEL), y_s.reshape(bs, dec_seq, D_MODEL), *stacked)
```

```python
import functools

import jax
import jax.numpy as jnp
from jax import lax
from jax.experimental import pallas as pl
from jax.experimental.pallas import tpu as pltpu

F32 = jnp.float32
BF16 = jnp.bfloat16

D_MODEL = 1024
D_FF = 2816
HEAD_DIM = 128
N_HEADS = 4
GROUP_W = N_HEADS * HEAD_DIM
CHUNK = 128
CONV_W = 4
DEC_SEQ = 8
SEQ_PER_TILE = CHUNK // DEC_SEQ
N_MAIN = 6 * GROUP_W
GATE_W = 128
FG_LANE = 64
FF_CHUNK = 256
N_FF_CHUNKS = D_FF // FF_CHUNK
FF_GROUP = 4
ALPHA = 2.0 ** 0.25
LN_EPS = 1e-5
TOKEN_TILE = 512
STAGE_C_SUBTILES = 2
PAD_ROWS = 8
MIX_UNITS_PER_TILE = 6 * (TOKEN_TILE // CHUNK) + 3
MIX_SHARE_IN_FFN = (19, 20)
VMEM_LIMIT_BYTES = 60 * 1024 * 1024

_NT = (((1,), (1,)), ((), ()))
_TN = (((0,), (0,)), ((), ()))


def _layer_norm(x, g, b):
    mu = jnp.mean(x, axis=-1, keepdims=True)
    xc = x - mu
    var = jnp.mean(xc * xc, axis=-1, keepdims=True)
    return xc * lax.rsqrt(var + LN_EPS) * g + b


def _head_rms_mxu(x, g, ones):
    ss = jnp.dot((x * x).astype(BF16), ones, preferred_element_type=F32)
    return x * lax.rsqrt(ss * (1.0 / HEAD_DIM) + LN_EPS) * g


def _silu(x):
    return x * jax.nn.sigmoid(x)


def _log_sigmoid(x):
    return jnp.minimum(x, 0.0) - jnp.log1p(jnp.exp(-jnp.abs(x)))


def _swiglu(xb_ref, wg_ref, wu_ref, wd_ref, h_ref, between_chunks=None):
    acc = None
    for c in range(N_FF_CHUNKS):
        cols = slice(c * FF_CHUNK, (c + 1) * FF_CHUNK)
        xb = xb_ref[...]
        g = jnp.dot(xb, wg_ref[:, cols], preferred_element_type=F32)
        u = jnp.dot(xb, wu_ref[:, cols], preferred_element_type=F32)
        h_ref[:, cols] = (_silu(g) * u).astype(BF16)
        if (c + 1) % FF_GROUP == 0 or c == N_FF_CHUNKS - 1:
            grp = slice((c // FF_GROUP) * FF_GROUP * FF_CHUNK, (c + 1) * FF_CHUNK)
            d = jnp.dot(h_ref[:, grp], wd_ref[grp, :], preferred_element_type=F32)
            acc = d if acc is None else acc + d
        if between_chunks is not None:
            between_chunks(c)
    return acc


def _ffn1_and_project(x_ref, wg_ref, wu_ref, wd_ref, lng_ref, lnb_ref, wmain_ref, bmain_ref, wgate_ref, bgate_ref,
                      vlng_ref, vlnb_ref, x1_ref, xb_ref, h_ref, store_z, store_gate, between_chunks=None):
    x = x_ref[...]
    xb_ref[...] = x.astype(BF16)
    ffn = _swiglu(xb_ref, wg_ref, wu_ref, wd_ref, h_ref, between_chunks)
    x1 = _layer_norm(ALPHA * x + 0.5 * ffn, lng_ref[...], lnb_ref[...])
    x1_ref[...] = x1
    xb_ref[...] = x1.astype(BF16)
    for grp in range(N_MAIN // GROUP_W):
        cols = slice(grp * GROUP_W, (grp + 1) * GROUP_W)
        z = jnp.dot(xb_ref[...], wmain_ref[:, cols], preferred_element_type=F32) + bmain_ref[:, cols]
        if grp == 0:
            store_z(cols, jax.nn.gelu(z))
        elif grp == 1:
            z = jax.nn.gelu(z)
            for h in range(N_HEADS):
                hc = slice(h * HEAD_DIM, (h + 1) * HEAD_DIM)
                store_z(slice(GROUP_W + h * HEAD_DIM, GROUP_W + (h + 1) * HEAD_DIM),
                        _layer_norm(z[:, hc], vlng_ref[:, hc], vlnb_ref[:, hc]))
        else:
            store_z(cols, z)
    store_gate(jnp.dot(xb_ref[...], wgate_ref[...], preferred_element_type=F32) + bgate_ref[...])


def _stage_a_kernel(*refs):
    *in_refs, x1_ref, zz_ref, gate_ref, xb_ref, h_ref = refs

    def store_z(cols, val):
        zz_ref[:, cols] = val

    def store_gate(val):
        gate_ref[...] = val

    _ffn1_and_project(*in_refs, x1_ref, xb_ref, h_ref, store_z, store_gate)


def _const_spec(shape):
    nd = len(shape)
    return pl.BlockSpec(shape, lambda i, _nd=nd: (0,) * _nd, pipeline_mode=pl.Buffered(1))


def _stage_a(x, wg, wu, wd, lng, lnb, wmain, bmain, wgate, bgate, vlng, vlnb):
    n = x.shape[0]
    tm = TOKEN_TILE
    row = lambda w: pl.BlockSpec((tm, w), lambda i: (i, 0))
    return pl.pallas_call(
        _stage_a_kernel,
        grid=(n // tm,),
        in_specs=[row(D_MODEL),
                  _const_spec(wg.shape), _const_spec(wu.shape), _const_spec(wd.shape),
                  _const_spec(lng.shape), _const_spec(lnb.shape),
                  _const_spec(wmain.shape), _const_spec(bmain.shape),
                  _const_spec(wgate.shape), _const_spec(bgate.shape),
                  _const_spec(vlng.shape), _const_spec(vlnb.shape)],
        out_specs=[row(D_MODEL), row(N_MAIN), row(GATE_W)],
        out_shape=[jax.ShapeDtypeStruct((n, D_MODEL), F32),
                   jax.ShapeDtypeStruct((n, N_MAIN), F32),
                   jax.ShapeDtypeStruct((n, GATE_W), F32)],
        scratch_shapes=[pltpu.VMEM((tm, D_MODEL), BF16), pltpu.VMEM((tm, D_FF), BF16)],
        compiler_params=pltpu.CompilerParams(dimension_semantics=("arbitrary",),
                                             vmem_limit_bytes=VMEM_LIMIT_BYTES),
        name="stage_a",
    )(x, wg, wu, wd, lng, lnb, wmain, bmain, wgate, bgate, vlng, vlnb)


def _stage_c_kernel(mix_ref, x1_ref, wout_ref, ln2g_ref, ln2b_ref, wg_ref, wu_ref, wd_ref, ln3g_ref, ln3b_ref,
                    y_ref, xb_ref, h_ref):
    tiles = [slice(s * TOKEN_TILE, (s + 1) * TOKEN_TILE) for s in range(STAGE_C_SUBTILES)]
    for rows in tiles:
        proj = jnp.dot(mix_ref[rows, :], wout_ref[...], preferred_element_type=F32)
        y = _layer_norm(ALPHA * x1_ref[rows, :] + proj, ln2g_ref[...], ln2b_ref[...])
        y_ref[rows, :] = y
        xb_ref[rows, :] = y.astype(BF16)
    for rows in tiles:
        ffn = _swiglu(xb_ref.at[rows, :], wg_ref, wu_ref, wd_ref, h_ref.at[rows, :])
        y_ref[rows, :] = _layer_norm(ALPHA * y_ref[rows, :] + 0.5 * ffn, ln3g_ref[...], ln3b_ref[...])


def _stage_c(mix, x1, wout, ln2g, ln2b, wg, wu, wd, ln3g, ln3b):
    n = x1.shape[0]
    tm = TOKEN_TILE * STAGE_C_SUBTILES
    row = lambda w: pl.BlockSpec((tm, w), lambda i: (i, 0))
    return pl.pallas_call(
        _stage_c_kernel,
        grid=(n // tm,),
        in_specs=[row(D_MODEL), row(D_MODEL),
                  _const_spec(wout.shape), _const_spec(ln2g.shape), _const_spec(ln2b.shape),
                  _const_spec(wg.shape), _const_spec(wu.shape), _const_spec(wd.shape),
                  _const_spec(ln3g.shape), _const_spec(ln3b.shape)],
        out_specs=row(D_MODEL),
        out_shape=jax.ShapeDtypeStruct((n, D_MODEL), F32),
        scratch_shapes=[pltpu.VMEM((tm, D_MODEL), BF16), pltpu.VMEM((tm, D_FF), BF16)],
        compiler_params=pltpu.CompilerParams(dimension_semantics=("arbitrary",),
                                             vmem_limit_bytes=VMEM_LIMIT_BYTES),
        name="stage_c",
    )(mix, x1, wout, ln2g, ln2b, wg, wu, wd, ln3g, ln3b)


def _head_cols(h, base=0):
    return slice(base + h * HEAD_DIM, base + (h + 1) * HEAD_DIM)


def _gmlp_heads(zz_ref, ws_ref, bs_ref, gmg_ref, mix_ref, mask, ones):
    for h in range(N_HEADS):
        u = zz_ref[:, _head_cols(h)]
        vn = zz_ref[:, _head_cols(h, GROUP_W)]
        w = jnp.where(mask, ws_ref[h], 0.0).astype(BF16)
        mixed = jnp.dot(w, vn.astype(BF16), preferred_element_type=F32) + bs_ref[:, h:h + 1]
        mix_ref[:, _head_cols(h)] = _head_rms_mxu(u * mixed, gmg_ref[:, _head_cols(h)], ones).astype(BF16)


def _intra_chunk(q, k, v, bcol, brow, irow, mcol, mask):
    dlog = jnp.where(mask, bcol - brow + irow, -jnp.inf)
    inter = bcol + mcol
    m_t = jnp.maximum(inter, jnp.max(dlog, axis=-1, keepdims=True))
    w_intra = jnp.exp(dlog - m_t)
    w_inter = jnp.exp(inter - m_t)
    s = lax.dot_general(q.astype(BF16), k.astype(BF16), _NT, preferred_element_type=F32) * w_intra
    sv = jnp.dot(s.astype(BF16), v.astype(BF16), preferred_element_type=F32)
    ssum = jnp.sum(s, axis=-1, keepdims=True)
    return m_t, w_inter, sv, ssum


def _conv_taps(x, prev_fn, cw_ref, cb_ref, row_in_seq):
    acc = x * cw_ref[CONV_W - 1:CONV_W, :] + cb_ref[...]
    for j in range(1, CONV_W):
        shifted = jnp.where(row_in_seq < j, prev_fn(j), pltpu.roll(x, j, 0))
        acc = acc + shifted * cw_ref[CONV_W - 1 - j:CONV_W - j, :]
    return acc


def _mix_prompt_tile(zt, gt, ws_ref, bs_ref, cw_ref, cb_ref, gmg_ref, mlg_ref,
                     mix_ref, cn_ref, m_ref, gmv_ref, ctail_ref, tail_ref):
    row = lax.broadcasted_iota(jnp.int32, (CHUNK, CHUNK), 0)
    col = lax.broadcasted_iota(jnp.int32, (CHUNK, CHUNK), 1)
    causal = col <= row
    causal_f = causal.astype(F32)
    qk_cols = slice(2 * GROUP_W, 4 * GROUP_W)
    scale = HEAD_DIM ** -0.5

    heads = range(N_HEADS)
    chunks = range(TOKEN_TILE // CHUNK)
    rows = [slice(PAD_ROWS + c * CHUNK, PAD_ROWS + (c + 1) * CHUNK) for c in chunks]
    out_rows = [slice(c * CHUNK, (c + 1) * CHUNK) for c in chunks]

    zt[0:PAD_ROWS, qk_cols] = tail_ref[...]
    ones = jnp.ones((HEAD_DIM, HEAD_DIM), BF16)
    ones_col = jnp.ones((CHUNK, HEAD_DIM), BF16)

    w_masked = [jnp.where(causal, ws_ref[h], 0.0).astype(BF16) for h in heads]
    bias = [jnp.broadcast_to(bs_ref[:, h:h + 1], (CHUNK, HEAD_DIM)) for h in heads]
    mixed = {}
    for c in chunks:
        for h in heads:
            mixed[c, h] = jnp.dot(w_masked[h], zt[rows[c], _head_cols(h, GROUP_W)].astype(BF16),
                                  preferred_element_type=F32)
        yield
    for c in chunks:
        for h in heads:
            gm = zt[rows[c], _head_cols(h)] * (mixed[c, h] + bias[h])
            mix_ref[out_rows[c], _head_cols(h)] = _head_rms_mxu(gm, gmg_ref[:, _head_cols(h)], ones).astype(BF16)
        yield

    qk = []
    for c in chunks:
        r0 = rows[c].start
        acc = zt[rows[c], qk_cols] * cw_ref[CONV_W - 1:CONV_W, :] + cb_ref[...]
        for j in range(1, CONV_W):
            acc = acc + zt[r0 - j:r0 - j + CHUNK, qk_cols] * cw_ref[CONV_W - 1 - j:CONV_W - j, :]
        qk.append(_silu(acc))
        yield

    gates, bcum, gates_t, bcum_t = [], [], [], []
    for c in chunks:
        g = gt[out_rows[c], :]
        b = jnp.dot(causal_f, _log_sigmoid(g), precision=lax.Precision.HIGHEST, preferred_element_type=F32)
        gates.append(g)
        bcum.append(b)
        gates_t.append(g.T)
        bcum_t.append(b.T)
    yield
    icol = {(c, h): gates[c][:, h:h + 1] for c in chunks for h in heads}
    bcol = {(c, h): bcum[c][:, FG_LANE + h:FG_LANE + h + 1] for c in chunks for h in heads}

    dlog, mx = {}, {}
    for c in chunks:
        for h in heads:
            irow = gates_t[c][h:h + 1, :]
            brow = bcum_t[c][FG_LANE + h:FG_LANE + h + 1, :]
            dlog[c, h] = jnp.where(causal, bcol[c, h] - brow + irow, -jnp.inf)
            mx[c, h] = jnp.max(dlog[c, h], axis=-1, keepdims=True)
        yield

    m_row = m_ref[0]
    lane = lax.broadcasted_iota(jnp.int32, m_row.shape, 1)
    m_t, w_inter, dec, wk = {}, {}, {}, {}
    for h in heads:
        m_prev = m_row[:, h:h + 1]
        for c in chunks:
            inter = bcol[c, h] + m_prev
            m_t[c, h] = jnp.maximum(inter, mx[c, h])
            w_inter[c, h] = jnp.exp(inter - m_t[c, h])
            m_new = m_t[c, h][CHUNK - 1:CHUNK, :]
            b_last = bcol[c, h][CHUNK - 1:CHUNK, :]
            dec[c, h] = jnp.exp(b_last + m_prev - m_new)
            wk[c, h] = jnp.exp(b_last - bcol[c, h] + icol[c, h] - m_new)
            m_prev = m_new
        m_row = jnp.where(lane == h, m_prev, m_row)
    m_ref[0] = m_row
    yield

    sv, upd = {}, {}
    for c in chunks:
        for h in heads:
            q = qk[c][:, _head_cols(h)]
            k = qk[c][:, _head_cols(h, GROUP_W)] * scale
            v1 = jnp.concatenate([zt[rows[c], _head_cols(h, 4 * GROUP_W)].astype(BF16), ones_col], axis=1)
            s = lax.dot_general(q.astype(BF16), k.astype(BF16), _NT, preferred_element_type=F32) * jnp.exp(
                dlog[c, h] - m_t[c, h])
            sv[c, h] = jnp.dot(s.astype(BF16), v1, preferred_element_type=F32)
            kw = k * wk[c, h]
            upd[c, h] = lax.dot_general(kw.astype(BF16), v1, _TN, preferred_element_type=F32)
        yield

    for h in heads:
        cn_state = cn_ref[0, h]
        for c in chunks:
            q = qk[c][:, _head_cols(h)]
            qcn = jnp.dot(q.astype(BF16), cn_state.astype(BF16), preferred_element_type=F32)
            nd = sv[c, h] + jnp.broadcast_to(w_inter[c, h], (CHUNK, 2 * HEAD_DIM)) * qcn
            num, den = nd[:, :HEAD_DIM], nd[:, HEAD_DIM:]
            hid = num / jnp.maximum(jnp.abs(den), jnp.exp(-m_t[c, h]))
            o = zt[rows[c], _head_cols(h, 5 * GROUP_W)]
            mix_ref[out_rows[c], _head_cols(h, GROUP_W)] = _head_rms_mxu(jax.nn.sigmoid(o) * hid,
                                                                        mlg_ref[:, _head_cols(h)], ones).astype(BF16)
            cn_state = dec[c, h] * cn_state + upd[c, h]
        cn_ref[0, h] = cn_state
        yield

    last8 = slice(PAD_ROWS + TOKEN_TILE - 8, PAD_ROWS + TOKEN_TILE)
    tail_ref[...] = zt[last8, qk_cols]
    ctail_ref[0] = zt[last8, qk_cols]
    gmv_ref[0] = zt[PAD_ROWS + TOKEN_TILE - CHUNK:PAD_ROWS + TOKEN_TILE, GROUP_W:2 * GROUP_W]


def _fused_prompt_kernel(tiles_per_seq, *refs):
    (x_ref, wg_ref, wu_ref, wd_ref, lng_ref, lnb_ref, wmain_ref, bmain_ref, wgate_ref, bgate_ref, vlng_ref, vlnb_ref,
     ws_ref, bs_ref, cw_ref, cb_ref, gmg_ref, mlg_ref,
     x1_ref, mix_ref, cn_ref, m_ref, gmv_ref, ctail_ref,
     xb_ref, h_ref, zbuf_ref, gbuf_ref, tail_ref) = refs
    i = pl.program_id(0)
    slot_w = i % 2
    slot_r = 1 - slot_w

    @pl.when(i == 0)
    def _():
        zbuf_ref[1] = jnp.zeros(zbuf_ref.shape[1:], F32)
        gbuf_ref[1] = jnp.zeros(gbuf_ref.shape[1:], F32)

    @pl.when((i == 0) | (i % tiles_per_seq == 1))
    def _():
        cn_ref[...] = jnp.zeros_like(cn_ref)
        m_ref[...] = jnp.zeros_like(m_ref)
        tail_ref[...] = jnp.zeros_like(tail_ref)

    def store_z(cols, val):
        zbuf_ref[slot_w, PAD_ROWS:PAD_ROWS + TOKEN_TILE, cols] = val

    def store_gate(val):
        gbuf_ref[slot_w] = val

    mixer = _mix_prompt_tile(zbuf_ref.at[slot_r], gbuf_ref.at[slot_r], ws_ref, bs_ref, cw_ref, cb_ref, gmg_ref,
                             mlg_ref, mix_ref, cn_ref, m_ref, gmv_ref, ctail_ref, tail_ref)

    traced = [0]

    def mixer_units(c):
        target = (MIX_SHARE_IN_FFN[0] * (c + 1) * MIX_UNITS_PER_TILE) // (MIX_SHARE_IN_FFN[1] * N_FF_CHUNKS)
        while traced[0] < target:
            next(mixer)
            traced[0] += 1

    _ffn1_and_project(x_ref, wg_ref, wu_ref, wd_ref, lng_ref, lnb_ref, wmain_ref, bmain_ref, wgate_ref, bgate_ref,
                      vlng_ref, vlnb_ref, x1_ref, xb_ref, h_ref, store_z, store_gate, mixer_units)
    for _ in mixer:
        traced[0] += 1
    assert traced[0] == MIX_UNITS_PER_TILE - 1


def _stage_a_mix_prompt(x, a_params, ws, bs_cols, cw, cb, gmg, mlg, batch, seq):
    tm = TOKEN_TILE
    n_tiles = (batch * seq) // tm
    tiles_per_seq = seq // tm
    mix_params = (ws, bs_cols, cw, cb, gmg, mlg)
    tile_in = lambda i: (jnp.minimum(i, n_tiles - 1), 0)
    tile_out = lambda i: (jnp.maximum(i - 1, 0), 0)
    seq_of = lambda i: jnp.maximum(i - 1, 0) // tiles_per_seq
    return pl.pallas_call(
        functools.partial(_fused_prompt_kernel, tiles_per_seq),
        grid=(n_tiles + 1,),
        in_specs=[pl.BlockSpec((tm, D_MODEL), tile_in)] + [_const_spec(p.shape) for p in (*a_params, *mix_params)],
        out_specs=[pl.BlockSpec((tm, D_MODEL), tile_in),
                   pl.BlockSpec((tm, D_MODEL), tile_out),
                   pl.BlockSpec((1, N_HEADS, HEAD_DIM, 2 * HEAD_DIM), lambda i: (seq_of(i), 0, 0, 0)),
                   pl.BlockSpec((1, 1, GATE_W), lambda i: (seq_of(i), 0, 0)),
                   pl.BlockSpec((1, CHUNK, GROUP_W), lambda i: (seq_of(i), 0, 0)),
                   pl.BlockSpec((1, 8, 2 * GROUP_W), lambda i: (seq_of(i), 0, 0))],
        out_shape=[jax.ShapeDtypeStruct((batch * seq, D_MODEL), F32),
                   jax.ShapeDtypeStruct((batch * seq, D_MODEL), BF16),
                   jax.ShapeDtypeStruct((batch, N_HEADS, HEAD_DIM, 2 * HEAD_DIM), F32),
                   jax.ShapeDtypeStruct((batch, 1, GATE_W), F32),
                   jax.ShapeDtypeStruct((batch, CHUNK, GROUP_W), F32),
                   jax.ShapeDtypeStruct((batch, 8, 2 * GROUP_W), F32)],
        scratch_shapes=[pltpu.VMEM((tm, D_MODEL), BF16),
                        pltpu.VMEM((tm, D_FF), BF16),
                        pltpu.VMEM((2, PAD_ROWS + tm, N_MAIN), F32),
                        pltpu.VMEM((2, tm, GATE_W), F32),
                        pltpu.VMEM((8, 2 * GROUP_W), F32)],
        compiler_params=pltpu.CompilerParams(dimension_semantics=("arbitrary",),
                                             vmem_limit_bytes=VMEM_LIMIT_BYTES),
        name="stage_a_mix_prompt",
    )(x, *a_params, *mix_params)


def _mix_sample_kernel(zz_ref, gate_ref, ws_ref, bs_ref, cw_ref, cb_ref, gmg_ref, mlg_ref, prev_ref,
                       c0_ref, n0_ref, m0_ref, mix_ref, c_ref, n_ref, mt_ref):
    row = lax.broadcasted_iota(jnp.int32, (CHUNK, CHUNK), 0)
    col = lax.broadcasted_iota(jnp.int32, (CHUNK, CHUNK), 1)
    seq_shift = DEC_SEQ.bit_length() - 1
    mask = (col <= row) & ((col >> seq_shift) == (row >> seq_shift))
    last_sel = (col == (row | (DEC_SEQ - 1))).astype(F32)

    ones = jnp.ones((HEAD_DIM, HEAD_DIM), BF16)
    _gmlp_heads(zz_ref, ws_ref, bs_ref, gmg_ref, mix_ref, mask, ones)

    x = zz_ref[:, 2 * GROUP_W:4 * GROUP_W]
    row_w = lax.broadcasted_iota(jnp.int32, (CHUNK, 2 * GROUP_W), 0) & (DEC_SEQ - 1)
    prev = prev_ref[...]
    qk = _silu(_conv_taps(x, lambda j: pltpu.roll(prev, CHUNK - DEC_SEQ + j, 0), cw_ref, cb_ref, row_w))

    gates = gate_ref[...]
    bcum = jnp.dot(mask.astype(F32), _log_sigmoid(gates), precision=lax.Precision.HIGHEST,
                   preferred_element_type=F32)
    gates_t = gates.T
    bcum_t = bcum.T
    lane = lax.broadcasted_iota(jnp.int32, (CHUNK, GATE_W), 1)
    mt_all = jnp.zeros((CHUNK, GATE_W), F32)
    scale = HEAD_DIM ** -0.5

    for h in range(N_HEADS):
        q = qk[:, _head_cols(h)]
        k = qk[:, _head_cols(h, GROUP_W)] * scale
        v = zz_ref[:, _head_cols(h, 4 * GROUP_W)]
        o = zz_ref[:, _head_cols(h, 5 * GROUP_W)]
        icol = gates[:, h:h + 1]
        bcol = bcum[:, FG_LANE + h:FG_LANE + h + 1]
        irow = gates_t[h:h + 1, :]
        brow = bcum_t[FG_LANE + h:FG_LANE + h + 1, :]
        m_prev = m0_ref[:, h:h + 1]

        m_t, w_inter, sv, ssum = _intra_chunk(q, k, v, bcol, brow, irow, m_prev, mask)
        qb = q.astype(BF16)
        qc_rows, qn_rows = [], []
        for s in range(SEQ_PER_TILE):
            rows = slice(s * DEC_SEQ, (s + 1) * DEC_SEQ)
            qc_rows.append(jnp.dot(qb[rows], c0_ref[s, h].astype(BF16), preferred_element_type=F32))
            qn_rows.append(jnp.sum(q[rows] * n0_ref[s, h:h + 1, :], axis=-1, keepdims=True))
        qc = jnp.concatenate(qc_rows, axis=0)
        qn = jnp.concatenate(qn_rows, axis=0)
        num = sv + w_inter * qc
        den = ssum + w_inter * qn
        hid = num / jnp.maximum(jnp.abs(den), jnp.exp(-m_t))
        mix_ref[:, _head_cols(h, GROUP_W)] = _head_rms_mxu(jax.nn.sigmoid(o) * hid, mlg_ref[:, _head_cols(h)],
                                                           ones).astype(BF16)

        packed = jnp.where(lane == 0, m_t, jnp.where(lane == 1, bcol, 0.0))
        lastv = jnp.dot(last_sel, packed, precision=lax.Precision.HIGHEST, preferred_element_type=F32)
        m_new = lastv[:, 0:1]
        b_last = lastv[:, 1:2]
        dec = jnp.exp(b_last + m_prev - m_new)
        kw = k * jnp.exp(b_last - bcol + icol - m_new)
        kwb = kw.astype(BF16)
        vb = v.astype(BF16)
        for s in range(SEQ_PER_TILE):
            rows = slice(s * DEC_SEQ, (s + 1) * DEC_SEQ)
            dec_s = dec[s * DEC_SEQ:s * DEC_SEQ + 1, :]
            c_ref[s, h] = dec_s * c0_ref[s, h] + lax.dot_general(kwb[rows], vb[rows], _TN,
                                                                preferred_element_type=F32)
            n_ref[s, h:h + 1, :] = dec_s * n0_ref[s, h:h + 1, :] + jnp.sum(kw[rows], axis=0, keepdims=True)
        mt_all = jnp.where(lane == h, m_t, mt_all)

    mt_ref[...] = mt_all


def _mix_sample(zz, gates, ws_t, bs_cols, cw, cb, gmg, mlg, prev, c0, n0, m0_tok):
    n = zz.shape[0]
    n_tiles = n // CHUNK
    row = lambda w: pl.BlockSpec((CHUNK, w), lambda i: (i, 0))
    const = lambda shape: pl.BlockSpec(shape, lambda i, _nd=len(shape): (0,) * _nd)
    c_spec = pl.BlockSpec((SEQ_PER_TILE, N_HEADS, HEAD_DIM, HEAD_DIM), lambda i: (i, 0, 0, 0))
    n_spec = pl.BlockSpec((SEQ_PER_TILE, N_HEADS, HEAD_DIM), lambda i: (i, 0, 0))
    return pl.pallas_call(
        _mix_sample_kernel,
        grid=(n_tiles,),
        in_specs=[row(N_MAIN), row(GATE_W), const(ws_t.shape), const(bs_cols.shape), const(cw.shape),
                  const(cb.shape), const(gmg.shape), const(mlg.shape), row(2 * GROUP_W),
                  c_spec, n_spec, row(N_HEADS)],
        out_specs=[row(D_MODEL), c_spec, n_spec, row(GATE_W)],
        out_shape=[jax.ShapeDtypeStruct((n, D_MODEL), BF16),
                   jax.ShapeDtypeStruct(c0.shape, F32),
                   jax.ShapeDtypeStruct(n0.shape, F32),
                   jax.ShapeDtypeStruct((n, GATE_W), F32)],
        compiler_params=pltpu.CompilerParams(dimension_semantics=("arbitrary",),
                                             vmem_limit_bytes=VMEM_LIMIT_BYTES),
        name="mix_sample",
    )(zz, gates, ws_t, bs_cols, cw, cb, gmg, mlg, prev, c0, n0, m0_tok)


def _ffn_weights(wg, wu, wd):
    return wg.astype(BF16), wu.astype(BF16), wd.astype(BF16)


def _gate_columns(w):
    out = jnp.zeros(w.shape[:-1] + (GATE_W,), w.dtype)
    out = out.at[..., 0:N_HEADS].set(w[..., N_MAIN:N_MAIN + N_HEADS])
    return out.at[..., FG_LANE:FG_LANE + N_HEADS].set(w[..., N_MAIN + N_HEADS:N_MAIN + 2 * N_HEADS])


def kernel(x_prompt, x_sample, state_conv, state_C, state_n, state_m, ffn1_wg, ffn1_wu, ffn1_wd, ln1_g, ln1_b, w_in, b_in, gm_ln_g, gm_ln_b, gm_ws, gm_bs, conv_w, conv_b, gm_out_g, ml_out_g, w_out, ln2_g, ln2_b, ffn2_wg, ffn2_wu, ffn2_wd, ln3_g, ln3_b):
    depth = ffn1_wg.shape[0]
    bp, seq, _ = x_prompt.shape
    bs, dec_seq, _ = x_sample.shape
    assert dec_seq == DEC_SEQ and seq % TOKEN_TILE == 0
    assert (bs * dec_seq) % (TOKEN_TILE * STAGE_C_SUBTILES) == 0 and (bp * seq) % (TOKEN_TILE * STAGE_C_SUBTILES) == 0
    y_p = x_prompt.reshape(bp * seq, D_MODEL)
    y_s = x_sample.reshape(bs * dec_seq, D_MODEL)
    outs = []
    for l in range(depth):
        f1 = _ffn_weights(ffn1_wg[l], ffn1_wu[l], ffn1_wd[l])
        f2 = _ffn_weights(ffn2_wg[l], ffn2_wu[l], ffn2_wd[l])
        row = lambda a: a.reshape(1, -1)
        a_params = (*f1, row(ln1_g[l]), row(ln1_b[l]),
                    w_in[l].astype(BF16), row(b_in[l]),
                    _gate_columns(w_in[l]).astype(BF16), row(_gate_columns(b_in[l])),
                    row(gm_ln_g[l]), row(gm_ln_b[l]))
        c_params = (w_out[l].astype(BF16), row(ln2_g[l]), row(ln2_b[l]), *f2, row(ln3_g[l]), row(ln3_b[l]))
        mix_params = (conv_w[l], row(conv_b[l]), row(gm_out_g[l]), row(ml_out_g[l]))

        x1_p, mix_p, cn_p, m_p, gmv_p, ctail_p = _stage_a_mix_prompt(
            y_p, a_params, gm_ws[l], jnp.transpose(gm_bs[l]), *mix_params, bp, seq)
        x1_s, zz_s, g_s = _stage_a(y_s, *a_params)

        ws_t = jnp.tile(gm_ws[l][:, :DEC_SEQ, :DEC_SEQ], (1, SEQ_PER_TILE, SEQ_PER_TILE))
        bs_t = jnp.tile(jnp.transpose(gm_bs[l][:, :DEC_SEQ]), (SEQ_PER_TILE, 1))
        prev = jnp.pad(state_conv[l], ((0, 0), (DEC_SEQ - (CONV_W - 1), 0), (0, 0))).reshape(bs * DEC_SEQ, 2 * GROUP_W)
        m0_tok = jnp.repeat(state_m[l], DEC_SEQ, axis=0)
        mix_s, c_s, n_s, mt_s = _mix_sample(zz_s, g_s, ws_t, bs_t, *mix_params, prev, state_C[l], state_n[l], m0_tok)

        y_p = _stage_c(mix_p, x1_p, *c_params)
        y_s = _stage_c(mix_s, x1_s, *c_params)

        zz_s3 = zz_s.reshape(bs, dec_seq, N_MAIN)
        outs.append((
            gmv_p.reshape(bp, CHUNK, N_HEADS, HEAD_DIM),
            zz_s3[:, :, GROUP_W:2 * GROUP_W].reshape(bs, dec_seq, N_HEADS, HEAD_DIM),
            ctail_p[:, 8 - (CONV_W - 1):, :],
            zz_s3[:, dec_seq - (CONV_W - 1):, 2 * GROUP_W:4 * GROUP_W],
            cn_p[..., :HEAD_DIM], c_s, cn_p[..., HEAD_DIM], n_s,
            m_p[:, 0, :N_HEADS],
            mt_s.reshape(bs, dec_seq, GATE_W)[:, dec_seq - 1, :N_HEADS],
        ))
    stacked = [jnp.stack(a) for a in zip(*outs)]
    return (y_p.reshape(bp, seq, D_MODEL), y_s.reshape(bs, dec_seq, D_MODEL), *stacked)
```
